```python
import math
import jax, jax.numpy as jnp
from jax import lax
import numpy as np

D_MODEL = 1024
BATCH = 4
SEQ = 8192
DEPTH = 4

GRID_W = 64
CTX_LEN = 256
N_MIXERS = 2
EPS = 1e-6

RET_HEADS = 4
RET_DK = 256
RET_DV = 512
RET_QK = RET_HEADS * RET_DK
RET_V = RET_HEADS * RET_DV
RET_IN = 2 * RET_QK + 2 * RET_V
RET_CHUNK = 128
ROPE_BASE = 10000.0

LRU_WIDTH = 1280
LRU_BLOCKS = 10
LRU_BW = LRU_WIDTH // LRU_BLOCKS
CONV_W = 4
LRU_C = 8.0

N_RET = (DEPTH + 1) // 2
N_LRU = DEPTH // 2

kernel_name = "hybrid_retention_rglru_prefix_dit"


def rms_norm(t, g):
    tf = t.astype(jnp.float32)
    y = tf * lax.rsqrt(jnp.mean(tf * tf, axis=-1, keepdims=True) + EPS)
    return (y * g.astype(jnp.float32)).astype(t.dtype)


def axial_rope_tables(n_tok):
    n_rows = n_tok // GRID_W
    row = jnp.repeat(jnp.arange(n_rows, dtype=jnp.float32), GRID_W)
    col = jnp.tile(jnp.arange(GRID_W, dtype=jnp.float32), n_rows)
    n_freq = RET_DK // 4
    inv = ROPE_BASE ** (-jnp.arange(n_freq, dtype=jnp.float32) / n_freq)
    ang = jnp.concatenate([row[:, None] * inv, col[:, None] * inv], axis=-1)
    return jnp.cos(ang), jnp.sin(ang)


def apply_rope(t, cos, sin):
    te, to = t[..., 0::2], t[..., 1::2]
    return jnp.stack([te * cos - to * sin, te * sin + to * cos], axis=-1).reshape(t.shape)


def split_heads(t, dh):
    b, n, _ = t.shape
    return t.reshape(b, n, -1, dh).transpose(0, 2, 1, 3).astype(jnp.float32)


def retention_scan(q, k, v, log_g, state0):
    b, h, n_tok, dk = q.shape
    dv = v.shape[-1]
    n_chunk = n_tok // RET_CHUNK
    idx = jnp.arange(RET_CHUNK, dtype=jnp.float32)
    rel = idx[:, None] - idx[None, :]
    intra = jnp.where(rel >= 0, jnp.exp(log_g[:, None, None] * jnp.maximum(rel, 0.0)), 0.0)
    q_dec = jnp.exp(log_g[:, None] * (idx + 1.0))[:, :, None]
    k_dec = jnp.exp(log_g[:, None] * (RET_CHUNK - 1.0 - idx))[:, :, None]
    chunk_dec = jnp.exp(log_g * RET_CHUNK)[:, None, None]

    def chunks(a):
        return jnp.moveaxis(a.reshape(b, h, n_chunk, RET_CHUNK, a.shape[-1]), 2, 0)

    def step(state, qkv):
        qc, kc, vc = qkv
        s = jnp.einsum('bhcd,bhsd->bhcs', qc, kc) * intra
        o = (jnp.einsum('bhcs,bhse->bhce', s, vc)
             + jnp.einsum('bhcd,bhde->bhce', qc * q_dec, state))
        state = state * chunk_dec + jnp.einsum('bhsd,bhse->bhde', kc * k_dec, vc)
        return state, o

    state, o = lax.scan(step, state0, (chunks(q), chunks(k), chunks(v)))
    o = jnp.moveaxis(o, 0, 2).reshape(b, h, n_tok, dv)
    return o, state


def retention_output(o, g, gn_g, w_out):
    mu = jnp.mean(o, axis=-1, keepdims=True)
    var = jnp.mean(jnp.square(o - mu), axis=-1, keepdims=True)
    o = (o - mu) * lax.rsqrt(var + EPS)
    b, h, n_tok, dv = o.shape
    o = o.transpose(0, 2, 1, 3).reshape(b, n_tok, h * dv) * gn_g.astype(jnp.float32)
    return (o * jax.nn.silu(g.astype(jnp.float32))).astype(w_out.dtype) @ w_out


def retention_mixer(h_lat, h_ctx, cos, sin, w_in, log_decay, gn_g, w_out, need_ctx):
    log_g = -jnp.abs(log_decay.astype(jnp.float32))
    scale = RET_DK ** -0.5

    def project(hh):
        p = hh @ w_in
        q = split_heads(p[..., :RET_QK], RET_DK)
        k = split_heads(p[..., RET_QK:2 * RET_QK], RET_DK) * scale
        v = split_heads(p[..., 2 * RET_QK:2 * RET_QK + RET_V], RET_DV)
        g = p[..., 2 * RET_QK + RET_V:]
        return q, k, v, g

    q_l, k_l, v_l, g_l = project(h_lat)
    q_l, k_l = apply_rope(q_l, cos, sin), apply_rope(k_l, cos, sin)
    q_c, k_c, v_c, g_c = project(h_ctx)

    b = h_lat.shape[0]
    zero = jnp.zeros((b, RET_HEADS, RET_DK, RET_DV), jnp.float32)
    flip = lambda a: jnp.flip(a, axis=2)
    oc_f, sc_f = retention_scan(q_c, k_c, v_c, log_g[0], zero)
    oc_b, sc_b = retention_scan(flip(q_c), flip(k_c), flip(v_c), log_g[1], zero)
    ol_f, _ = retention_scan(q_l, k_l, v_l, log_g[0], sc_f)
    ol_b, _ = retention_scan(flip(q_l), flip(k_l), flip(v_l), log_g[1], sc_b)
    y_lat = retention_output(ol_f + flip(ol_b), g_l, gn_g, w_out)
    y_ctx = retention_output(oc_f + flip(oc_b), g_c, gn_g, w_out) if need_ctx else None
    return y_lat, y_ctx


def conv_centred(t, w, bias):
    n_tok = t.shape[1]
    left = CONV_W // 2
    tp = jnp.pad(t, ((0, 0), (left, CONV_W - 1 - left), (0, 0)))
    out = bias.astype(jnp.float32)
    for j in range(CONV_W):
        out = out + tp[:, j:j + n_tok] * w[j].astype(jnp.float32)
    return out


def rglru_coeffs(xc, w_a, b_a, w_x, b_x, lam):
    b, n_tok, wd = xc.shape
    xb = xc.reshape(b, n_tok, LRU_BLOCKS, LRU_BW)
    r = jax.nn.sigmoid(jnp.einsum('btni,nij->btnj', xb, w_a.astype(jnp.float32)).reshape(b, n_tok, wd)
                       + b_a.astype(jnp.float32))
    gi = jax.nn.sigmoid(jnp.einsum('btni,nij->btnj', xb, w_x.astype(jnp.float32)).reshape(b, n_tok, wd)
                        + b_x.astype(jnp.float32))
    log_a = -LRU_C * r * jax.nn.softplus(-lam.astype(jnp.float32))
    a = jnp.exp(log_a)
    u = jnp.sqrt(-jnp.expm1(2.0 * log_a)) * gi * xc
    return a, u


def linear_scan(a, u, h0):
    def comb(e1, e2):
        a1, u1 = e1
        a2, u2 = e2
        return a1 * a2, a2 * u1 + u2
    a_cum, hs = lax.associative_scan(comb, (a, u), axis=1)
    return hs + a_cum * h0[:, None, :]


def lru_mixer(h_lat, h_ctx, w_in, conv_w, conv_b, w_a, b_a, w_x, b_x, lam, w_out, need_ctx):
    def branch(hh):
        p = hh @ w_in
        xr = p[..., :LRU_WIDTH].astype(jnp.float32)
        return conv_centred(xr, conv_w, conv_b), p[..., LRU_WIDTH:]

    xc_l, g_l = branch(h_lat)
    xc_c, g_c = branch(h_ctx)
    b = h_lat.shape[0]
    zero = jnp.zeros((b, LRU_WIDTH), jnp.float32)
    flip = lambda t: jnp.flip(t, axis=1)
    hl_sum = 0.0
    hc_sum = 0.0
    for d in range(2):
        a_c, u_c = rglru_coeffs(xc_c, w_a[d], b_a[d], w_x[d], b_x[d], lam[d])
        a_l, u_l = rglru_coeffs(xc_l, w_a[d], b_a[d], w_x[d], b_x[d], lam[d])
        if d == 1:
            a_c, u_c, a_l, u_l = flip(a_c), flip(u_c), flip(a_l), flip(u_l)
        hc = linear_scan(a_c, u_c, zero)
        hl = linear_scan(a_l, u_l, hc[:, -1])
        if d == 1:
            hc, hl = flip(hc), flip(hl)
        hl_sum = hl_sum + hl
        hc_sum = hc_sum + hc
    y_lat = (hl_sum * jax.nn.silu(g_l.astype(jnp.float32))).astype(w_out.dtype) @ w_out
    y_ctx = ((hc_sum * jax.nn.silu(g_c.astype(jnp.float32))).astype(w_out.dtype) @ w_out) if need_ctx else None
    return y_lat, y_ctx


def setup_inputs(seed: int = 0) -> dict:
    key = jax.random.key(seed)
    ks = jax.random.split(key, 24)
    f32 = jnp.float32
    nrm = lambda k, shape, s: jax.random.normal(k, shape, f32) * s
    x = nrm(ks[0], (BATCH, SEQ, D_MODEL), 1.0)
    c = nrm(ks[1], (BATCH, D_MODEL), 1.0)
    ctx = nrm(ks[2], (BATCH, CTX_LEN, D_MODEL), 1.0)
    c_ctx = nrm(ks[3], (D_MODEL,), 1.0)
    mod_w = nrm(ks[4], (DEPTH, D_MODEL, 3 * D_MODEL), D_MODEL ** -0.5)
    mod_b = nrm(ks[5], (DEPTH, 3 * D_MODEL), 0.02)
    norm_pre = 1.0 + nrm(ks[6], (DEPTH, D_MODEL), 0.02)
    norm_post = 1.0 + nrm(ks[7], (DEPTH, D_MODEL), 0.02)
    ret_w_in = nrm(ks[8], (N_RET, D_MODEL, RET_IN), D_MODEL ** -0.5)
    base = jnp.log1p(-(2.0 ** (-5.0 - jnp.arange(RET_HEADS, dtype=f32))))
    ret_log_decay = base * (1.0 + nrm(ks[9], (N_RET, 2, RET_HEADS), 0.1))
    ret_gn = 1.0 + nrm(ks[10], (N_RET, RET_V), 0.02)
    ret_w_out = nrm(ks[11], (N_RET, RET_V, D_MODEL), RET_V ** -0.5)
    lru_w_in = nrm(ks[12], (N_LRU, D_MODEL, 2 * LRU_WIDTH), D_MODEL ** -0.5)
    lru_conv_w = nrm(ks[13], (N_LRU, CONV_W, LRU_WIDTH), CONV_W ** -0.5)
    lru_conv_b = nrm(ks[14], (N_LRU, LRU_WIDTH), 0.01)
    lru_w_a = nrm(ks[15], (N_LRU, 2, LRU_BLOCKS, LRU_BW, LRU_BW), LRU_BW ** -0.5)
    lru_b_a = nrm(ks[16], (N_LRU, 2, LRU_WIDTH), 0.01)
    lru_w_x = nrm(ks[17], (N_LRU, 2, LRU_BLOCKS, LRU_BW, LRU_BW), LRU_BW ** -0.5)
    lru_b_x = nrm(ks[18], (N_LRU, 2, LRU_WIDTH), 0.01)
    a_pow_c = jax.random.uniform(ks[19], (N_LRU, 2, LRU_WIDTH), f32, 0.9, 0.999)
    a_base = a_pow_c ** (1.0 / LRU_C)
    lru_lambda = jnp.log(a_base) - jnp.log1p(-a_base)
    lru_w_out = nrm(ks[20], (N_LRU, LRU_WIDTH, D_MODEL), LRU_WIDTH ** -0.5)
    return {"x": x, "c": c, "ctx": ctx, "c_ctx": c_ctx,
            "mod_w": mod_w, "mod_b": mod_b, "norm_pre": norm_pre, "norm_post": norm_post,
            "ret_w_in": ret_w_in, "ret_log_decay": ret_log_decay, "ret_gn": ret_gn, "ret_w_out": ret_w_out,
            "lru_w_in": lru_w_in, "lru_conv_w": lru_conv_w, "lru_conv_b": lru_conv_b,
            "lru_w_a": lru_w_a, "lru_b_a": lru_b_a, "lru_w_x": lru_w_x, "lru_b_x": lru_b_x,
            "lru_lambda": lru_lambda, "lru_w_out": lru_w_out}


def reference(x, c, ctx, c_ctx, mod_w, mod_b, norm_pre, norm_post,
              ret_w_in, ret_log_decay, ret_gn, ret_w_out,
              lru_w_in, lru_conv_w, lru_conv_b, lru_w_a, lru_b_a, lru_w_x, lru_b_x,
              lru_lambda, lru_w_out):
    n_tok = x.shape[1]
    cos, sin = axial_rope_tables(n_tok)
    s_ctx = ctx
    act_l = jax.nn.silu(c)
    act_c = jax.nn.silu(c_ctx)
    for i in range(DEPTH):
        need_ctx = i < DEPTH - 1
        shift_l, scale_l, gate_l = jnp.split(act_l @ mod_w[i] + mod_b[i], 3, axis=-1)
        shift_c, scale_c, gate_c = jnp.split(act_c @ mod_w[i] + mod_b[i], 3, axis=-1)
        h_l = rms_norm(x, norm_pre[i]) * (1.0 + scale_l[:, None, :]) + shift_l[:, None, :]
        h_c = rms_norm(s_ctx, norm_pre[i]) * (1.0 + scale_c) + shift_c
        j = i // N_MIXERS
        if i % N_MIXERS == 0:
            y_l, y_c = retention_mixer(h_l, h_c, cos, sin, ret_w_in[j], ret_log_decay[j],
                                       ret_gn[j], ret_w_out[j], need_ctx)
        else:
            y_l, y_c = lru_mixer(h_l, h_c, lru_w_in[j], lru_conv_w[j], lru_conv_b[j],
                                 lru_w_a[j], lru_b_a[j], lru_w_x[j], lru_b_x[j],
                                 lru_lambda[j], lru_w_out[j], need_ctx)
        x = x + gate_l[:, None, :] * rms_norm(y_l, norm_post[i])
        if need_ctx:
            s_ctx = s_ctx + gate_c * rms_norm(y_c, norm_post[i])
    return x
```

```python
import functools

import jax
import jax.numpy as jnp
from jax import lax
from jax.experimental import pallas as pl
from jax.experimental.pallas import tpu as pltpu

F32 = jnp.float32
BF16 = jnp.bfloat16

EPS = 1e-6
GRID_W = 64
ROPE_BASE = 10000.0

RET_HEADS = 4
RET_DK = 256
RET_DV = 512
RET_QK = RET_HEADS * RET_DK
RET_V = RET_HEADS * RET_DV
RET_IN = 2 * RET_QK + 2 * RET_V
HALF_DK = RET_DK // 2

LRU_BLOCKS = 10
LRU_BW = 128
LRU_WIDTH = LRU_BLOCKS * LRU_BW
CONV_W = 4
LRU_C = 8.0

LANES = 128
SUBLANES = 8
MIB = 1024 * 1024

RET_CHUNK = 256
ROW_TILE = 512
LRU_TILE = 512
HALO_ROWS = 16


def _params(n_axes, vmem_mib):
    return pltpu.CompilerParams(
        dimension_semantics=("arbitrary",) * n_axes,
        vmem_limit_bytes=vmem_mib * MIB)


def _smem_spec():
    return pl.BlockSpec(memory_space=pltpu.SMEM)


def _mod_kernel(cc_ref, w_ref, b_ref, o_ref):
    cc = cc_ref[...]
    act = cc * jax.nn.sigmoid(cc)
    o_ref[...] = jnp.dot(act.astype(BF16), w_ref[...].astype(BF16),
                         preferred_element_type=F32) + b_ref[...]


def _modulation(cc, mod_w, mod_b):
    depth, d, d3 = mod_w.shape
    rows = cc.shape[0]
    return pl.pallas_call(
        _mod_kernel,
        grid=(depth, d3 // d),
        in_specs=[pl.BlockSpec((rows, d), lambda i, j: (0, 0)),
                  pl.BlockSpec((None, d, d), lambda i, j: (i, 0, j)),
                  pl.BlockSpec((None, 1, d), lambda i, j: (i, 0, j))],
        out_specs=pl.BlockSpec((None, rows, d), lambda i, j: (i, 0, j)),
        out_shape=jax.ShapeDtypeStruct((depth, rows, d3), F32),
        compiler_params=_params(2, 32),
        name="modulation",
    )(cc, mod_w, mod_b.reshape(depth, 1, d3))


def _norm_modulate(x_ref, mul_ref, add_ref, h_scr):
    x = x_ref[...]
    ms = jnp.mean(x * x, axis=-1, keepdims=True)
    h = x * lax.rsqrt(ms + EPS) * mul_ref[...] + add_ref[...]
    h_scr[...] = h.astype(BF16)


def _ret_inproj_kernel(x_ref, mul_ref, add_ref, w_ref, cos_ref, sin_ref, o_ref, h_scr, *, rope):
    _norm_modulate(x_ref, mul_ref, add_ref, h_scr)
    hb = h_scr[...]
    scale = RET_DK ** -0.5
    for j in range(2 * RET_HEADS):
        cols = slice(j * RET_DK, (j + 1) * RET_DK)
        acc = jnp.dot(hb, w_ref[:, cols], preferred_element_type=F32)
        if j >= RET_HEADS:
            acc = acc * scale
        if rope:
            e, o = acc[:, :HALF_DK], acc[:, HALF_DK:]
            c, s = cos_ref[...], sin_ref[...]
            acc = jnp.concatenate([e * c - o * s, e * s + o * c], axis=1)
        o_ref[:, cols] = acc.astype(BF16)
    for j in range(2 * RET_QK // RET_DV, RET_IN // RET_DV):
        cols = slice(j * RET_DV, (j + 1) * RET_DV)
        o_ref[:, cols] = jnp.dot(hb, w_ref[:, cols], preferred_element_type=F32).astype(BF16)


def _ret_inproj(x, mul, add, w, cos, sin, *, rope, tm):
    b, t, d = x.shape
    kern = functools.partial(_ret_inproj_kernel, rope=rope)
    return pl.pallas_call(
        kern,
        grid=(b, t // tm),
        in_specs=[pl.BlockSpec((None, tm, d), lambda i, j: (i, j, 0)),
                  pl.BlockSpec((None, 1, d), lambda i, j: (i, 0, 0)),
                  pl.BlockSpec((None, 1, d), lambda i, j: (i, 0, 0)),
                  pl.BlockSpec((d, RET_IN), lambda i, j: (0, 0)),
                  pl.BlockSpec((tm, HALF_DK), lambda i, j: (j, 0)),
                  pl.BlockSpec((tm, HALF_DK), lambda i, j: (j, 0))],
        out_specs=pl.BlockSpec((None, tm, RET_IN), lambda i, j: (i, j, 0)),
        out_shape=jax.ShapeDtypeStruct((b, t, RET_IN), BF16),
        scratch_shapes=[pltpu.VMEM((tm, d), BF16)],
        compiler_params=_params(2, 56),
        name="ret_inproj_rope" if rope else "ret_inproj",
    )(x, mul, add, w, cos, sin)


def _lru_inproj_kernel(x_ref, mul_ref, add_ref, w_ref, xr_ref, g_ref, h_scr):
    _norm_modulate(x_ref, mul_ref, add_ref, h_scr)
    hb = h_scr[...]
    for j in range(LRU_WIDTH // (2 * LANES)):
        cols = slice(j * 2 * LANES, (j + 1) * 2 * LANES)
        xr_ref[:, cols] = jnp.dot(hb, w_ref[:, cols], preferred_element_type=F32)
        gcols = slice(LRU_WIDTH + j * 2 * LANES, LRU_WIDTH + (j + 1) * 2 * LANES)
        g_ref[:, cols] = jnp.dot(hb, w_ref[:, gcols], preferred_element_type=F32).astype(BF16)


def _lru_inproj(x, mul, add, w, *, tm):
    b, t, d = x.shape
    return pl.pallas_call(
        _lru_inproj_kernel,
        grid=(b, t // tm),
        in_specs=[pl.BlockSpec((None, tm, d), lambda i, j: (i, j, 0)),
                  pl.BlockSpec((None, 1, d), lambda i, j: (i, 0, 0)),
                  pl.BlockSpec((None, 1, d), lambda i, j: (i, 0, 0)),
                  pl.BlockSpec((d, 2 * LRU_WIDTH), lambda i, j: (0, 0))],
        out_specs=[pl.BlockSpec((None, tm, LRU_WIDTH), lambda i, j: (i, j, 0)),
                   pl.BlockSpec((None, tm, LRU_WIDTH), lambda i, j: (i, j, 0))],
        out_shape=[jax.ShapeDtypeStruct((b, t, LRU_WIDTH), F32),
                   jax.ShapeDtypeStruct((b, t, LRU_WIDTH), BF16)],
        scratch_shapes=[pltpu.VMEM((tm, d), BF16)],
        compiler_params=_params(2, 48),
        name="lru_inproj",
    )(x, mul, add, w)


def _outproj_kernel(y_ref, w_ref, x_ref, gp_ref, o_ref):
    z = jnp.dot(y_ref[...], w_ref[...], preferred_element_type=F32)
    ms = jnp.mean(z * z, axis=-1, keepdims=True)
    o_ref[...] = x_ref[...] + z * lax.rsqrt(ms + EPS) * gp_ref[...]


def _outproj(y, w, x, gp, *, tm):
    b, t, d = x.shape
    k = y.shape[-1]
    return pl.pallas_call(
        _outproj_kernel,
        grid=(b, t // tm),
        in_specs=[pl.BlockSpec((None, tm, k), lambda i, j: (i, j, 0)),
                  pl.BlockSpec((k, d), lambda i, j: (0, 0)),
                  pl.BlockSpec((None, tm, d), lambda i, j: (i, j, 0)),
                  pl.BlockSpec((None, 1, d), lambda i, j: (i, 0, 0))],
        out_specs=pl.BlockSpec((None, tm, d), lambda i, j: (i, j, 0)),
        out_shape=jax.ShapeDtypeStruct((b, t, d), F32),
        compiler_params=_params(2, 48),
        name="outproj",
    )(y, w, x, gp)


def _row_scale(x, dec):
    n = x.shape[1] // LANES
    return jnp.concatenate([x[:, i * LANES:(i + 1) * LANES] * dec for i in range(n)], axis=1)


def _chunk_pos():
    return lax.broadcasted_iota(jnp.int32, (RET_CHUNK, LANES), 0).astype(F32)


def _ret_bwd_kernel(lg_ref, q_ref, k_ref, v_ref, s0_ref, ob_ref, sfin_ref, s_scr):
    c = pl.program_id(1)

    @pl.when(c == 0)
    def _():
        s_scr[...] = s0_ref[...]

    pos = _chunk_pos()
    for h in range(RET_HEADS):
        lg = lg_ref[1, h]
        q = q_ref[:, h * RET_DK:(h + 1) * RET_DK]
        k = k_ref[:, h * RET_DK:(h + 1) * RET_DK]
        v = v_ref[:, h * RET_DV:(h + 1) * RET_DV]
        st = s_scr[h]
        qs = jnp.dot(q, st.astype(BF16), preferred_element_type=F32)
        ob_ref[:, h * RET_DV:(h + 1) * RET_DV] = _row_scale(qs, jnp.exp(lg * (RET_CHUNK - pos)))
        kd = _row_scale(k.astype(F32), jnp.exp(lg * pos)).astype(BF16)
        kv = lax.dot_general(kd, v, (((0,), (0,)), ((), ())), preferred_element_type=F32)
        s_scr[h] = st * jnp.exp(lg * RET_CHUNK) + kv

    @pl.when(c == pl.num_programs(1) - 1)
    def _():
        sfin_ref[...] = s_scr[...]


def _ret_fwd_kernel(lg_ref, q_ref, k_ref, v_ref, g_ref, ob_ref, gn_ref, s0_ref,
                    y_ref, sfin_ref, s_scr, m_scr):
    c = pl.program_id(1)

    @pl.when(c == 0)
    def _():
        s_scr[...] = s0_ref[...]
        ri = lax.broadcasted_iota(jnp.int32, (RET_CHUNK, RET_CHUNK), 0)
        ci = lax.broadcasted_iota(jnp.int32, (RET_CHUNK, RET_CHUNK), 1)
        rel = (ri - ci).astype(F32)
        for h in range(RET_HEADS):
            mf = jnp.where(rel >= 0, jnp.exp(lg_ref[0, h] * jnp.maximum(rel, 0.0)), 0.0)
            mb = jnp.where(rel <= 0, jnp.exp(lg_ref[1, h] * jnp.maximum(-rel, 0.0)), 0.0)
            m_scr[h] = mf + mb

    pos = _chunk_pos()
    for h in range(RET_HEADS):
        lg = lg_ref[0, h]
        vcols = slice(h * RET_DV, (h + 1) * RET_DV)
        q = q_ref[:, h * RET_DK:(h + 1) * RET_DK]
        k = k_ref[:, h * RET_DK:(h + 1) * RET_DK]
        v = v_ref[:, vcols]
        st = s_scr[h]
        s = lax.dot_general(q, k, (((1,), (1,)), ((), ())), preferred_element_type=F32)
        p = (s * m_scr[h]).astype(BF16)
        qs = jnp.dot(q, st.astype(BF16), preferred_element_type=F32)
        o = (jnp.dot(p, v, preferred_element_type=F32)
             + _row_scale(qs, jnp.exp(lg * (pos + 1.0))) + ob_ref[:, vcols])
        kd = _row_scale(k.astype(F32), jnp.exp(lg * (RET_CHUNK - 1.0 - pos))).astype(BF16)
        kv = lax.dot_general(kd, v, (((0,), (0,)), ((), ())), preferred_element_type=F32)
        s_scr[h] = st * jnp.exp(lg * RET_CHUNK) + kv
        dlt = o - jnp.mean(o, axis=-1, keepdims=True)
        var = jnp.mean(dlt * dlt, axis=-1, keepdims=True)
        g = g_ref[:, vcols].astype(F32)
        y = dlt * lax.rsqrt(var + EPS) * gn_ref[:, vcols] * (g * jax.nn.sigmoid(g))
        y_ref[:, vcols] = y.astype(BF16)

    @pl.when(c == pl.num_programs(1) - 1)
    def _():
        sfin_ref[...] = s_scr[...]


def _state_spec():
    return pl.BlockSpec((None, RET_HEADS, RET_DK, RET_DV), lambda i, j: (i, 0, 0, 0))


def _ret_bwd(p, lg, s0):
    b, t, _ = p.shape
    nc = t // RET_CHUNK
    rev = lambda blk: (lambda i, j: (i, nc - 1 - j, blk))
    return pl.pallas_call(
        _ret_bwd_kernel,
        grid=(b, nc),
        in_specs=[_smem_spec(),
                  pl.BlockSpec((None, RET_CHUNK, RET_QK), rev(0)),
                  pl.BlockSpec((None, RET_CHUNK, RET_QK), rev(1)),
                  pl.BlockSpec((None, RET_CHUNK, RET_V), rev(1)),
                  _state_spec()],
        out_specs=[pl.BlockSpec((None, RET_CHUNK, RET_V), rev(0)), _state_spec()],
        out_shape=[jax.ShapeDtypeStruct((b, t, RET_V), F32),
                   jax.ShapeDtypeStruct(s0.shape, F32)],
        scratch_shapes=[pltpu.VMEM((RET_HEADS, RET_DK, RET_DV), F32)],
        compiler_params=_params(2, 48),
        name="ret_bwd",
    )(lg, p, p, p, s0)


def _ret_fwd(p, ob, lg, gn, s0):
    b, t, _ = p.shape
    nc = t // RET_CHUNK
    fwd = lambda blk: (lambda i, j: (i, j, blk))
    return pl.pallas_call(
        _ret_fwd_kernel,
        grid=(b, nc),
        in_specs=[_smem_spec(),
                  pl.BlockSpec((None, RET_CHUNK, RET_QK), fwd(0)),
                  pl.BlockSpec((None, RET_CHUNK, RET_QK), fwd(1)),
                  pl.BlockSpec((None, RET_CHUNK, RET_V), fwd(1)),
                  pl.BlockSpec((None, RET_CHUNK, RET_V), fwd(2)),
                  pl.BlockSpec((None, RET_CHUNK, RET_V), fwd(0)),
                  pl.BlockSpec((1, RET_V), lambda i, j: (0, 0)),
                  _state_spec()],
        out_specs=[pl.BlockSpec((None, RET_CHUNK, RET_V), fwd(0)), _state_spec()],
        out_shape=[jax.ShapeDtypeStruct((b, t, RET_V), BF16),
                   jax.ShapeDtypeStruct(s0.shape, F32)],
        scratch_shapes=[pltpu.VMEM((RET_HEADS, RET_DK, RET_DV), F32),
                        pltpu.VMEM((RET_HEADS, RET_CHUNK, RET_CHUNK), F32)],
        compiler_params=_params(2, 48),
        name="ret_fwd",
    )(lg, p, p, p, p, ob, gn, s0)


def _lru_coeffs(ext_scr, cw_ref, cb_ref, wg_ref, bg_ref, c8_ref, a_scr, u_scr, tt):
    for nb in range(LRU_BLOCKS):
        cols = slice(nb * LRU_BW, (nb + 1) * LRU_BW)
        xc = cb_ref[:, cols]
        for j in range(CONV_W):
            xc = xc + ext_scr[SUBLANES - CONV_W // 2 + j:SUBLANES - CONV_W // 2 + j + tt, cols] \
                * cw_ref[j:j + 1, cols]
        res = jnp.dot(xc.astype(BF16), wg_ref[nb], preferred_element_type=F32)
        r = jax.nn.sigmoid(res[:, :LRU_BW] + bg_ref[0:1, cols])
        gi = jax.nn.sigmoid(res[:, LRU_BW:] + bg_ref[1:2, cols])
        log_a = r * c8_ref[:, cols]
        th = jnp.tanh(log_a)
        one_minus_a2 = -2.0 * th / (1.0 - th)
        a_scr[:, cols] = jnp.exp(log_a)
        u_scr[:, cols] = jnp.sqrt(one_minus_a2) * gi * xc


def _lru_scan(a_scr, u_scr, h_dst, carry_scr, tt, reverse):
    n = tt // SUBLANES
    row = lax.broadcasted_iota(jnp.int32, (SUBLANES, LRU_WIDTH), 0)

    def body(i, carry):
        g = (n - 1 - i) if reverse else i
        start = pl.multiple_of(g * SUBLANES, SUBLANES)
        a = a_scr[pl.ds(start, SUBLANES), :]
        u = u_scr[pl.ds(start, SUBLANES), :]
        for d in (1, 2, 4):
            if reverse:
                ok = row < SUBLANES - d
                shift = SUBLANES - d
            else:
                ok = row >= d
                shift = d
            a_s = jnp.where(ok, pltpu.roll(a, shift, 0), 1.0)
            u_s = jnp.where(ok, pltpu.roll(u, shift, 0), 0.0)
            u = a * u_s + u
            a = a * a_s
        hh = u + a * carry
        h_dst[pl.ds(start, SUBLANES), :] = hh
        last = hh[0:1, :] if reverse else hh[SUBLANES - 1:SUBLANES, :]
        return jnp.broadcast_to(last, (SUBLANES, LRU_WIDTH))

    carry_scr[...] = lax.fori_loop(0, n, body, carry_scr[...])


def _lru_bwd_kernel(xr_ref, halo_ref, cw_ref, cb_ref, wg_ref, bg_ref, c8_ref, h0_ref,
                    h_ref, hfin_ref, ext_scr, a_scr, u_scr, carry_scr, *, tt):
    i = pl.program_id(1)

    @pl.when(i == 0)
    def _():
        carry_scr[...] = h0_ref[...]

    ext_scr[0:SUBLANES, :] = halo_ref[0:SUBLANES, :]
    ext_scr[SUBLANES:SUBLANES + tt, :] = xr_ref[...]
    ext_scr[SUBLANES + tt:2 * SUBLANES + tt, :] = halo_ref[SUBLANES:2 * SUBLANES, :]
    _lru_coeffs(ext_scr, cw_ref, cb_ref, wg_ref, bg_ref, c8_ref, a_scr, u_scr, tt)
    _lru_scan(a_scr, u_scr, h_ref, carry_scr, tt, True)

    @pl.when(i == pl.num_programs(1) - 1)
    def _():
        hfin_ref[...] = carry_scr[...]


def _lru_fwd_kernel(xr_ref, halo_ref, cw_ref, cb_ref, wg_ref, bg_ref, c8_ref, h0_ref,
                    hb_ref, g_ref, y_ref, hfin_ref, ext_scr, a_scr, u_scr, carry_scr, *, tt):
    i = pl.program_id(1)

    @pl.when(i == 0)
    def _():
        carry_scr[...] = h0_ref[...]

    ext_scr[0:SUBLANES, :] = halo_ref[0:SUBLANES, :]
    ext_scr[SUBLANES:SUBLANES + tt, :] = xr_ref[...]
    ext_scr[SUBLANES + tt:2 * SUBLANES + tt, :] = halo_ref[SUBLANES:2 * SUBLANES, :]
    _lru_coeffs(ext_scr, cw_ref, cb_ref, wg_ref, bg_ref, c8_ref, a_scr, u_scr, tt)
    _lru_scan(a_scr, u_scr, a_scr, carry_scr, tt, False)
    g = g_ref[...].astype(F32)
    y_ref[...] = ((a_scr[...] + hb_ref[...]) * (g * jax.nn.sigmoid(g))).astype(BF16)

    @pl.when(i == pl.num_programs(1) - 1)
    def _():
        hfin_ref[...] = carry_scr[...]


def _lru_pass(xr, halo, cw, cb, wg, bg, c8, h0, hb=None, g=None, *, tt):
    b, t, w = xr.shape
    nblk = t // tt
    reverse = hb is None
    tmap = (lambda i, j: (i, nblk - 1 - j, 0)) if reverse else (lambda i, j: (i, j, 0))
    hmap = (lambda i, j: (i, nblk - 1 - j, 0, 0)) if reverse else (lambda i, j: (i, j, 0, 0))
    const2 = lambda i, j: (0, 0)
    tile = pl.BlockSpec((None, tt, w), tmap)
    state = pl.BlockSpec((None, SUBLANES, w), lambda i, j: (i, 0, 0))
    in_specs = [tile,
                pl.BlockSpec((None, None, HALO_ROWS, w), hmap),
                pl.BlockSpec((CONV_W, w), const2),
                pl.BlockSpec((1, w), const2),
                pl.BlockSpec((LRU_BLOCKS, LRU_BW, 2 * LRU_BW), lambda i, j: (0, 0, 0)),
                pl.BlockSpec((2, w), const2),
                pl.BlockSpec((1, w), const2),
                state]
    args = [xr, halo, cw, cb, wg, bg, c8, h0]
    if reverse:
        kern = functools.partial(_lru_bwd_kernel, tt=tt)
        out_dtype = F32
    else:
        kern = functools.partial(_lru_fwd_kernel, tt=tt)
        in_specs += [tile, tile]
        args += [hb, g]
        out_dtype = BF16
    return pl.pallas_call(
        kern,
        grid=(b, nblk),
        in_specs=in_specs,
        out_specs=[tile, state],
        out_shape=[jax.ShapeDtypeStruct((b, t, w), out_dtype),
                   jax.ShapeDtypeStruct((b, SUBLANES, w), F32)],
        scratch_shapes=[pltpu.VMEM((tt + 2 * SUBLANES, w), F32),
                        pltpu.VMEM((tt, w), F32),
                        pltpu.VMEM((tt, w), F32),
                        pltpu.VMEM((SUBLANES, w), F32)],
        compiler_params=_params(2, 48),
        name="lru_bwd" if reverse else "lru_fwd",
    )(*args)


def _conv_halo(xr, tt):
    b, t, w = xr.shape
    nblk = t // tt
    r = xr.reshape(b, nblk, tt, w)
    prev = jnp.concatenate([jnp.zeros((b, 1, 2, w), F32), r[:, :-1, tt - 2:, :]], axis=1)
    nxt = jnp.concatenate([r[:, 1:, 0:1, :], jnp.zeros((b, 1, 1, w), F32)], axis=1)
    return jnp.concatenate([jnp.zeros((b, nblk, SUBLANES - 2, w), F32), prev, nxt,
                            jnp.zeros((b, nblk, SUBLANES - 1, w), F32)], axis=2)


def _rope_tables(n_tok):
    n_rows = n_tok // GRID_W
    row = jnp.repeat(jnp.arange(n_rows, dtype=F32), GRID_W)
    col = jnp.tile(jnp.arange(GRID_W, dtype=F32), n_rows)
    n_freq = RET_DK // 4
    inv = ROPE_BASE ** (-jnp.arange(n_freq, dtype=F32) / n_freq)
    ang = jnp.concatenate([row[:, None] * inv, col[:, None] * inv], axis=-1)
    return jnp.cos(ang), jnp.sin(ang)


def _deinterleave_heads(w):
    d = w.shape[0]
    return w.reshape(d, RET_HEADS, HALF_DK, 2).transpose(0, 1, 3, 2).reshape(d, RET_QK)


def _retention_layer(x, s_ctx, mods, cos, sin, w_in, log_decay, gn_g, w_out, need_ctx):
    (mul_l, add_l, gp_l), (mul_c, add_c, gp_c) = mods
    b = x.shape[0]
    lc = s_ctx.shape[1]
    w = jnp.concatenate([_deinterleave_heads(w_in[:, :RET_QK]),
                         _deinterleave_heads(w_in[:, RET_QK:2 * RET_QK]),
                         w_in[:, 2 * RET_QK:]], axis=1).astype(BF16)
    lg = -jnp.abs(log_decay.astype(F32))
    gn = gn_g.astype(F32).reshape(1, RET_V)
    w_o = w_out.astype(BF16)

    p_l = _ret_inproj(x, mul_l, add_l, w, cos, sin, rope=True, tm=ROW_TILE)
    p_c = _ret_inproj(s_ctx, mul_c, add_c, w, cos[:lc], sin[:lc], rope=False, tm=lc)
    zero = jnp.zeros((b, RET_HEADS, RET_DK, RET_DV), F32)
    ob_c, sb_c = _ret_bwd(p_c, lg, zero)
    ob_l, _ = _ret_bwd(p_l, lg, sb_c)
    y_c, sf_c = _ret_fwd(p_c, ob_c, lg, gn, zero)
    y_l, _ = _ret_fwd(p_l, ob_l, lg, gn, sf_c)
    x = _outproj(y_l, w_o, x, gp_l, tm=ROW_TILE)
    if need_ctx:
        s_ctx = _outproj(y_c, w_o, s_ctx, gp_c, tm=lc)
    return x, s_ctx


def _lru_layer(x, s_ctx, mods, w_in, conv_w, conv_b, w_a, b_a, w_x, b_x, lam, w_out, need_ctx):
    (mul_l, add_l, gp_l), (mul_c, add_c, gp_c) = mods
    b = x.shape[0]
    lc = s_ctx.shape[1]
    w = w_in.astype(BF16)
    w_o = w_out.astype(BF16)
    cw = conv_w.astype(F32)
    cb = conv_b.astype(F32).reshape(1, LRU_WIDTH)
    c8 = -LRU_C * jax.nn.softplus(-lam.astype(F32))

    xr_l, g_l = _lru_inproj(x, mul_l, add_l, w, tm=ROW_TILE)
    xr_c, g_c = _lru_inproj(s_ctx, mul_c, add_c, w, tm=lc)
    halo_l = _conv_halo(xr_l, LRU_TILE)
    halo_c = _conv_halo(xr_c, lc)
    zero = jnp.zeros((b, SUBLANES, LRU_WIDTH), F32)

    def gate_params(d):
        wg = jnp.concatenate([w_a[d], w_x[d]], axis=-1).astype(BF16)
        bg = jnp.stack([b_a[d], b_x[d]]).astype(F32)
        return cw, cb, wg, bg, c8[d].reshape(1, LRU_WIDTH)

    hb_c, hb_fin = _lru_pass(xr_c, halo_c, *gate_params(1), zero, tt=lc)
    hb_l, _ = _lru_pass(xr_l, halo_l, *gate_params(1), hb_fin, tt=LRU_TILE)
    y_c, hf_fin = _lru_pass(xr_c, halo_c, *gate_params(0), zero, hb_c, g_c, tt=lc)
    y_l, _ = _lru_pass(xr_l, halo_l, *gate_params(0), hf_fin, hb_l, g_l, tt=LRU_TILE)
    x = _outproj(y_l, w_o, x, gp_l, tm=ROW_TILE)
    if need_ctx:
        s_ctx = _outproj(y_c, w_o, s_ctx, gp_c, tm=lc)
    return x, s_ctx


def kernel(x, c, ctx, c_ctx, mod_w, mod_b, norm_pre, norm_post, ret_w_in, ret_log_decay, ret_gn,
           ret_w_out, lru_w_in, lru_conv_w, lru_conv_b, lru_w_a, lru_b_a, lru_w_x, lru_b_x,
           lru_lambda, lru_w_out):
    b, n_tok, d = x.shape
    depth = mod_w.shape[0]
    assert n_tok % ROW_TILE == 0 and n_tok % LRU_TILE == 0 and n_tok % RET_CHUNK == 0
    assert ctx.shape[1] % RET_CHUNK == 0 and ctx.shape[1] <= ROW_TILE
    cos, sin = _rope_tables(n_tok)

    rows = -(-(b + 1) // SUBLANES) * SUBLANES
    cc = jnp.concatenate([c, c_ctx[None, :], jnp.zeros((rows - b - 1, d), F32)], axis=0)
    mod = _modulation(cc, mod_w, mod_b)

    s_ctx = ctx
    for i in range(depth):
        shift, scale, gate = jnp.split(mod[i], 3, axis=-1)
        mods = []
        for sel in (slice(0, b), slice(b, b + 1)):
            mul = jnp.broadcast_to(norm_pre[i] * (1.0 + scale[sel]), (b, d))[:, None, :]
            add = jnp.broadcast_to(shift[sel], (b, d))[:, None, :]
            gp = jnp.broadcast_to(gate[sel] * norm_post[i], (b, d))[:, None, :]
            mods.append((mul, add, gp))
        need_ctx = i < depth - 1
        j = i // 2
        if i % 2 == 0:
            x, s_ctx = _retention_layer(x, s_ctx, mods, cos, sin, ret_w_in[j], ret_log_decay[j],
                                        ret_gn[j], ret_w_out[j], need_ctx)
        else:
            x, s_ctx = _lru_layer(x, s_ctx, mods, lru_w_in[j], lru_conv_w[j], lru_conv_b[j],
                                  lru_w_a[j], lru_b_a[j], lru_w_x[j], lru_b_x[j],
                                  lru_lambda[j], lru_w_out[j], need_ctx)
    return x
```

```python
import functools

import jax
import jax.numpy as jnp
from jax import lax
from jax.experimental import pallas as pl
from jax.experimental.pallas import tpu as pltpu

F32 = jnp.float32
BF16 = jnp.bfloat16

EPS = 1e-6
GRID_W = 64
ROPE_BASE = 10000.0

RET_HEADS = 4
RET_DK = 256
RET_DV = 512
RET_QK = RET_HEADS * RET_DK
RET_V = RET_HEADS * RET_DV
RET_IN = 2 * RET_QK + 2 * RET_V
HALF_DK = RET_DK // 2

LRU_BLOCKS = 10
LRU_BW = 128
LRU_WIDTH = LRU_BLOCKS * LRU_BW
CONV_W = 4
LRU_C = 8.0

LANES = 128
SUBLANES = 8
MIB = 1024 * 1024

RET_CHUNK = 256
ROW_TILE = 512


def _params(n_axes, vmem_mib):
    return pltpu.CompilerParams(
        dimension_semantics=("arbitrary",) * n_axes,
        vmem_limit_bytes=vmem_mib * MIB)


def _smem_spec():
    return pl.BlockSpec(memory_space=pltpu.SMEM)


def _mod_kernel(cc_ref, w_ref, b_ref, o_ref):
    cc = cc_ref[...]
    act = cc * jax.nn.sigmoid(cc)
    o_ref[...] = jnp.dot(act.astype(BF16), w_ref[...].astype(BF16),
                         preferred_element_type=F32) + b_ref[...]


def _modulation(cc, mod_w, mod_b):
    depth, d, d3 = mod_w.shape
    rows = cc.shape[0]
    return pl.pallas_call(
        _mod_kernel,
        grid=(depth, d3 // d),
        in_specs=[pl.BlockSpec((rows, d), lambda i, j: (0, 0)),
                  pl.BlockSpec((None, d, d), lambda i, j: (i, 0, j)),
                  pl.BlockSpec((None, 1, d), lambda i, j: (i, 0, j))],
        out_specs=pl.BlockSpec((None, rows, d), lambda i, j: (i, 0, j)),
        out_shape=jax.ShapeDtypeStruct((depth, rows, d3), F32),
        compiler_params=_params(2, 32),
        name="modulation",
    )(cc, mod_w, mod_b.reshape(depth, 1, d3))


def _norm_modulate(x_ref, mul_ref, add_ref, h_scr):
    x = x_ref[...]
    ms = jnp.mean(x * x, axis=-1, keepdims=True)
    h = x * lax.rsqrt(ms + EPS) * mul_ref[...] + add_ref[...]
    h_scr[...] = h.astype(BF16)


def _ret_inproj_kernel(x_ref, mul_ref, add_ref, w_ref, cos_ref, sin_ref, o_ref, h_scr, *, rope):
    _norm_modulate(x_ref, mul_ref, add_ref, h_scr)
    hb = h_scr[...]
    scale = RET_DK ** -0.5
    for j in range(2 * RET_HEADS):
        cols = slice(j * RET_DK, (j + 1) * RET_DK)
        acc = jnp.dot(hb, w_ref[:, cols], preferred_element_type=F32)
        if j >= RET_HEADS:
            acc = acc * scale
        if rope:
            e, o = acc[:, :HALF_DK], acc[:, HALF_DK:]
            c, s = cos_ref[...], sin_ref[...]
            acc = jnp.concatenate([e * c - o * s, e * s + o * c], axis=1)
        o_ref[:, cols] = acc.astype(BF16)
    for j in range(2 * RET_QK // RET_DV, RET_IN // RET_DV):
        cols = slice(j * RET_DV, (j + 1) * RET_DV)
        o_ref[:, cols] = jnp.dot(hb, w_ref[:, cols], preferred_element_type=F32).astype(BF16)


def _ret_inproj(x, mul, add, w, cos, sin, *, rope, tm):
    b, t, d = x.shape
    kern = functools.partial(_ret_inproj_kernel, rope=rope)
    return pl.pallas_call(
        kern,
        grid=(b, t // tm),
        in_specs=[pl.BlockSpec((None, tm, d), lambda i, j: (i, j, 0)),
                  pl.BlockSpec((None, 1, d), lambda i, j: (i, 0, 0)),
                  pl.BlockSpec((None, 1, d), lambda i, j: (i, 0, 0)),
                  pl.BlockSpec((d, RET_IN), lambda i, j: (0, 0)),
                  pl.BlockSpec((tm, HALF_DK), lambda i, j: (j, 0)),
                  pl.BlockSpec((tm, HALF_DK), lambda i, j: (j, 0))],
        out_specs=pl.BlockSpec((None, tm, RET_IN), lambda i, j: (i, j, 0)),
        out_shape=jax.ShapeDtypeStruct((b, t, RET_IN), BF16),
        scratch_shapes=[pltpu.VMEM((tm, d), BF16)],
        compiler_params=_params(2, 56),
        name="ret_inproj_rope" if rope else "ret_inproj",
    )(x, mul, add, w, cos, sin)


def _store_interleaved(dst_ref, blk, val):
    seg = val.shape[0] // SUBLANES
    for s in range(SUBLANES):
        dst_ref[blk, pl.ds(s, seg, stride=SUBLANES), :] = val[s * seg:(s + 1) * seg, :]


def _lru_inproj_kernel(x_ref, mul_ref, add_ref, w_ref, xr_ref, g_ref, h_scr):
    _norm_modulate(x_ref, mul_ref, add_ref, h_scr)
    hb = h_scr[...]
    for j in range(LRU_WIDTH // (2 * LANES)):
        for dst_ref, base in ((xr_ref, 0), (g_ref, LRU_WIDTH)):
            cols = slice(base + j * 2 * LANES, base + (j + 1) * 2 * LANES)
            acc = jnp.dot(hb, w_ref[:, cols], preferred_element_type=F32)
            _store_interleaved(dst_ref, 2 * j, acc[:, :LANES])
            _store_interleaved(dst_ref, 2 * j + 1, acc[:, LANES:])


def _lru_inproj(x, mul, add, w, *, tm):
    b, t, d = x.shape
    slab = pl.BlockSpec((None, LRU_BLOCKS, tm, LRU_BW), lambda i, j: (i, 0, j, 0))
    return pl.pallas_call(
        _lru_inproj_kernel,
        grid=(b, t // tm),
        in_specs=[pl.BlockSpec((None, tm, d), lambda i, j: (i, j, 0)),
                  pl.BlockSpec((None, 1, d), lambda i, j: (i, 0, 0)),
                  pl.BlockSpec((None, 1, d), lambda i, j: (i, 0, 0)),
                  pl.BlockSpec((d, 2 * LRU_WIDTH), lambda i, j: (0, 0))],
        out_specs=[slab, slab],
        out_shape=[jax.ShapeDtypeStruct((b, LRU_BLOCKS, t, LRU_BW), F32),
                   jax.ShapeDtypeStruct((b, LRU_BLOCKS, t, LRU_BW), F32)],
        scratch_shapes=[pltpu.VMEM((tm, d), BF16)],
        compiler_params=_params(2, 48),
        name="lru_inproj",
    )(x, mul, add, w)


def _outproj_kernel(y_ref, w_ref, x_ref, gp_ref, o_ref):
    z = jnp.dot(y_ref[...], w_ref[...], preferred_element_type=F32)
    ms = jnp.mean(z * z, axis=-1, keepdims=True)
    o_ref[...] = x_ref[...] + z * lax.rsqrt(ms + EPS) * gp_ref[...]


def _outproj(y, w, x, gp, *, tm):
    b, t, d = x.shape
    k = y.shape[-1]
    return pl.pallas_call(
        _outproj_kernel,
        grid=(b, t // tm),
        in_specs=[pl.BlockSpec((None, tm, k), lambda i, j: (i, j, 0)),
                  pl.BlockSpec((k, d), lambda i, j: (0, 0)),
                  pl.BlockSpec((None, tm, d), lambda i, j: (i, j, 0)),
                  pl.BlockSpec((None, 1, d), lambda i, j: (i, 0, 0))],
        out_specs=pl.BlockSpec((None, tm, d), lambda i, j: (i, j, 0)),
        out_shape=jax.ShapeDtypeStruct((b, t, d), F32),
        compiler_params=_params(2, 48),
        name="outproj",
    )(y, w, x, gp)


def _outproj_interleaved_kernel(y_ref, w_ref, x_ref, gp_ref, o_ref, z_scr):
    z = jnp.dot(y_ref[...], w_ref[...], preferred_element_type=F32)
    ms = jnp.mean(z * z, axis=-1, keepdims=True)
    zn = z * lax.rsqrt(ms + EPS) * gp_ref[...]
    n_slab = zn.shape[1] // LANES
    seg = zn.shape[0] // SUBLANES
    for d in range(n_slab):
        z_scr[d] = zn[:, d * LANES:(d + 1) * LANES]
    for d in range(n_slab):
        cols = slice(d * LANES, (d + 1) * LANES)
        for s in range(SUBLANES):
            rows = slice(s * seg, (s + 1) * seg)
            o_ref[rows, cols] = x_ref[rows, cols] + z_scr[d, pl.ds(s, seg, stride=SUBLANES), :]


def _outproj_interleaved(y, w, x, gp, *, tm):
    b, t, d = x.shape
    k = y.shape[-1]
    return pl.pallas_call(
        _outproj_interleaved_kernel,
        grid=(b, t // tm),
        in_specs=[pl.BlockSpec((None, tm, k), lambda i, j: (i, j, 0)),
                  pl.BlockSpec((k, d), lambda i, j: (0, 0)),
                  pl.BlockSpec((None, tm, d), lambda i, j: (i, j, 0)),
                  pl.BlockSpec((None, 1, d), lambda i, j: (i, 0, 0))],
        out_specs=pl.BlockSpec((None, tm, d), lambda i, j: (i, j, 0)),
        out_shape=jax.ShapeDtypeStruct((b, t, d), F32),
        scratch_shapes=[pltpu.VMEM((d // LANES, tm, LANES), F32)],
        compiler_params=_params(2, 48),
        name="outproj_interleaved",
    )(y, w, x, gp)


def _row_scale(x, dec):
    n = x.shape[1] // LANES
    return jnp.concatenate([x[:, i * LANES:(i + 1) * LANES] * dec for i in range(n)], axis=1)


def _chunk_pos():
    return lax.broadcasted_iota(jnp.int32, (RET_CHUNK, LANES), 0).astype(F32)


def _ret_bwd_kernel(lg_ref, q_ref, k_ref, v_ref, s0_ref, ob_ref, sfin_ref, s_scr):
    c = pl.program_id(1)

    @pl.when(c == 0)
    def _():
        s_scr[...] = s0_ref[...]

    pos = _chunk_pos()
    for h in range(RET_HEADS):
        lg = lg_ref[1, h]
        q = q_ref[:, h * RET_DK:(h + 1) * RET_DK]
        k = k_ref[:, h * RET_DK:(h + 1) * RET_DK]
        v = v_ref[:, h * RET_DV:(h + 1) * RET_DV]
        st = s_scr[h]
        qs = jnp.dot(q, st.astype(BF16), preferred_element_type=F32)
        ob_ref[:, h * RET_DV:(h + 1) * RET_DV] = _row_scale(qs, jnp.exp(lg * (RET_CHUNK - pos)))
        kd = _row_scale(k.astype(F32), jnp.exp(lg * pos)).astype(BF16)
        kv = lax.dot_general(kd, v, (((0,), (0,)), ((), ())), preferred_element_type=F32)
        s_scr[h] = st * jnp.exp(lg * RET_CHUNK) + kv

    @pl.when(c == pl.num_programs(1) - 1)
    def _():
        sfin_ref[...] = s_scr[...]


def _ret_fwd_kernel(lg_ref, q_ref, k_ref, v_ref, g_ref, ob_ref, gn_ref, s0_ref,
                    y_ref, sfin_ref, s_scr, m_scr):
    c = pl.program_id(1)

    @pl.when(c == 0)
    def _():
        s_scr[...] = s0_ref[...]
        ri = lax.broadcasted_iota(jnp.int32, (RET_CHUNK, RET_CHUNK), 0)
        ci = lax.broadcasted_iota(jnp.int32, (RET_CHUNK, RET_CHUNK), 1)
        rel = (ri - ci).astype(F32)
        for h in range(RET_HEADS):
            mf = jnp.where(rel >= 0, jnp.exp(lg_ref[0, h] * jnp.maximum(rel, 0.0)), 0.0)
            mb = jnp.where(rel <= 0, jnp.exp(lg_ref[1, h] * jnp.maximum(-rel, 0.0)), 0.0)
            m_scr[h] = mf + mb

    pos = _chunk_pos()
    for h in range(RET_HEADS):
        lg = lg_ref[0, h]
        vcols = slice(h * RET_DV, (h + 1) * RET_DV)
        q = q_ref[:, h * RET_DK:(h + 1) * RET_DK]
        k = k_ref[:, h * RET_DK:(h + 1) * RET_DK]
        v = v_ref[:, vcols]
        st = s_scr[h]
        s = lax.dot_general(q, k, (((1,), (1,)), ((), ())), preferred_element_type=F32)
        p = (s * m_scr[h]).astype(BF16)
        qs = jnp.dot(q, st.astype(BF16), preferred_element_type=F32)
        o = (jnp.dot(p, v, preferred_element_type=F32)
             + _row_scale(qs, jnp.exp(lg * (pos + 1.0))) + ob_ref[:, vcols])
        kd = _row_scale(k.astype(F32), jnp.exp(lg * (RET_CHUNK - 1.0 - pos))).astype(BF16)
        kv = lax.dot_general(kd, v, (((0,), (0,)), ((), ())), preferred_element_type=F32)
        s_scr[h] = st * jnp.exp(lg * RET_CHUNK) + kv
        dlt = o - jnp.mean(o, axis=-1, keepdims=True)
        var = jnp.mean(dlt * dlt, axis=-1, keepdims=True)
        g = g_ref[:, vcols].astype(F32)
        y = dlt * lax.rsqrt(var + EPS) * gn_ref[:, vcols] * (g * jax.nn.sigmoid(g))
        y_ref[:, vcols] = y.astype(BF16)

    @pl.when(c == pl.num_programs(1) - 1)
    def _():
        sfin_ref[...] = s_scr[...]


def _state_spec():
    return pl.BlockSpec((None, RET_HEADS, RET_DK, RET_DV), lambda i, j: (i, 0, 0, 0))


def _ret_bwd(p, lg, s0):
    b, t, _ = p.shape
    nc = t // RET_CHUNK
    rev = lambda blk: (lambda i, j: (i, nc - 1 - j, blk))
    return pl.pallas_call(
        _ret_bwd_kernel,
        grid=(b, nc),
        in_specs=[_smem_spec(),
                  pl.BlockSpec((None, RET_CHUNK, RET_QK), rev(0)),
                  pl.BlockSpec((None, RET_CHUNK, RET_QK), rev(1)),
                  pl.BlockSpec((None, RET_CHUNK, RET_V), rev(1)),
                  _state_spec()],
        out_specs=[pl.BlockSpec((None, RET_CHUNK, RET_V), rev(0)), _state_spec()],
        out_shape=[jax.ShapeDtypeStruct((b, t, RET_V), F32),
                   jax.ShapeDtypeStruct(s0.shape, F32)],
        scratch_shapes=[pltpu.VMEM((RET_HEADS, RET_DK, RET_DV), F32)],
        compiler_params=_params(2, 48),
        name="ret_bwd",
    )(lg, p, p, p, s0)


def _ret_fwd(p, ob, lg, gn, s0):
    b, t, _ = p.shape
    nc = t // RET_CHUNK
    fwd = lambda blk: (lambda i, j: (i, j, blk))
    return pl.pallas_call(
        _ret_fwd_kernel,
        grid=(b, nc),
        in_specs=[_smem_spec(),
                  pl.BlockSpec((None, RET_CHUNK, RET_QK), fwd(0)),
                  pl.BlockSpec((None, RET_CHUNK, RET_QK), fwd(1)),
                  pl.BlockSpec((None, RET_CHUNK, RET_V), fwd(1)),
                  pl.BlockSpec((None, RET_CHUNK, RET_V), fwd(2)),
                  pl.BlockSpec((None, RET_CHUNK, RET_V), fwd(0)),
                  pl.BlockSpec((1, RET_V), lambda i, j: (0, 0)),
                  _state_spec()],
        out_specs=[pl.BlockSpec((None, RET_CHUNK, RET_V), fwd(0)), _state_spec()],
        out_shape=[jax.ShapeDtypeStruct((b, t, RET_V), BF16),
                   jax.ShapeDtypeStruct(s0.shape, F32)],
        scratch_shapes=[pltpu.VMEM((RET_HEADS, RET_DK, RET_DV), F32),
                        pltpu.VMEM((RET_HEADS, RET_CHUNK, RET_CHUNK), F32)],
        compiler_params=_params(2, 48),
        name="ret_fwd",
    )(lg, p, p, p, p, ob, gn, s0)


def _lru_conv(xr_ref, halo_ref, cw_ref, cb_ref, ext_scr, xc_ref, tt):
    sub = lax.broadcasted_iota(jnp.int32, (SUBLANES, LRU_BW), 0)
    left = CONV_W // 2
    for nb in range(LRU_BLOCKS):
        ext_scr[nb, left * SUBLANES:left * SUBLANES + tt, :] = xr_ref[nb]
        for k in range(left):
            src = xr_ref[nb, tt - (left - k) * SUBLANES:tt - (left - k - 1) * SUBLANES, :]
            ext_scr[nb, k * SUBLANES:(k + 1) * SUBLANES, :] = jnp.where(
                sub == 0, halo_ref[nb, k:k + 1, :], pltpu.roll(src, 1, 0))
        for k in range(CONV_W - 1 - left):
            src = xr_ref[nb, k * SUBLANES:(k + 1) * SUBLANES, :]
            ext_scr[nb, (left + k) * SUBLANES + tt:(left + k + 1) * SUBLANES + tt, :] = jnp.where(
                sub == SUBLANES - 1, halo_ref[nb, left + k:left + k + 1, :],
                pltpu.roll(src, SUBLANES - 1, 0))
        xc = cb_ref[nb:nb + 1, :]
        for j in range(CONV_W):
            xc = xc + ext_scr[nb, j * SUBLANES:j * SUBLANES + tt, :] * cw_ref[j, nb:nb + 1, :]
        xc_ref[nb] = xc


def _lru_coeffs(xc_ref, wg_ref, bg_ref, kh_ref, a_scr, u_scr):
    for nb in range(LRU_BLOCKS):
        xc = xc_ref[nb]
        res = jnp.dot(xc.astype(BF16), wg_ref[nb], preferred_element_type=F32)
        t_r = jnp.tanh(res[:, :LRU_BW] + bg_ref[0, nb:nb + 1, :])
        t_g = jnp.tanh(res[:, LRU_BW:] + bg_ref[1, nb:nb + 1, :])
        kh = kh_ref[nb:nb + 1, :]
        p = jnp.tanh(t_r * kh + kh)
        w = 1.0 / (1.0 + p)
        a_scr[nb] = (1.0 - p) * w
        sqrt_p = jnp.where(p > 0.0, p * lax.rsqrt(p), 0.0)
        u_scr[nb] = sqrt_p * w * (t_g + 1.0) * xc


def _lru_scan(a_scr, u_scr, dst_ref, carry_scr, tt, reverse):
    n = tt // SUBLANES
    shape = (LRU_BLOCKS, SUBLANES, LRU_BW)
    sub = lax.broadcasted_iota(jnp.int32, shape, 1)
    steps = [slice(g * SUBLANES, (g + 1) * SUBLANES) for g in range(n)]
    if reverse:
        steps = steps[::-1]

    a_seg = a_scr[:, steps[0], :]
    h_seg = u_scr[:, steps[0], :]
    for rows in steps[1:]:
        a = a_scr[:, rows, :]
        h_seg = a * h_seg + u_scr[:, rows, :]
        a_seg = a_seg * a
    for d in (1, 2, 4):
        ok = (sub < SUBLANES - d) if reverse else (sub >= d)
        shift = SUBLANES - d if reverse else d
        a_s = jnp.where(ok, pltpu.roll(a_seg, shift, 1), 1.0)
        h_s = jnp.where(ok, pltpu.roll(h_seg, shift, 1), 0.0)
        h_seg = a_seg * h_s + h_seg
        a_seg = a_seg * a_s
    carry = carry_scr[...]
    h_end = a_seg * carry + h_seg
    if reverse:
        h_in = jnp.where(sub == SUBLANES - 1, carry, pltpu.roll(h_end, SUBLANES - 1, 1))
        last = h_end[:, 0:1, :]
    else:
        h_in = jnp.where(sub == 0, carry, pltpu.roll(h_end, 1, 1))
        last = h_end[:, SUBLANES - 1:SUBLANES, :]
    carry_scr[...] = jnp.broadcast_to(last, shape)

    h = h_in
    for rows in steps:
        h = a_scr[:, rows, :] * h + u_scr[:, rows, :]
        dst_ref[:, rows, :] = h


def _lru_bwd_kernel(xr_ref, halo_ref, cw_ref, cb_ref, wg_ref, bg_ref, kh_ref, h0_ref,
                    hb_ref, xc_ref, hfin_ref, ext_scr, a_scr, u_scr, carry_scr, *, tt):
    i = pl.program_id(1)

    @pl.when(i == 0)
    def _():
        carry_scr[...] = h0_ref[...]

    _lru_conv(xr_ref, halo_ref, cw_ref, cb_ref, ext_scr, xc_ref, tt)
    _lru_coeffs(xc_ref, wg_ref, bg_ref, kh_ref, a_scr, u_scr)
    _lru_scan(a_scr, u_scr, hb_ref, carry_scr, tt, True)

    @pl.when(i == pl.num_programs(1) - 1)
    def _():
        hfin_ref[...] = carry_scr[...]


def _lru_fwd_kernel(xc_ref, wg_ref, bg_ref, kh_ref, h0_ref, hb_ref, g_ref,
                    y_ref, hfin_ref, a_scr, u_scr, carry_scr, *, tt):
    i = pl.program_id(1)

    @pl.when(i == 0)
    def _():
        carry_scr[...] = h0_ref[...]

    _lru_coeffs(xc_ref, wg_ref, bg_ref, kh_ref, a_scr, u_scr)
    _lru_scan(a_scr, u_scr, u_scr, carry_scr, tt, False)
    for nb in range(LRU_BLOCKS):
        gh = g_ref[nb]
        y = (u_scr[nb] + hb_ref[nb]) * (gh * (jnp.tanh(gh) + 1.0))
        y_ref[:, nb * LRU_BW:(nb + 1) * LRU_BW] = y.astype(BF16)

    @pl.when(i == pl.num_programs(1) - 1)
    def _():
        hfin_ref[...] = carry_scr[...]


def _lru_specs(b, t, tt, reverse):
    nblk = t // tt
    tidx = (lambda j: nblk - 1 - j) if reverse else (lambda j: j)
    slab = pl.BlockSpec((None, LRU_BLOCKS, tt, LRU_BW), lambda i, j: (i, 0, tidx(j), 0))
    state = pl.BlockSpec((None, LRU_BLOCKS, SUBLANES, LRU_BW), lambda i, j: (i, 0, 0, 0))
    halo = pl.BlockSpec((None, None, LRU_BLOCKS, SUBLANES, LRU_BW),
                        lambda i, j: (i, tidx(j), 0, 0, 0))
    gate = [pl.BlockSpec((LRU_BLOCKS, LRU_BW, 2 * LRU_BW), lambda i, j: (0, 0, 0)),
            pl.BlockSpec((2, LRU_BLOCKS, LRU_BW), lambda i, j: (0, 0, 0)),
            pl.BlockSpec((LRU_BLOCKS, LRU_BW), lambda i, j: (0, 0))]
    return nblk, slab, state, halo, gate


def _lru_bwd(xr, halo, cw, cb, wg, bg, kh, h0, *, tt):
    b, _, t, _ = xr.shape
    nblk, slab, state, halo_spec, gate = _lru_specs(b, t, tt, True)
    slab_shape = jax.ShapeDtypeStruct(xr.shape, F32)
    tile = (LRU_BLOCKS, tt, LRU_BW)
    return pl.pallas_call(
        functools.partial(_lru_bwd_kernel, tt=tt),
        grid=(b, nblk),
        in_specs=[slab, halo_spec,
                  pl.BlockSpec((CONV_W, LRU_BLOCKS, LRU_BW), lambda i, j: (0, 0, 0)),
                  pl.BlockSpec((LRU_BLOCKS, LRU_BW), lambda i, j: (0, 0))] + gate + [state],
        out_specs=[slab, slab, state],
        out_shape=[slab_shape, slab_shape, jax.ShapeDtypeStruct(h0.shape, F32)],
        scratch_shapes=[pltpu.VMEM((LRU_BLOCKS, tt + (CONV_W - 1) * SUBLANES, LRU_BW), F32),
                        pltpu.VMEM(tile, F32), pltpu.VMEM(tile, F32),
                        pltpu.VMEM((LRU_BLOCKS, SUBLANES, LRU_BW), F32)],
        compiler_params=_params(2, 48),
        name="lru_bwd",
    )(xr, halo, cw, cb, wg, bg, kh, h0)


def _lru_fwd(xc, wg, bg, kh, h0, hb, g, *, tt):
    b, _, t, _ = xc.shape
    nblk, slab, state, _, gate = _lru_specs(b, t, tt, False)
    tile = (LRU_BLOCKS, tt, LRU_BW)
    return pl.pallas_call(
        functools.partial(_lru_fwd_kernel, tt=tt),
        grid=(b, nblk),
        in_specs=[slab] + gate + [state, slab, slab],
        out_specs=[pl.BlockSpec((None, tt, LRU_WIDTH), lambda i, j: (i, j, 0)), state],
        out_shape=[jax.ShapeDtypeStruct((b, t, LRU_WIDTH), BF16),
                   jax.ShapeDtypeStruct(h0.shape, F32)],
        scratch_shapes=[pltpu.VMEM(tile, F32), pltpu.VMEM(tile, F32),
                        pltpu.VMEM((LRU_BLOCKS, SUBLANES, LRU_BW), F32)],
        compiler_params=_params(2, 48),
        name="lru_fwd",
    )(xc, wg, bg, kh, h0, hb, g)


def _conv_halo(xr, tt):
    b, nbk, t, w = xr.shape
    nblk = t // tt
    r = xr.reshape(b, nbk, nblk, tt, w)
    z = jnp.zeros((b, nbk, 1, w), F32)
    last = tt - 1
    prev2 = jnp.concatenate([z, r[:, :, :-1, last - SUBLANES, :]], axis=2)
    prev1 = jnp.concatenate([z, r[:, :, :-1, last, :]], axis=2)
    nxt = jnp.concatenate([r[:, :, 1:, 0, :], z], axis=2)
    pad = jnp.zeros_like(prev1)
    halo = jnp.stack([prev2, prev1, nxt] + [pad] * (SUBLANES - 3), axis=3)
    return halo.transpose(0, 2, 1, 3, 4)


def _rope_tables(n_tok):
    n_rows = n_tok // GRID_W
    row = jnp.repeat(jnp.arange(n_rows, dtype=F32), GRID_W)
    col = jnp.tile(jnp.arange(GRID_W, dtype=F32), n_rows)
    n_freq = RET_DK // 4
    inv = ROPE_BASE ** (-jnp.arange(n_freq, dtype=F32) / n_freq)
    ang = jnp.concatenate([row[:, None] * inv, col[:, None] * inv], axis=-1)
    return jnp.cos(ang), jnp.sin(ang)


def _deinterleave_heads(w):
    d = w.shape[0]
    return w.reshape(d, RET_HEADS, HALF_DK, 2).transpose(0, 1, 3, 2).reshape(d, RET_QK)


def _retention_layer(x, s_ctx, mods, cos, sin, w_in, log_decay, gn_g, w_out, need_ctx):
    (mul_l, add_l, gp_l), (mul_c, add_c, gp_c) = mods
    b = x.shape[0]
    lc = s_ctx.shape[1]
    w = jnp.concatenate([_deinterleave_heads(w_in[:, :RET_QK]),
                         _deinterleave_heads(w_in[:, RET_QK:2 * RET_QK]),
                         w_in[:, 2 * RET_QK:]], axis=1).astype(BF16)
    lg = -jnp.abs(log_decay.astype(F32))
    gn = gn_g.astype(F32).reshape(1, RET_V)
    w_o = w_out.astype(BF16)

    p_l = _ret_inproj(x, mul_l, add_l, w, cos, sin, rope=True, tm=ROW_TILE)
    p_c = _ret_inproj(s_ctx, mul_c, add_c, w, cos[:lc], sin[:lc], rope=False, tm=lc)
    zero = jnp.zeros((b, RET_HEADS, RET_DK, RET_DV), F32)
    ob_c, sb_c = _ret_bwd(p_c, lg, zero)
    ob_l, _ = _ret_bwd(p_l, lg, sb_c)
    y_c, sf_c = _ret_fwd(p_c, ob_c, lg, gn, zero)
    y_l, _ = _ret_fwd(p_l, ob_l, lg, gn, sf_c)
    x = _outproj(y_l, w_o, x, gp_l, tm=ROW_TILE)
    if need_ctx:
        s_ctx = _outproj(y_c, w_o, s_ctx, gp_c, tm=lc)
    return x, s_ctx


def _lru_layer(x, s_ctx, mods, w_in, conv_w, conv_b, w_a, b_a, w_x, b_x, lam, w_out, need_ctx):
    (mul_l, add_l, gp_l), (mul_c, add_c, gp_c) = mods
    b = x.shape[0]
    lc = s_ctx.shape[1]
    w = jnp.concatenate([w_in[:, :LRU_WIDTH], 0.5 * w_in[:, LRU_WIDTH:]], axis=1).astype(BF16)
    w_o = w_out.astype(BF16)
    cw = conv_w.astype(F32).reshape(CONV_W, LRU_BLOCKS, LRU_BW)
    cb = conv_b.astype(F32).reshape(LRU_BLOCKS, LRU_BW)
    kh = (0.25 * LRU_C) * jax.nn.softplus(-lam.astype(F32))

    def gate_params(d):
        wg = (0.5 * jnp.concatenate([w_a[d], w_x[d]], axis=-1)).astype(BF16)
        bg = (0.5 * jnp.stack([b_a[d], b_x[d]])).astype(F32).reshape(2, LRU_BLOCKS, LRU_BW)
        return wg, bg, kh[d].reshape(LRU_BLOCKS, LRU_BW)

    xr_l, g_l = _lru_inproj(x, mul_l, add_l, w, tm=ROW_TILE)
    xr_c, g_c = _lru_inproj(s_ctx, mul_c, add_c, w, tm=lc)
    zero = jnp.zeros((b, LRU_BLOCKS, SUBLANES, LRU_BW), F32)
    hb_c, xc_c, hb_fin = _lru_bwd(xr_c, _conv_halo(xr_c, lc), cw, cb, *gate_params(1), zero, tt=lc)
    hb_l, xc_l, _ = _lru_bwd(xr_l, _conv_halo(xr_l, ROW_TILE), cw, cb, *gate_params(1), hb_fin,
                             tt=ROW_TILE)
    y_c, hf_fin = _lru_fwd(xc_c, *gate_params(0), zero, hb_c, g_c, tt=lc)
    y_l, _ = _lru_fwd(xc_l, *gate_params(0), hf_fin, hb_l, g_l, tt=ROW_TILE)
    x = _outproj_interleaved(y_l, w_o, x, gp_l, tm=ROW_TILE)
    if need_ctx:
        s_ctx = _outproj_interleaved(y_c, w_o, s_ctx, gp_c, tm=lc)
    return x, s_ctx


def kernel(x, c, ctx, c_ctx, mod_w, mod_b, norm_pre, norm_post, ret_w_in, ret_log_decay, ret_gn,
           ret_w_out, lru_w_in, lru_conv_w, lru_conv_b, lru_w_a, lru_b_a, lru_w_x, lru_b_x,
           lru_lambda, lru_w_out):
    b, n_tok, d = x.shape
    depth = mod_w.shape[0]
    assert n_tok % ROW_TILE == 0 and n_tok % RET_CHUNK == 0
    assert ctx.shape[1] % RET_CHUNK == 0 and ctx.shape[1] <= ROW_TILE
    cos, sin = _rope_tables(n_tok)

    rows = -(-(b + 1) // SUBLANES) * SUBLANES
    cc = jnp.concatenate([c, c_ctx[None, :], jnp.zeros((rows - b - 1, d), F32)], axis=0)
    mod = _modulation(cc, mod_w, mod_b)

    s_ctx = ctx
    for i in range(depth):
        shift, scale, gate = jnp.split(mod[i], 3, axis=-1)
        mods = []
        for sel in (slice(0, b), slice(b, b + 1)):
            mul = jnp.broadcast_to(norm_pre[i] * (1.0 + scale[sel]), (b, d))[:, None, :]
            add = jnp.broadcast_to(shift[sel], (b, d))[:, None, :]
            gp = jnp.broadcast_to(gate[sel] * norm_post[i], (b, d))[:, None, :]
            mods.append((mul, add, gp))
        need_ctx = i < depth - 1
        j = i // 2
        if i % 2 == 0:
            x, s_ctx = _retention_layer(x, s_ctx, mods, cos, sin, ret_w_in[j], ret_log_decay[j],
                                        ret_gn[j], ret_w_out[j], need_ctx)
        else:
            x, s_ctx = _lru_layer(x, s_ctx, mods, lru_w_in[j], lru_conv_w[j], lru_conv_b[j],
                                  lru_w_a[j], lru_b_a[j], lru_w_x[j], lru_b_x[j],
                                  lru_lambda[j], lru_w_out[j], need_ctx)
    return x
```

```python
import functools

import jax
import jax.numpy as jnp
from jax import lax
from jax.experimental import pallas as pl
from jax.experimental.pallas import tpu as pltpu

F32 = jnp.float32
BF16 = jnp.bfloat16

EPS = 1e-6
GRID_W = 64
ROPE_BASE = 10000.0

RET_HEADS = 4
RET_DK = 256
RET_DV = 512
RET_QK = RET_HEADS * RET_DK
RET_V = RET_HEADS * RET_DV
RET_IN = 2 * RET_QK + 2 * RET_V
HALF_DK = RET_DK // 2

LRU_BLOCKS = 10
LRU_BW = 128
LRU_WIDTH = LRU_BLOCKS * LRU_BW
CONV_W = 4
LRU_C = 8.0

LANES = 128
SUBLANES = 8
MIB = 1024 * 1024

RET_CHUNK = 256
RET_STEP_CHUNKS = 2
ROW_TILE = 512


def _params(n_axes, vmem_mib):
    return pltpu.CompilerParams(
        dimension_semantics=("arbitrary",) * n_axes,
        vmem_limit_bytes=vmem_mib * MIB)


def _smem_spec():
    return pl.BlockSpec(memory_space=pltpu.SMEM)


def _mod_kernel(cc_ref, w_ref, b_ref, o_ref):
    cc = cc_ref[...]
    act = cc * jax.nn.sigmoid(cc)
    o_ref[...] = jnp.dot(act.astype(BF16), w_ref[...].astype(BF16),
                         preferred_element_type=F32) + b_ref[...]


def _modulation(cc, mod_w, mod_b):
    depth, d, d3 = mod_w.shape
    rows = cc.shape[0]
    return pl.pallas_call(
        _mod_kernel,
        grid=(depth, d3 // d),
        in_specs=[pl.BlockSpec((rows, d), lambda i, j: (0, 0)),
                  pl.BlockSpec((None, d, d), lambda i, j: (i, 0, j)),
                  pl.BlockSpec((None, 1, d), lambda i, j: (i, 0, j))],
        out_specs=pl.BlockSpec((None, rows, d), lambda i, j: (i, 0, j)),
        out_shape=jax.ShapeDtypeStruct((depth, rows, d3), F32),
        compiler_params=_params(2, 32),
        name="modulation",
    )(cc, mod_w, mod_b.reshape(depth, 1, d3))


def _norm_modulate(x_ref, mul_ref, add_ref, h_scr):
    x = x_ref[...]
    ms = jnp.mean(x * x, axis=-1, keepdims=True)
    h = x * lax.rsqrt(ms + EPS) * mul_ref[...] + add_ref[...]
    h_scr[...] = h.astype(BF16)


def _ret_inproj_kernel(x_ref, mul_ref, add_ref, w_ref, cos_ref, sin_ref, o_ref, h_scr, *, rope):
    _norm_modulate(x_ref, mul_ref, add_ref, h_scr)
    hb = h_scr[...]
    scale = RET_DK ** -0.5
    for j in range(2 * RET_HEADS):
        cols = slice(j * RET_DK, (j + 1) * RET_DK)
        acc = jnp.dot(hb, w_ref[:, cols], preferred_element_type=F32)
        if j >= RET_HEADS:
            acc = acc * scale
        if rope:
            e, o = acc[:, :HALF_DK], acc[:, HALF_DK:]
            c, s = cos_ref[...], sin_ref[...]
            acc = jnp.concatenate([e * c - o * s, e * s + o * c], axis=1)
        o_ref[:, cols] = acc.astype(BF16)
    for j in range(2 * RET_QK // RET_DV, RET_IN // RET_DV):
        cols = slice(j * RET_DV, (j + 1) * RET_DV)
        o_ref[:, cols] = jnp.dot(hb, w_ref[:, cols], preferred_element_type=F32).astype(BF16)


def _ret_inproj(x, mul, add, w, cos, sin, *, rope, tm):
    b, t, d = x.shape
    kern = functools.partial(_ret_inproj_kernel, rope=rope)
    return pl.pallas_call(
        kern,
        grid=(b, t // tm),
        in_specs=[pl.BlockSpec((None, tm, d), lambda i, j: (i, j, 0)),
                  pl.BlockSpec((None, 1, d), lambda i, j: (i, 0, 0)),
                  pl.BlockSpec((None, 1, d), lambda i, j: (i, 0, 0)),
                  pl.BlockSpec((d, RET_IN), lambda i, j: (0, 0)),
                  pl.BlockSpec((tm, HALF_DK), lambda i, j: (j, 0)),
                  pl.BlockSpec((tm, HALF_DK), lambda i, j: (j, 0))],
        out_specs=pl.BlockSpec((None, tm, RET_IN), lambda i, j: (i, j, 0)),
        out_shape=jax.ShapeDtypeStruct((b, t, RET_IN), BF16),
        scratch_shapes=[pltpu.VMEM((tm, d), BF16)],
        compiler_params=_params(2, 56),
        name="ret_inproj_rope" if rope else "ret_inproj",
    )(x, mul, add, w, cos, sin)


def _store_interleaved(dst_ref, blk, val):
    seg = val.shape[0] // SUBLANES
    for s in range(SUBLANES):
        dst_ref[blk, pl.ds(s, seg, stride=SUBLANES), :] = val[s * seg:(s + 1) * seg, :]


def _lru_inproj_kernel(x_ref, mul_ref, add_ref, w_ref, xr_ref, g_ref, h_scr):
    _norm_modulate(x_ref, mul_ref, add_ref, h_scr)
    hb = h_scr[...]
    for j in range(LRU_WIDTH // (2 * LANES)):
        for dst_ref, base in ((xr_ref, 0), (g_ref, LRU_WIDTH)):
            cols = slice(base + j * 2 * LANES, base + (j + 1) * 2 * LANES)
            acc = jnp.dot(hb, w_ref[:, cols], preferred_element_type=F32)
            _store_interleaved(dst_ref, 2 * j, acc[:, :LANES])
            _store_interleaved(dst_ref, 2 * j + 1, acc[:, LANES:])


def _lru_inproj(x, mul, add, w, *, tm):
    b, t, d = x.shape
    slab = pl.BlockSpec((None, LRU_BLOCKS, tm, LRU_BW), lambda i, j: (i, 0, j, 0))
    return pl.pallas_call(
        _lru_inproj_kernel,
        grid=(b, t // tm),
        in_specs=[pl.BlockSpec((None, tm, d), lambda i, j: (i, j, 0)),
                  pl.BlockSpec((None, 1, d), lambda i, j: (i, 0, 0)),
                  pl.BlockSpec((None, 1, d), lambda i, j: (i, 0, 0)),
                  pl.BlockSpec((d, 2 * LRU_WIDTH), lambda i, j: (0, 0))],
        out_specs=[slab, slab],
        out_shape=[jax.ShapeDtypeStruct((b, LRU_BLOCKS, t, LRU_BW), F32),
                   jax.ShapeDtypeStruct((b, LRU_BLOCKS, t, LRU_BW), F32)],
        scratch_shapes=[pltpu.VMEM((tm, d), BF16)],
        compiler_params=_params(2, 48),
        name="lru_inproj",
    )(x, mul, add, w)


def _outproj_kernel(y_ref, w_ref, x_ref, gp_ref, o_ref):
    z = jnp.dot(y_ref[...], w_ref[...], preferred_element_type=F32)
    ms = jnp.mean(z * z, axis=-1, keepdims=True)
    o_ref[...] = x_ref[...] + z * lax.rsqrt(ms + EPS) * gp_ref[...]


def _outproj(y, w, x, gp, *, tm):
    b, t, d = x.shape
    k = y.shape[-1]
    return pl.pallas_call(
        _outproj_kernel,
        grid=(b, t // tm),
        in_specs=[pl.BlockSpec((None, tm, k), lambda i, j: (i, j, 0)),
                  pl.BlockSpec((k, d), lambda i, j: (0, 0)),
                  pl.BlockSpec((None, tm, d), lambda i, j: (i, j, 0)),
                  pl.BlockSpec((None, 1, d), lambda i, j: (i, 0, 0))],
        out_specs=pl.BlockSpec((None, tm, d), lambda i, j: (i, j, 0)),
        out_shape=jax.ShapeDtypeStruct((b, t, d), F32),
        compiler_params=_params(2, 48),
        name="outproj",
    )(y, w, x, gp)


def _outproj_interleaved_kernel(y_ref, w_ref, x_ref, gp_ref, o_ref, z_scr):
    z = jnp.dot(y_ref[...], w_ref[...], preferred_element_type=F32)
    ms = jnp.mean(z * z, axis=-1, keepdims=True)
    zn = z * lax.rsqrt(ms + EPS) * gp_ref[...]
    n_slab = zn.shape[1] // LANES
    seg = zn.shape[0] // SUBLANES
    for d in range(n_slab):
        z_scr[d] = zn[:, d * LANES:(d + 1) * LANES]
    for d in range(n_slab):
        cols = slice(d * LANES, (d + 1) * LANES)
        for s in range(SUBLANES):
            rows = slice(s * seg, (s + 1) * seg)
            o_ref[rows, cols] = x_ref[rows, cols] + z_scr[d, pl.ds(s, seg, stride=SUBLANES), :]


def _outproj_interleaved(y, w, x, gp, *, tm):
    b, t, d = x.shape
    k = y.shape[-1]
    return pl.pallas_call(
        _outproj_interleaved_kernel,
        grid=(b, t // tm),
        in_specs=[pl.BlockSpec((None, tm, k), lambda i, j: (i, j, 0)),
                  pl.BlockSpec((k, d), lambda i, j: (0, 0)),
                  pl.BlockSpec((None, tm, d), lambda i, j: (i, j, 0)),
                  pl.BlockSpec((None, 1, d), lambda i, j: (i, 0, 0))],
        out_specs=pl.BlockSpec((None, tm, d), lambda i, j: (i, j, 0)),
        out_shape=jax.ShapeDtypeStruct((b, t, d), F32),
        scratch_shapes=[pltpu.VMEM((d // LANES, tm, LANES), F32)],
        compiler_params=_params(2, 48),
        name="outproj_interleaved",
    )(y, w, x, gp)


def _row_scale(x, dec):
    n = x.shape[1] // LANES
    return jnp.concatenate([x[:, i * LANES:(i + 1) * LANES] * dec for i in range(n)], axis=1)


def _chunk_pos():
    return lax.broadcasted_iota(jnp.int32, (RET_CHUNK, LANES), 0).astype(F32)


def _ret_bwd_kernel(lg_ref, k_ref, v_ref, s0_ref, sb_ref, sfin_ref, s_scr, dec_scr, *, nch):
    c = pl.program_id(1)

    @pl.when(c == 0)
    def _():
        s_scr[...] = s0_ref[...]
        pos = _chunk_pos()
        for h in range(RET_HEADS):
            dec_scr[h] = jnp.exp(lg_ref[1, h] * pos).astype(BF16)

    for ci in reversed(range(nch)):
        rows = slice(ci * RET_CHUNK, (ci + 1) * RET_CHUNK)
        for h in range(RET_HEADS):
            k = k_ref[rows, h * RET_DK:(h + 1) * RET_DK]
            v = v_ref[rows, h * RET_DV:(h + 1) * RET_DV]
            st = s_scr[h]
            sb_ref[ci, h] = st.astype(BF16)
            kd = _row_scale(k, dec_scr[h])
            kv = lax.dot_general(kd, v, (((0,), (0,)), ((), ())), preferred_element_type=F32)
            s_scr[h] = st * jnp.exp(lg_ref[1, h] * RET_CHUNK) + kv

    @pl.when(c == pl.num_programs(1) - 1)
    def _():
        sfin_ref[...] = s_scr[...]


def _ret_fwd_kernel(lg_ref, q_ref, k_ref, v_ref, g_ref, sb_ref, s0_ref,
                    y_ref, sfin_ref, s_scr, m_scr, dec_scr, *, nch):
    c = pl.program_id(1)

    @pl.when(c == 0)
    def _():
        s_scr[...] = s0_ref[...]
        ri = lax.broadcasted_iota(jnp.int32, (RET_CHUNK, RET_CHUNK), 0)
        ci = lax.broadcasted_iota(jnp.int32, (RET_CHUNK, RET_CHUNK), 1)
        rel = (ri - ci).astype(F32)
        for h in range(RET_HEADS):
            mf = jnp.where(rel >= 0, jnp.exp(lg_ref[0, h] * jnp.maximum(rel, 0.0)), 0.0)
            mb = jnp.where(rel <= 0, jnp.exp(lg_ref[1, h] * jnp.maximum(-rel, 0.0)), 0.0)
            m_scr[h] = mf + mb
        pos = _chunk_pos()
        for h in range(RET_HEADS):
            dec_scr[h, 0] = jnp.exp(lg_ref[0, h] * (pos + 1.0)).astype(BF16)
            dec_scr[h, 1] = jnp.exp(lg_ref[1, h] * (RET_CHUNK - pos)).astype(BF16)
            dec_scr[h, 2] = jnp.exp(lg_ref[0, h] * (RET_CHUNK - 1.0 - pos)).astype(BF16)

    for ci, h in [(ci, h) for ci in range(nch) for h in range(RET_HEADS)]:
        rows = slice(ci * RET_CHUNK, (ci + 1) * RET_CHUNK)
        vcols = slice(h * RET_DV, (h + 1) * RET_DV)
        q = q_ref[rows, h * RET_DK:(h + 1) * RET_DK]
        k = k_ref[rows, h * RET_DK:(h + 1) * RET_DK]
        v = v_ref[rows, vcols]
        st = s_scr[h]
        s = lax.dot_general(q, k, (((1,), (1,)), ((), ())), preferred_element_type=F32)
        o = jnp.dot((s * m_scr[h]).astype(BF16), v, preferred_element_type=F32)
        o = jnp.dot(_row_scale(q, dec_scr[h, 0]), st.astype(BF16), preferred_element_type=F32) + o
        o = jnp.dot(_row_scale(q, dec_scr[h, 1]), sb_ref[ci, h], preferred_element_type=F32) + o
        kv = lax.dot_general(_row_scale(k, dec_scr[h, 2]), v, (((0,), (0,)), ((), ())),
                             preferred_element_type=F32)
        s_scr[h] = st * jnp.exp(lg_ref[0, h] * RET_CHUNK) + kv
        dlt = o - jnp.mean(o, axis=-1, keepdims=True)
        var = jnp.mean(dlt * dlt, axis=-1, keepdims=True)
        gh = g_ref[rows, vcols]
        y_ref[rows, vcols] = ((dlt * lax.rsqrt(var + EPS)).astype(BF16)
                              * (gh * (jnp.tanh(gh) + 1.0)))

    @pl.when(c == pl.num_programs(1) - 1)
    def _():
        sfin_ref[...] = s_scr[...]


def _state_spec():
    return pl.BlockSpec((None, RET_HEADS, RET_DK, RET_DV), lambda i, j: (i, 0, 0, 0))


def _step_chunks(n_chunks):
    return RET_STEP_CHUNKS if n_chunks % RET_STEP_CHUNKS == 0 else 1


def _ret_bwd(p, lg, s0):
    b, t, _ = p.shape
    nc = t // RET_CHUNK
    nch = _step_chunks(nc)
    ns = nc // nch
    rev = lambda blk: (lambda i, j: (i, ns - 1 - j, blk))
    hist = pl.BlockSpec((None, nch, RET_HEADS, RET_DK, RET_DV),
                        lambda i, j: (i, ns - 1 - j, 0, 0, 0))
    return pl.pallas_call(
        functools.partial(_ret_bwd_kernel, nch=nch),
        grid=(b, ns),
        in_specs=[_smem_spec(),
                  pl.BlockSpec((None, nch * RET_CHUNK, RET_QK), rev(1)),
                  pl.BlockSpec((None, nch * RET_CHUNK, RET_V), rev(1)),
                  _state_spec()],
        out_specs=[hist, _state_spec()],
        out_shape=[jax.ShapeDtypeStruct((b, nc, RET_HEADS, RET_DK, RET_DV), BF16),
                   jax.ShapeDtypeStruct(s0.shape, F32)],
        scratch_shapes=[pltpu.VMEM((RET_HEADS, RET_DK, RET_DV), F32),
                        pltpu.VMEM((RET_HEADS, RET_CHUNK, LANES), BF16)],
        compiler_params=_params(2, 48),
        name="ret_bwd",
    )(lg, p, p, s0)


def _ret_fwd(p, sb, lg, s0):
    b, t, _ = p.shape
    nc = t // RET_CHUNK
    nch = _step_chunks(nc)
    rows = nch * RET_CHUNK
    fwd = lambda blk: (lambda i, j: (i, j, blk))
    return pl.pallas_call(
        functools.partial(_ret_fwd_kernel, nch=nch),
        grid=(b, nc // nch),
        in_specs=[_smem_spec(),
                  pl.BlockSpec((None, rows, RET_QK), fwd(0)),
                  pl.BlockSpec((None, rows, RET_QK), fwd(1)),
                  pl.BlockSpec((None, rows, RET_V), fwd(1)),
                  pl.BlockSpec((None, rows, RET_V), fwd(2)),
                  pl.BlockSpec((None, nch, RET_HEADS, RET_DK, RET_DV),
                               lambda i, j: (i, j, 0, 0, 0)),
                  _state_spec()],
        out_specs=[pl.BlockSpec((None, rows, RET_V), fwd(0)), _state_spec()],
        out_shape=[jax.ShapeDtypeStruct((b, t, RET_V), BF16),
                   jax.ShapeDtypeStruct(s0.shape, F32)],
        scratch_shapes=[pltpu.VMEM((RET_HEADS, RET_DK, RET_DV), F32),
                        pltpu.VMEM((RET_HEADS, RET_CHUNK, RET_CHUNK), F32),
                        pltpu.VMEM((RET_HEADS, 3, RET_CHUNK, LANES), BF16)],
        compiler_params=_params(2, 48),
        name="ret_fwd",
    )(lg, p, p, p, p, sb, s0)


def _lru_conv(xr_ref, halo_ref, cw_ref, cb_ref, ext_scr, xc_ref, tt):
    sub = lax.broadcasted_iota(jnp.int32, (SUBLANES, LRU_BW), 0)
    left = CONV_W // 2
    for nb in range(LRU_BLOCKS):
        ext_scr[nb, left * SUBLANES:left * SUBLANES + tt, :] = xr_ref[nb]
        for k in range(left):
            src = xr_ref[nb, tt - (left - k) * SUBLANES:tt - (left - k - 1) * SUBLANES, :]
            ext_scr[nb, k * SUBLANES:(k + 1) * SUBLANES, :] = jnp.where(
                sub == 0, halo_ref[nb, k:k + 1, :], pltpu.roll(src, 1, 0))
        for k in range(CONV_W - 1 - left):
            src = xr_ref[nb, k * SUBLANES:(k + 1) * SUBLANES, :]
            ext_scr[nb, (left + k) * SUBLANES + tt:(left + k + 1) * SUBLANES + tt, :] = jnp.where(
                sub == SUBLANES - 1, halo_ref[nb, left + k:left + k + 1, :],
                pltpu.roll(src, SUBLANES - 1, 0))
        xc = cb_ref[nb:nb + 1, :]
        for j in range(CONV_W):
            xc = xc + ext_scr[nb, j * SUBLANES:j * SUBLANES + tt, :] * cw_ref[j, nb:nb + 1, :]
        xc_ref[nb] = xc


def _lru_coeffs(xc_ref, wg_ref, bg_ref, kh_ref, a_scr, u_scr):
    for nb in range(LRU_BLOCKS):
        xc = xc_ref[nb]
        res = jnp.dot(xc.astype(BF16), wg_ref[nb], preferred_element_type=F32)
        t_r = jnp.tanh(res[:, :LRU_BW] + bg_ref[0, nb:nb + 1, :])
        t_g = jnp.tanh(res[:, LRU_BW:] + bg_ref[1, nb:nb + 1, :])
        kh = kh_ref[nb:nb + 1, :]
        p = jnp.tanh(t_r * kh + kh)
        w = 1.0 / (1.0 + p)
        a_scr[nb] = (1.0 - p) * w
        sqrt_p = jnp.where(p > 0.0, p * lax.rsqrt(p), 0.0)
        u_scr[nb] = sqrt_p * w * (t_g + 1.0) * xc


def _lru_scan(a_scr, u_scr, dst_ref, carry_scr, tt, reverse):
    n = tt // SUBLANES
    shape = (LRU_BLOCKS, SUBLANES, LRU_BW)
    sub = lax.broadcasted_iota(jnp.int32, shape, 1)
    steps = [slice(g * SUBLANES, (g + 1) * SUBLANES) for g in range(n)]
    if reverse:
        steps = steps[::-1]

    a_seg = a_scr[:, steps[0], :]
    h_seg = u_scr[:, steps[0], :]
    for rows in steps[1:]:
        a = a_scr[:, rows, :]
        h_seg = a * h_seg + u_scr[:, rows, :]
        a_seg = a_seg * a
    for d in (1, 2, 4):
        ok = (sub < SUBLANES - d) if reverse else (sub >= d)
        shift = SUBLANES - d if reverse else d
        a_s = jnp.where(ok, pltpu.roll(a_seg, shift, 1), 1.0)
        h_s = jnp.where(ok, pltpu.roll(h_seg, shift, 1), 0.0)
        h_seg = a_seg * h_s + h_seg
        a_seg = a_seg * a_s
    carry = carry_scr[...]
    h_end = a_seg * carry + h_seg
    if reverse:
        h_in = jnp.where(sub == SUBLANES - 1, carry, pltpu.roll(h_end, SUBLANES - 1, 1))
        last = h_end[:, 0:1, :]
    else:
        h_in = jnp.where(sub == 0, carry, pltpu.roll(h_end, 1, 1))
        last = h_end[:, SUBLANES - 1:SUBLANES, :]
    carry_scr[...] = jnp.broadcast_to(last, shape)

    h = h_in
    for rows in steps:
        h = a_scr[:, rows, :] * h + u_scr[:, rows, :]
        dst_ref[:, rows, :] = h


def _lru_bwd_kernel(xr_ref, halo_ref, cw_ref, cb_ref, wg_ref, bg_ref, kh_ref, h0_ref,
                    hb_ref, xc_ref, hfin_ref, ext_scr, a_scr, u_scr, carry_scr, *, tt):
    i = pl.program_id(1)

    @pl.when(i == 0)
    def _():
        carry_scr[...] = h0_ref[...]

    _lru_conv(xr_ref, halo_ref, cw_ref, cb_ref, ext_scr, xc_ref, tt)
    _lru_coeffs(xc_ref, wg_ref, bg_ref, kh_ref, a_scr, u_scr)
    _lru_scan(a_scr, u_scr, hb_ref, carry_scr, tt, True)

    @pl.when(i == pl.num_programs(1) - 1)
    def _():
        hfin_ref[...] = carry_scr[...]


def _lru_fwd_kernel(xc_ref, wg_ref, bg_ref, kh_ref, h0_ref, hb_ref, g_ref,
                    y_ref, hfin_ref, a_scr, u_scr, carry_scr, *, tt):
    i = pl.program_id(1)

    @pl.when(i == 0)
    def _():
        carry_scr[...] = h0_ref[...]

    _lru_coeffs(xc_ref, wg_ref, bg_ref, kh_ref, a_scr, u_scr)
    _lru_scan(a_scr, u_scr, u_scr, carry_scr, tt, False)
    for nb in range(LRU_BLOCKS):
        gh = g_ref[nb]
        y = (u_scr[nb] + hb_ref[nb]) * (gh * (jnp.tanh(gh) + 1.0))
        y_ref[:, nb * LRU_BW:(nb + 1) * LRU_BW] = y.astype(BF16)

    @pl.when(i == pl.num_programs(1) - 1)
    def _():
        hfin_ref[...] = carry_scr[...]


def _lru_specs(b, t, tt, reverse):
    nblk = t // tt
    tidx = (lambda j: nblk - 1 - j) if reverse else (lambda j: j)
    slab = pl.BlockSpec((None, LRU_BLOCKS, tt, LRU_BW), lambda i, j: (i, 0, tidx(j), 0))
    state = pl.BlockSpec((None, LRU_BLOCKS, SUBLANES, LRU_BW), lambda i, j: (i, 0, 0, 0))
    halo = pl.BlockSpec((None, None, LRU_BLOCKS, SUBLANES, LRU_BW),
                        lambda i, j: (i, tidx(j), 0, 0, 0))
    gate = [pl.BlockSpec((LRU_BLOCKS, LRU_BW, 2 * LRU_BW), lambda i, j: (0, 0, 0)),
            pl.BlockSpec((2, LRU_BLOCKS, LRU_BW), lambda i, j: (0, 0, 0)),
            pl.BlockSpec((LRU_BLOCKS, LRU_BW), lambda i, j: (0, 0))]
    return nblk, slab, state, halo, gate


def _lru_bwd(xr, halo, cw, cb, wg, bg, kh, h0, *, tt):
    b, _, t, _ = xr.shape
    nblk, slab, state, halo_spec, gate = _lru_specs(b, t, tt, True)
    slab_shape = jax.ShapeDtypeStruct(xr.shape, F32)
    tile = (LRU_BLOCKS, tt, LRU_BW)
    return pl.pallas_call(
        functools.partial(_lru_bwd_kernel, tt=tt),
        grid=(b, nblk),
        in_specs=[slab, halo_spec,
                  pl.BlockSpec((CONV_W, LRU_BLOCKS, LRU_BW), lambda i, j: (0, 0, 0)),
                  pl.BlockSpec((LRU_BLOCKS, LRU_BW), lambda i, j: (0, 0))] + gate + [state],
        out_specs=[slab, slab, state],
        out_shape=[slab_shape, slab_shape, jax.ShapeDtypeStruct(h0.shape, F32)],
        scratch_shapes=[pltpu.VMEM((LRU_BLOCKS, tt + (CONV_W - 1) * SUBLANES, LRU_BW), F32),
                        pltpu.VMEM(tile, F32), pltpu.VMEM(tile, F32),
                        pltpu.VMEM((LRU_BLOCKS, SUBLANES, LRU_BW), F32)],
        compiler_params=_params(2, 48),
        name="lru_bwd",
    )(xr, halo, cw, cb, wg, bg, kh, h0)


def _lru_fwd(xc, wg, bg, kh, h0, hb, g, *, tt):
    b, _, t, _ = xc.shape
    nblk, slab, state, _, gate = _lru_specs(b, t, tt, False)
    tile = (LRU_BLOCKS, tt, LRU_BW)
    return pl.pallas_call(
        functools.partial(_lru_fwd_kernel, tt=tt),
        grid=(b, nblk),
        in_specs=[slab] + gate + [state, slab, slab],
        out_specs=[pl.BlockSpec((None, tt, LRU_WIDTH), lambda i, j: (i, j, 0)), state],
        out_shape=[jax.ShapeDtypeStruct((b, t, LRU_WIDTH), BF16),
                   jax.ShapeDtypeStruct(h0.shape, F32)],
        scratch_shapes=[pltpu.VMEM(tile, F32), pltpu.VMEM(tile, F32),
                        pltpu.VMEM((LRU_BLOCKS, SUBLANES, LRU_BW), F32)],
        compiler_params=_params(2, 48),
        name="lru_fwd",
    )(xc, wg, bg, kh, h0, hb, g)


def _conv_halo(xr, tt):
    b, nbk, t, w = xr.shape
    nblk = t // tt
    r = xr.reshape(b, nbk, nblk, tt, w)
    z = jnp.zeros((b, nbk, 1, w), F32)
    last = tt - 1
    prev2 = jnp.concatenate([z, r[:, :, :-1, last - SUBLANES, :]], axis=2)
    prev1 = jnp.concatenate([z, r[:, :, :-1, last, :]], axis=2)
    nxt = jnp.concatenate([r[:, :, 1:, 0, :], z], axis=2)
    pad = jnp.zeros_like(prev1)
    halo = jnp.stack([prev2, prev1, nxt] + [pad] * (SUBLANES - 3), axis=3)
    return halo.transpose(0, 2, 1, 3, 4)


def _rope_tables(n_tok):
    n_rows = n_tok // GRID_W
    row = jnp.repeat(jnp.arange(n_rows, dtype=F32), GRID_W)
    col = jnp.tile(jnp.arange(GRID_W, dtype=F32), n_rows)
    n_freq = RET_DK // 4
    inv = ROPE_BASE ** (-jnp.arange(n_freq, dtype=F32) / n_freq)
    ang = jnp.concatenate([row[:, None] * inv, col[:, None] * inv], axis=-1)
    return jnp.cos(ang), jnp.sin(ang)


def _deinterleave_heads(w):
    d = w.shape[0]
    return w.reshape(d, RET_HEADS, HALF_DK, 2).transpose(0, 1, 3, 2).reshape(d, RET_QK)


def _retention_layer(x, s_ctx, mods, cos, sin, w_in, log_decay, gn_g, w_out, need_ctx):
    (mul_l, add_l, gp_l), (mul_c, add_c, gp_c) = mods
    b = x.shape[0]
    lc = s_ctx.shape[1]
    w = jnp.concatenate([_deinterleave_heads(w_in[:, :RET_QK]),
                         _deinterleave_heads(w_in[:, RET_QK:2 * RET_QK]),
                         w_in[:, 2 * RET_QK:2 * RET_QK + RET_V],
                         0.5 * w_in[:, 2 * RET_QK + RET_V:]], axis=1).astype(BF16)
    lg = -jnp.abs(log_decay.astype(F32))
    w_o = (gn_g.astype(F32)[:, None] * w_out).astype(BF16)

    p_l = _ret_inproj(x, mul_l, add_l, w, cos, sin, rope=True, tm=ROW_TILE)
    p_c = _ret_inproj(s_ctx, mul_c, add_c, w, cos[:lc], sin[:lc], rope=False, tm=lc)
    zero = jnp.zeros((b, RET_HEADS, RET_DK, RET_DV), F32)
    sb_c, sb_fin = _ret_bwd(p_c, lg, zero)
    sb_l, _ = _ret_bwd(p_l, lg, sb_fin)
    y_c, sf_c = _ret_fwd(p_c, sb_c, lg, zero)
    y_l, _ = _ret_fwd(p_l, sb_l, lg, sf_c)
    x = _outproj(y_l, w_o, x, gp_l, tm=ROW_TILE)
    if need_ctx:
        s_ctx = _outproj(y_c, w_o, s_ctx, gp_c, tm=lc)
    return x, s_ctx


def _lru_layer(x, s_ctx, mods, w_in, conv_w, conv_b, w_a, b_a, w_x, b_x, lam, w_out, need_ctx):
    (mul_l, add_l, gp_l), (mul_c, add_c, gp_c) = mods
    b = x.shape[0]
    lc = s_ctx.shape[1]
    w = jnp.concatenate([w_in[:, :LRU_WIDTH], 0.5 * w_in[:, LRU_WIDTH:]], axis=1).astype(BF16)
    w_o = w_out.astype(BF16)
    cw = conv_w.astype(F32).reshape(CONV_W, LRU_BLOCKS, LRU_BW)
    cb = conv_b.astype(F32).reshape(LRU_BLOCKS, LRU_BW)
    kh = (0.25 * LRU_C) * jax.nn.softplus(-lam.astype(F32))

    def gate_params(d):
        wg = (0.5 * jnp.concatenate([w_a[d], w_x[d]], axis=-1)).astype(BF16)
        bg = (0.5 * jnp.stack([b_a[d], b_x[d]])).astype(F32).reshape(2, LRU_BLOCKS, LRU_BW)
        return wg, bg, kh[d].reshape(LRU_BLOCKS, LRU_BW)

    xr_l, g_l = _lru_inproj(x, mul_l, add_l, w, tm=ROW_TILE)
    xr_c, g_c = _lru_inproj(s_ctx, mul_c, add_c, w, tm=lc)
    zero = jnp.zeros((b, LRU_BLOCKS, SUBLANES, LRU_BW), F32)
    hb_c, xc_c, hb_fin = _lru_bwd(xr_c, _conv_halo(xr_c, lc), cw, cb, *gate_params(1), zero, tt=lc)
    hb_l, xc_l, _ = _lru_bwd(xr_l, _conv_halo(xr_l, ROW_TILE), cw, cb, *gate_params(1), hb_fin,
                             tt=ROW_TILE)
    y_c, hf_fin = _lru_fwd(xc_c, *gate_params(0), zero, hb_c, g_c, tt=lc)
    y_l, _ = _lru_fwd(xc_l, *gate_params(0), hf_fin, hb_l, g_l, tt=ROW_TILE)
    x = _outproj_interleaved(y_l, w_o, x, gp_l, tm=ROW_TILE)
    if need_ctx:
        s_ctx = _outproj_interleaved(y_c, w_o, s_ctx, gp_c, tm=lc)
    return x, s_ctx


def kernel(x, c, ctx, c_ctx, mod_w, mod_b, norm_pre, norm_post, ret_w_in, ret_log_decay, ret_gn,
           ret_w_out, lru_w_in, lru_conv_w, lru_conv_b, lru_w_a, lru_b_a, lru_w_x, lru_b_x,
           lru_lambda, lru_w_out):
    b, n_tok, d = x.shape
    depth = mod_w.shape[0]
    assert n_tok % ROW_TILE == 0 and n_tok % RET_CHUNK == 0
    assert ctx.shape[1] % RET_CHUNK == 0 and ctx.shape[1] <= ROW_TILE
    cos, sin = _rope_tables(n_tok)

    rows = -(-(b + 1) // SUBLANES) * SUBLANES
    cc = jnp.concatenate([c, c_ctx[None, :], jnp.zeros((rows - b - 1, d), F32)], axis=0)
    mod = _modulation(cc, mod_w, mod_b)

    s_ctx = ctx
    for i in range(depth):
        shift, scale, gate = jnp.split(mod[i], 3, axis=-1)
        mods = []
        for sel in (slice(0, b), slice(b, b + 1)):
            mul = jnp.broadcast_to(norm_pre[i] * (1.0 + scale[sel]), (b, d))[:, None, :]
            add = jnp.broadcast_to(shift[sel], (b, d))[:, None, :]
            gp = jnp.broadcast_to(gate[sel] * norm_post[i], (b, d))[:, None, :]
            mods.append((mul, add, gp))
        need_ctx = i < depth - 1
        j = i // 2
        if i % 2 == 0:
            x, s_ctx = _retention_layer(x, s_ctx, mods, cos, sin, ret_w_in[j], ret_log_decay[j],
                                        ret_gn[j], ret_w_out[j], need_ctx)
        else:
            x, s_ctx = _lru_layer(x, s_ctx, mods, lru_w_in[j], lru_conv_w[j], lru_conv_b[j],
                                  lru_w_a[j], lru_b_a[j], lru_w_x[j], lru_b_x[j],
                                  lru_lambda[j], lru_w_out[j], need_ctx)
    return x
```

```python
import functools

import jax
import jax.numpy as jnp
from jax import lax
from jax.experimental import pallas as pl
from jax.experimental.pallas import tpu as pltpu

F32 = jnp.float32
BF16 = jnp.bfloat16

EPS = 1e-6
GRID_W = 64
ROPE_BASE = 10000.0

RET_HEADS = 4
RET_DK = 256
RET_DV = 512
RET_QK = RET_HEADS * RET_DK
RET_V = RET_HEADS * RET_DV
RET_IN = 2 * RET_QK + 2 * RET_V
HALF_DK = RET_DK // 2

LRU_BLOCKS = 10
LRU_BW = 128
LRU_WIDTH = LRU_BLOCKS * LRU_BW
CONV_W = 4
LRU_C = 8.0

LANES = 128
SUBLANES = 8
MIB = 1024 * 1024

RET_CHUNK = 256
RET_STEP_CHUNKS = 2
ROW_TILE = 512


def _params(n_axes, vmem_mib):
    return pltpu.CompilerParams(
        dimension_semantics=("arbitrary",) * n_axes,
        vmem_limit_bytes=vmem_mib * MIB)


def _smem_spec():
    return pl.BlockSpec(memory_space=pltpu.SMEM)


def _mod_kernel(cc_ref, w_ref, b_ref, o_ref):
    cc = cc_ref[...]
    act = cc * jax.nn.sigmoid(cc)
    o_ref[...] = jnp.dot(act.astype(BF16), w_ref[...].astype(BF16),
                         preferred_element_type=F32) + b_ref[...]


def _modulation(cc, mod_w, mod_b):
    depth, d, d3 = mod_w.shape
    rows = cc.shape[0]
    return pl.pallas_call(
        _mod_kernel,
        grid=(depth, d3 // d),
        in_specs=[pl.BlockSpec((rows, d), lambda i, j: (0, 0)),
                  pl.BlockSpec((None, d, d), lambda i, j: (i, 0, j)),
                  pl.BlockSpec((None, 1, d), lambda i, j: (i, 0, j))],
        out_specs=pl.BlockSpec((None, rows, d), lambda i, j: (i, 0, j)),
        out_shape=jax.ShapeDtypeStruct((depth, rows, d3), F32),
        compiler_params=_params(2, 32),
        name="modulation",
    )(cc, mod_w, mod_b.reshape(depth, 1, d3))


def _norm_modulate(x_ref, mul_ref, add_ref, h_scr):
    x = x_ref[...]
    ms = jnp.mean(x * x, axis=-1, keepdims=True)
    h = x * lax.rsqrt(ms + EPS) * mul_ref[...] + add_ref[...]
    h_scr[...] = h.astype(BF16)


def _ret_inproj_kernel(x_ref, mul_ref, add_ref, w_ref, cos_ref, sin_ref, o_ref, h_scr, *, rope):
    _norm_modulate(x_ref, mul_ref, add_ref, h_scr)
    hb = h_scr[...]
    scale = RET_DK ** -0.5
    for j in range(2 * RET_HEADS):
        cols = slice(j * RET_DK, (j + 1) * RET_DK)
        acc = jnp.dot(hb, w_ref[:, cols], preferred_element_type=F32)
        if j >= RET_HEADS:
            acc = acc * scale
        if rope:
            e, o = acc[:, :HALF_DK], acc[:, HALF_DK:]
            c, s = cos_ref[...], sin_ref[...]
            acc = jnp.concatenate([e * c - o * s, e * s + o * c], axis=1)
        o_ref[:, cols] = acc.astype(BF16)
    for j in range(2 * RET_QK // RET_DV, RET_IN // RET_DV):
        cols = slice(j * RET_DV, (j + 1) * RET_DV)
        o_ref[:, cols] = jnp.dot(hb, w_ref[:, cols], preferred_element_type=F32).astype(BF16)


def _ret_inproj(x, mul, add, w, cos, sin, *, rope, tm):
    b, t, d = x.shape
    kern = functools.partial(_ret_inproj_kernel, rope=rope)
    return pl.pallas_call(
        kern,
        grid=(b, t // tm),
        in_specs=[pl.BlockSpec((None, tm, d), lambda i, j: (i, j, 0)),
                  pl.BlockSpec((None, 1, d), lambda i, j: (i, 0, 0)),
                  pl.BlockSpec((None, 1, d), lambda i, j: (i, 0, 0)),
                  pl.BlockSpec((d, RET_IN), lambda i, j: (0, 0)),
                  pl.BlockSpec((tm, HALF_DK), lambda i, j: (j, 0)),
                  pl.BlockSpec((tm, HALF_DK), lambda i, j: (j, 0))],
        out_specs=pl.BlockSpec((None, tm, RET_IN), lambda i, j: (i, j, 0)),
        out_shape=jax.ShapeDtypeStruct((b, t, RET_IN), BF16),
        scratch_shapes=[pltpu.VMEM((tm, d), BF16)],
        compiler_params=_params(2, 56),
        name="ret_inproj_rope" if rope else "ret_inproj",
    )(x, mul, add, w, cos, sin)


def _store_interleaved(dst_ref, blk, val):
    seg = val.shape[0] // SUBLANES
    for s in range(SUBLANES):
        dst_ref[blk, pl.ds(s, seg, stride=SUBLANES), :] = val[s * seg:(s + 1) * seg, :]


def _lru_inproj_kernel(x_ref, mul_ref, add_ref, w_ref, xr_ref, g_ref, h_scr):
    _norm_modulate(x_ref, mul_ref, add_ref, h_scr)
    hb = h_scr[...]
    for j in range(LRU_WIDTH // (2 * LANES)):
        for dst_ref, base in ((xr_ref, 0), (g_ref, LRU_WIDTH)):
            cols = slice(base + j * 2 * LANES, base + (j + 1) * 2 * LANES)
            acc = jnp.dot(hb, w_ref[:, cols], preferred_element_type=F32)
            _store_interleaved(dst_ref, 2 * j, acc[:, :LANES])
            _store_interleaved(dst_ref, 2 * j + 1, acc[:, LANES:])


def _lru_inproj(x, mul, add, w, *, tm):
    b, t, d = x.shape
    slab = pl.BlockSpec((None, LRU_BLOCKS, tm, LRU_BW), lambda i, j: (i, 0, j, 0))
    return pl.pallas_call(
        _lru_inproj_kernel,
        grid=(b, t // tm),
        in_specs=[pl.BlockSpec((None, tm, d), lambda i, j: (i, j, 0)),
                  pl.BlockSpec((None, 1, d), lambda i, j: (i, 0, 0)),
                  pl.BlockSpec((None, 1, d), lambda i, j: (i, 0, 0)),
                  pl.BlockSpec((d, 2 * LRU_WIDTH), lambda i, j: (0, 0))],
        out_specs=[slab, slab],
        out_shape=[jax.ShapeDtypeStruct((b, LRU_BLOCKS, t, LRU_BW), F32),
                   jax.ShapeDtypeStruct((b, LRU_BLOCKS, t, LRU_BW), F32)],
        scratch_shapes=[pltpu.VMEM((tm, d), BF16)],
        compiler_params=_params(2, 48),
        name="lru_inproj",
    )(x, mul, add, w)


def _gated_norm_proj(y, w_ref, gp_ref):
    z = jnp.dot(y, w_ref[...], preferred_element_type=F32)
    ms = jnp.mean(z * z, axis=-1, keepdims=True)
    return z * lax.rsqrt(ms + EPS) * gp_ref[...]


def _residual_out(x_ref, zn, o_ref, rows):
    o_ref[rows, :] = x_ref[rows, :] + zn


def _residual_out_interleaved(x_ref, zn, o_ref, z_scr):
    n_slab = zn.shape[1] // LANES
    seg = zn.shape[0] // SUBLANES
    for d in range(n_slab):
        z_scr[d] = zn[:, d * LANES:(d + 1) * LANES]
    for d in range(n_slab):
        cols = slice(d * LANES, (d + 1) * LANES)
        for s in range(SUBLANES):
            rows = slice(s * seg, (s + 1) * seg)
            o_ref[rows, cols] = x_ref[rows, cols] + z_scr[d, pl.ds(s, seg, stride=SUBLANES), :]


def _row_scale(x, dec):
    n = x.shape[1] // LANES
    return jnp.concatenate([x[:, i * LANES:(i + 1) * LANES] * dec for i in range(n)], axis=1)


def _chunk_pos():
    return lax.broadcasted_iota(jnp.int32, (RET_CHUNK, LANES), 0).astype(F32)


def _ret_bwd_kernel(lg_ref, k_ref, v_ref, s0_ref, sb_ref, sfin_ref, s_scr, dec_scr, *, nch):
    c = pl.program_id(1)

    @pl.when(c == 0)
    def _():
        s_scr[...] = s0_ref[...]
        pos = _chunk_pos()
        for h in range(RET_HEADS):
            dec_scr[h] = jnp.exp(lg_ref[1, h] * pos).astype(BF16)

    for ci in reversed(range(nch)):
        rows = slice(ci * RET_CHUNK, (ci + 1) * RET_CHUNK)
        for h in range(RET_HEADS):
            k = k_ref[rows, h * RET_DK:(h + 1) * RET_DK]
            v = v_ref[rows, h * RET_DV:(h + 1) * RET_DV]
            st = s_scr[h]
            sb_ref[ci, h] = st.astype(BF16)
            kd = _row_scale(k, dec_scr[h])
            kv = lax.dot_general(kd, v, (((0,), (0,)), ((), ())), preferred_element_type=F32)
            s_scr[h] = st * jnp.exp(lg_ref[1, h] * RET_CHUNK) + kv

    @pl.when(c == pl.num_programs(1) - 1)
    def _():
        sfin_ref[...] = s_scr[...]


def _ret_fwd_kernel(lg_ref, q_ref, k_ref, v_ref, g_ref, sb_ref, s0_ref, w_ref, x_ref, gp_ref,
                    xo_ref, sfin_ref, s_scr, m_scr, dec_scr, y_scr, *, nch):
    c = pl.program_id(1)

    @pl.when(c == 0)
    def _():
        s_scr[...] = s0_ref[...]
        ri = lax.broadcasted_iota(jnp.int32, (RET_CHUNK, RET_CHUNK), 0)
        ci = lax.broadcasted_iota(jnp.int32, (RET_CHUNK, RET_CHUNK), 1)
        rel = (ri - ci).astype(F32)
        for h in range(RET_HEADS):
            mf = jnp.where(rel >= 0, jnp.exp(lg_ref[0, h] * jnp.maximum(rel, 0.0)), 0.0)
            mb = jnp.where(rel <= 0, jnp.exp(lg_ref[1, h] * jnp.maximum(-rel, 0.0)), 0.0)
            m_scr[h] = mf + mb
        pos = _chunk_pos()
        for h in range(RET_HEADS):
            dec_scr[h, 0] = jnp.exp(lg_ref[0, h] * (pos + 1.0)).astype(BF16)
            dec_scr[h, 1] = jnp.exp(lg_ref[1, h] * (RET_CHUNK - pos)).astype(BF16)
            dec_scr[h, 2] = jnp.exp(lg_ref[0, h] * (RET_CHUNK - 1.0 - pos)).astype(BF16)

    for ci, h in [(ci, h) for ci in range(nch) for h in range(RET_HEADS)]:
        rows = slice(ci * RET_CHUNK, (ci + 1) * RET_CHUNK)
        vcols = slice(h * RET_DV, (h + 1) * RET_DV)
        q = q_ref[rows, h * RET_DK:(h + 1) * RET_DK]
        k = k_ref[rows, h * RET_DK:(h + 1) * RET_DK]
        v = v_ref[rows, vcols]
        st = s_scr[h]
        s = lax.dot_general(q, k, (((1,), (1,)), ((), ())), preferred_element_type=F32)
        o = jnp.dot((s * m_scr[h]).astype(BF16), v, preferred_element_type=F32)
        o = jnp.dot(_row_scale(q, dec_scr[h, 0]), st.astype(BF16), preferred_element_type=F32) + o
        o = jnp.dot(_row_scale(q, dec_scr[h, 1]), sb_ref[ci, h], preferred_element_type=F32) + o
        kv = lax.dot_general(_row_scale(k, dec_scr[h, 2]), v, (((0,), (0,)), ((), ())),
                             preferred_element_type=F32)
        s_scr[h] = st * jnp.exp(lg_ref[0, h] * RET_CHUNK) + kv
        dlt = o - jnp.mean(o, axis=-1, keepdims=True)
        var = jnp.mean(dlt * dlt, axis=-1, keepdims=True)
        gh = g_ref[rows, vcols]
        y_scr[rows, vcols] = ((dlt * lax.rsqrt(var + EPS)).astype(BF16)
                              * (gh * (jnp.tanh(gh) + 1.0)))
        if h == RET_HEADS - 1:
            _residual_out(x_ref, _gated_norm_proj(y_scr[rows, :], w_ref, gp_ref), xo_ref, rows)

    @pl.when(c == pl.num_programs(1) - 1)
    def _():
        sfin_ref[...] = s_scr[...]


def _state_spec():
    return pl.BlockSpec((None, RET_HEADS, RET_DK, RET_DV), lambda i, j: (i, 0, 0, 0))


def _step_chunks(n_chunks):
    return RET_STEP_CHUNKS if n_chunks % RET_STEP_CHUNKS == 0 else 1


def _ret_bwd(p, lg, s0):
    b, t, _ = p.shape
    nc = t // RET_CHUNK
    nch = _step_chunks(nc)
    ns = nc // nch
    rev = lambda blk: (lambda i, j: (i, ns - 1 - j, blk))
    hist = pl.BlockSpec((None, nch, RET_HEADS, RET_DK, RET_DV),
                        lambda i, j: (i, ns - 1 - j, 0, 0, 0))
    return pl.pallas_call(
        functools.partial(_ret_bwd_kernel, nch=nch),
        grid=(b, ns),
        in_specs=[_smem_spec(),
                  pl.BlockSpec((None, nch * RET_CHUNK, RET_QK), rev(1)),
                  pl.BlockSpec((None, nch * RET_CHUNK, RET_V), rev(1)),
                  _state_spec()],
        out_specs=[hist, _state_spec()],
        out_shape=[jax.ShapeDtypeStruct((b, nc, RET_HEADS, RET_DK, RET_DV), BF16),
                   jax.ShapeDtypeStruct(s0.shape, F32)],
        scratch_shapes=[pltpu.VMEM((RET_HEADS, RET_DK, RET_DV), F32),
                        pltpu.VMEM((RET_HEADS, RET_CHUNK, LANES), BF16)],
        compiler_params=_params(2, 48),
        name="ret_bwd",
    )(lg, p, p, s0)


def _ret_fwd(p, sb, lg, s0, w_out, x, gp):
    b, t, _ = p.shape
    d = x.shape[-1]
    nc = t // RET_CHUNK
    nch = _step_chunks(nc)
    rows = nch * RET_CHUNK
    fwd = lambda blk: (lambda i, j: (i, j, blk))
    resid = pl.BlockSpec((None, rows, d), fwd(0))
    return pl.pallas_call(
        functools.partial(_ret_fwd_kernel, nch=nch),
        grid=(b, nc // nch),
        in_specs=[_smem_spec(),
                  pl.BlockSpec((None, rows, RET_QK), fwd(0)),
                  pl.BlockSpec((None, rows, RET_QK), fwd(1)),
                  pl.BlockSpec((None, rows, RET_V), fwd(1)),
                  pl.BlockSpec((None, rows, RET_V), fwd(2)),
                  pl.BlockSpec((None, nch, RET_HEADS, RET_DK, RET_DV),
                               lambda i, j: (i, j, 0, 0, 0)),
                  _state_spec(),
                  pl.BlockSpec((RET_V, d), lambda i, j: (0, 0)),
                  resid,
                  pl.BlockSpec((None, 1, d), lambda i, j: (i, 0, 0))],
        out_specs=[resid, _state_spec()],
        out_shape=[jax.ShapeDtypeStruct((b, t, d), F32),
                   jax.ShapeDtypeStruct(s0.shape, F32)],
        scratch_shapes=[pltpu.VMEM((RET_HEADS, RET_DK, RET_DV), F32),
                        pltpu.VMEM((RET_HEADS, RET_CHUNK, RET_CHUNK), F32),
                        pltpu.VMEM((RET_HEADS, 3, RET_CHUNK, LANES), BF16),
                        pltpu.VMEM((rows, RET_V), BF16)],
        compiler_params=_params(2, 56),
        name="ret_fwd",
    )(lg, p, p, p, p, sb, s0, w_out, x, gp)


def _lru_conv(xr_ref, halo_ref, cw_ref, cb_ref, ext_scr, xc_ref, tt):
    sub = lax.broadcasted_iota(jnp.int32, (SUBLANES, LRU_BW), 0)
    left = CONV_W // 2
    for nb in range(LRU_BLOCKS):
        ext_scr[nb, left * SUBLANES:left * SUBLANES + tt, :] = xr_ref[nb]
        for k in range(left):
            src = xr_ref[nb, tt - (left - k) * SUBLANES:tt - (left - k - 1) * SUBLANES, :]
            ext_scr[nb, k * SUBLANES:(k + 1) * SUBLANES, :] = jnp.where(
                sub == 0, halo_ref[nb, k:k + 1, :], pltpu.roll(src, 1, 0))
        for k in range(CONV_W - 1 - left):
            src = xr_ref[nb, k * SUBLANES:(k + 1) * SUBLANES, :]
            ext_scr[nb, (left + k) * SUBLANES + tt:(left + k + 1) * SUBLANES + tt, :] = jnp.where(
                sub == SUBLANES - 1, halo_ref[nb, left + k:left + k + 1, :],
                pltpu.roll(src, SUBLANES - 1, 0))
        xc = cb_ref[nb:nb + 1, :]
        for j in range(CONV_W):
            xc = xc + ext_scr[nb, j * SUBLANES:j * SUBLANES + tt, :] * cw_ref[j, nb:nb + 1, :]
        xc_ref[nb] = xc


def _lru_coeffs(xc_ref, wg_ref, bg_ref, kh_ref, a_scr, u_scr):
    for nb in range(LRU_BLOCKS):
        xc = xc_ref[nb]
        res = jnp.dot(xc.astype(BF16), wg_ref[nb], preferred_element_type=F32)
        t_r = jnp.tanh(res[:, :LRU_BW] + bg_ref[0, nb:nb + 1, :])
        t_g = jnp.tanh(res[:, LRU_BW:] + bg_ref[1, nb:nb + 1, :])
        kh = kh_ref[nb:nb + 1, :]
        p = jnp.tanh(t_r * kh + kh)
        w = 1.0 / (1.0 + p)
        a_scr[nb] = (1.0 - p) * w
        sqrt_p = jnp.where(p > 0.0, p * lax.rsqrt(p), 0.0)
        u_scr[nb] = sqrt_p * w * (t_g + 1.0) * xc


def _lru_scan(a_scr, u_scr, dst_ref, carry_scr, tt, reverse):
    n = tt // SUBLANES
    shape = (LRU_BLOCKS, SUBLANES, LRU_BW)
    sub = lax.broadcasted_iota(jnp.int32, shape, 1)
    steps = [slice(g * SUBLANES, (g + 1) * SUBLANES) for g in range(n)]
    if reverse:
        steps = steps[::-1]

    a_seg = a_scr[:, steps[0], :]
    h_seg = u_scr[:, steps[0], :]
    for rows in steps[1:]:
        a = a_scr[:, rows, :]
        h_seg = a * h_seg + u_scr[:, rows, :]
        a_seg = a_seg * a
    for d in (1, 2, 4):
        ok = (sub < SUBLANES - d) if reverse else (sub >= d)
        shift = SUBLANES - d if reverse else d
        a_s = jnp.where(ok, pltpu.roll(a_seg, shift, 1), 1.0)
        h_s = jnp.where(ok, pltpu.roll(h_seg, shift, 1), 0.0)
        h_seg = a_seg * h_s + h_seg
        a_seg = a_seg * a_s
    carry = carry_scr[...]
    h_end = a_seg * carry + h_seg
    if reverse:
        h_in = jnp.where(sub == SUBLANES - 1, carry, pltpu.roll(h_end, SUBLANES - 1, 1))
        last = h_end[:, 0:1, :]
    else:
        h_in = jnp.where(sub == 0, carry, pltpu.roll(h_end, 1, 1))
        last = h_end[:, SUBLANES - 1:SUBLANES, :]
    carry_scr[...] = jnp.broadcast_to(last, shape)

    h = h_in
    for rows in steps:
        h = a_scr[:, rows, :] * h + u_scr[:, rows, :]
        dst_ref[:, rows, :] = h


def _lru_bwd_kernel(xr_ref, halo_ref, cw_ref, cb_ref, wg_ref, bg_ref, kh_ref, h0_ref,
                    hb_ref, xc_ref, hfin_ref, ext_scr, a_scr, u_scr, carry_scr, *, tt):
    i = pl.program_id(1)

    @pl.when(i == 0)
    def _():
        carry_scr[...] = h0_ref[...]

    _lru_conv(xr_ref, halo_ref, cw_ref, cb_ref, ext_scr, xc_ref, tt)
    _lru_coeffs(xc_ref, wg_ref, bg_ref, kh_ref, a_scr, u_scr)
    _lru_scan(a_scr, u_scr, hb_ref, carry_scr, tt, True)

    @pl.when(i == pl.num_programs(1) - 1)
    def _():
        hfin_ref[...] = carry_scr[...]


def _lru_fwd_kernel(xc_ref, wg_ref, bg_ref, kh_ref, h0_ref, hb_ref, g_ref, w_ref, x_ref, gp_ref,
                    xo_ref, hfin_ref, a_scr, u_scr, carry_scr, y_scr, z_scr, *, tt):
    i = pl.program_id(1)

    @pl.when(i == 0)
    def _():
        carry_scr[...] = h0_ref[...]

    _lru_coeffs(xc_ref, wg_ref, bg_ref, kh_ref, a_scr, u_scr)
    _lru_scan(a_scr, u_scr, u_scr, carry_scr, tt, False)
    for nb in range(LRU_BLOCKS):
        gh = g_ref[nb]
        y = (u_scr[nb] + hb_ref[nb]) * (gh * (jnp.tanh(gh) + 1.0))
        y_scr[:, nb * LRU_BW:(nb + 1) * LRU_BW] = y.astype(BF16)
    _residual_out_interleaved(x_ref, _gated_norm_proj(y_scr[...], w_ref, gp_ref), xo_ref, z_scr)

    @pl.when(i == pl.num_programs(1) - 1)
    def _():
        hfin_ref[...] = carry_scr[...]


def _lru_specs(b, t, tt, reverse):
    nblk = t // tt
    tidx = (lambda j: nblk - 1 - j) if reverse else (lambda j: j)
    slab = pl.BlockSpec((None, LRU_BLOCKS, tt, LRU_BW), lambda i, j: (i, 0, tidx(j), 0))
    state = pl.BlockSpec((None, LRU_BLOCKS, SUBLANES, LRU_BW), lambda i, j: (i, 0, 0, 0))
    halo = pl.BlockSpec((None, None, LRU_BLOCKS, SUBLANES, LRU_BW),
                        lambda i, j: (i, tidx(j), 0, 0, 0))
    gate = [pl.BlockSpec((LRU_BLOCKS, LRU_BW, 2 * LRU_BW), lambda i, j: (0, 0, 0)),
            pl.BlockSpec((2, LRU_BLOCKS, LRU_BW), lambda i, j: (0, 0, 0)),
            pl.BlockSpec((LRU_BLOCKS, LRU_BW), lambda i, j: (0, 0))]
    return nblk, slab, state, halo, gate


def _lru_bwd(xr, halo, cw, cb, wg, bg, kh, h0, *, tt):
    b, _, t, _ = xr.shape
    nblk, slab, state, halo_spec, gate = _lru_specs(b, t, tt, True)
    slab_shape = jax.ShapeDtypeStruct(xr.shape, F32)
    tile = (LRU_BLOCKS, tt, LRU_BW)
    return pl.pallas_call(
        functools.partial(_lru_bwd_kernel, tt=tt),
        grid=(b, nblk),
        in_specs=[slab, halo_spec,
                  pl.BlockSpec((CONV_W, LRU_BLOCKS, LRU_BW), lambda i, j: (0, 0, 0)),
                  pl.BlockSpec((LRU_BLOCKS, LRU_BW), lambda i, j: (0, 0))] + gate + [state],
        out_specs=[slab, slab, state],
        out_shape=[slab_shape, slab_shape, jax.ShapeDtypeStruct(h0.shape, F32)],
        scratch_shapes=[pltpu.VMEM((LRU_BLOCKS, tt + (CONV_W - 1) * SUBLANES, LRU_BW), F32),
                        pltpu.VMEM(tile, F32), pltpu.VMEM(tile, F32),
                        pltpu.VMEM((LRU_BLOCKS, SUBLANES, LRU_BW), F32)],
        compiler_params=_params(2, 48),
        name="lru_bwd",
    )(xr, halo, cw, cb, wg, bg, kh, h0)


def _lru_fwd(xc, wg, bg, kh, h0, hb, g, w_out, x, gp, *, tt):
    b, _, t, _ = xc.shape
    d = x.shape[-1]
    nblk, slab, state, _, gate = _lru_specs(b, t, tt, False)
    tile = (LRU_BLOCKS, tt, LRU_BW)
    resid = pl.BlockSpec((None, tt, d), lambda i, j: (i, j, 0))
    return pl.pallas_call(
        functools.partial(_lru_fwd_kernel, tt=tt),
        grid=(b, nblk),
        in_specs=[slab] + gate + [state, slab, slab,
                                  pl.BlockSpec((LRU_WIDTH, d), lambda i, j: (0, 0)),
                                  resid,
                                  pl.BlockSpec((None, 1, d), lambda i, j: (i, 0, 0))],
        out_specs=[resid, state],
        out_shape=[jax.ShapeDtypeStruct((b, t, d), F32),
                   jax.ShapeDtypeStruct(h0.shape, F32)],
        scratch_shapes=[pltpu.VMEM(tile, F32), pltpu.VMEM(tile, F32),
                        pltpu.VMEM((LRU_BLOCKS, SUBLANES, LRU_BW), F32),
                        pltpu.VMEM((tt, LRU_WIDTH), BF16),
                        pltpu.VMEM((d // LANES, tt, LANES), F32)],
        compiler_params=_params(2, 56),
        name="lru_fwd",
    )(xc, wg, bg, kh, h0, hb, g, w_out, x, gp)


def _conv_halo(xr, tt):
    b, nbk, t, w = xr.shape
    nblk = t // tt
    r = xr.reshape(b, nbk, nblk, tt, w)
    z = jnp.zeros((b, nbk, 1, w), F32)
    last = tt - 1
    prev2 = jnp.concatenate([z, r[:, :, :-1, last - SUBLANES, :]], axis=2)
    prev1 = jnp.concatenate([z, r[:, :, :-1, last, :]], axis=2)
    nxt = jnp.concatenate([r[:, :, 1:, 0, :], z], axis=2)
    pad = jnp.zeros_like(prev1)
    halo = jnp.stack([prev2, prev1, nxt] + [pad] * (SUBLANES - 3), axis=3)
    return halo.transpose(0, 2, 1, 3, 4)


def _rope_tables(n_tok):
    n_rows = n_tok // GRID_W
    row = jnp.repeat(jnp.arange(n_rows, dtype=F32), GRID_W)
    col = jnp.tile(jnp.arange(GRID_W, dtype=F32), n_rows)
    n_freq = RET_DK // 4
    inv = ROPE_BASE ** (-jnp.arange(n_freq, dtype=F32) / n_freq)
    ang = jnp.concatenate([row[:, None] * inv, col[:, None] * inv], axis=-1)
    return jnp.cos(ang), jnp.sin(ang)


def _deinterleave_heads(w):
    d = w.shape[0]
    return w.reshape(d, RET_HEADS, HALF_DK, 2).transpose(0, 1, 3, 2).reshape(d, RET_QK)


def _retention_layer(x, s_ctx, mods, cos, sin, w_in, log_decay, gn_g, w_out):
    (mul_l, add_l, gp_l), (mul_c, add_c, gp_c) = mods
    b = x.shape[0]
    lc = s_ctx.shape[1]
    w = jnp.concatenate([_deinterleave_heads(w_in[:, :RET_QK]),
                         _deinterleave_heads(w_in[:, RET_QK:2 * RET_QK]),
                         w_in[:, 2 * RET_QK:2 * RET_QK + RET_V],
                         0.5 * w_in[:, 2 * RET_QK + RET_V:]], axis=1).astype(BF16)
    lg = -jnp.abs(log_decay.astype(F32))
    w_o = (gn_g.astype(F32)[:, None] * w_out).astype(BF16)

    p_l = _ret_inproj(x, mul_l, add_l, w, cos, sin, rope=True, tm=ROW_TILE)
    p_c = _ret_inproj(s_ctx, mul_c, add_c, w, cos[:lc], sin[:lc], rope=False, tm=lc)
    zero = jnp.zeros((b, RET_HEADS, RET_DK, RET_DV), F32)
    sb_c, sb_fin = _ret_bwd(p_c, lg, zero)
    sb_l, _ = _ret_bwd(p_l, lg, sb_fin)
    s_ctx, sf_c = _ret_fwd(p_c, sb_c, lg, zero, w_o, s_ctx, gp_c)
    x, _ = _ret_fwd(p_l, sb_l, lg, sf_c, w_o, x, gp_l)
    return x, s_ctx


def _lru_layer(x, s_ctx, mods, w_in, conv_w, conv_b, w_a, b_a, w_x, b_x, lam, w_out):
    (mul_l, add_l, gp_l), (mul_c, add_c, gp_c) = mods
    b = x.shape[0]
    lc = s_ctx.shape[1]
    w = jnp.concatenate([w_in[:, :LRU_WIDTH], 0.5 * w_in[:, LRU_WIDTH:]], axis=1).astype(BF16)
    w_o = w_out.astype(BF16)
    cw = conv_w.astype(F32).reshape(CONV_W, LRU_BLOCKS, LRU_BW)
    cb = conv_b.astype(F32).reshape(LRU_BLOCKS, LRU_BW)
    kh = (0.25 * LRU_C) * jax.nn.softplus(-lam.astype(F32))

    def gate_params(d):
        wg = (0.5 * jnp.concatenate([w_a[d], w_x[d]], axis=-1)).astype(BF16)
        bg = (0.5 * jnp.stack([b_a[d], b_x[d]])).astype(F32).reshape(2, LRU_BLOCKS, LRU_BW)
        return wg, bg, kh[d].reshape(LRU_BLOCKS, LRU_BW)

    xr_l, g_l = _lru_inproj(x, mul_l, add_l, w, tm=ROW_TILE)
    xr_c, g_c = _lru_inproj(s_ctx, mul_c, add_c, w, tm=lc)
    zero = jnp.zeros((b, LRU_BLOCKS, SUBLANES, LRU_BW), F32)
    hb_c, xc_c, hb_fin = _lru_bwd(xr_c, _conv_halo(xr_c, lc), cw, cb, *gate_params(1), zero, tt=lc)
    hb_l, xc_l, _ = _lru_bwd(xr_l, _conv_halo(xr_l, ROW_TILE), cw, cb, *gate_params(1), hb_fin,
                             tt=ROW_TILE)
    s_ctx, hf_fin = _lru_fwd(xc_c, *gate_params(0), zero, hb_c, g_c, w_o, s_ctx, gp_c, tt=lc)
    x, _ = _lru_fwd(xc_l, *gate_params(0), hf_fin, hb_l, g_l, w_o, x, gp_l, tt=ROW_TILE)
    return x, s_ctx


def kernel(x, c, ctx, c_ctx, mod_w, mod_b, norm_pre, norm_post, ret_w_in, ret_log_decay, ret_gn,
           ret_w_out, lru_w_in, lru_conv_w, lru_conv_b, lru_w_a, lru_b_a, lru_w_x, lru_b_x,
           lru_lambda, lru_w_out):
    b, n_tok, d = x.shape
    depth = mod_w.shape[0]
    assert n_tok % ROW_TILE == 0 and n_tok % RET_CHUNK == 0
    assert ctx.shape[1] % RET_CHUNK == 0 and ctx.shape[1] <= ROW_TILE
    cos, sin = _rope_tables(n_tok)

    rows = -(-(b + 1) // SUBLANES) * SUBLANES
    cc = jnp.concatenate([c, c_ctx[None, :], jnp.zeros((rows - b - 1, d), F32)], axis=0)
    mod = _modulation(cc, mod_w, mod_b)

    s_ctx = ctx
    for i in range(depth):
        shift, scale, gate = jnp.split(mod[i], 3, axis=-1)
        mods = []
        for sel in (slice(0, b), slice(b, b + 1)):
            mul = jnp.broadcast_to(norm_pre[i] * (1.0 + scale[sel]), (b, d))[:, None, :]
            add = jnp.broadcast_to(shift[sel], (b, d))[:, None, :]
            gp = jnp.broadcast_to(gate[sel] * norm_post[i], (b, d))[:, None, :]
            mods.append((mul, add, gp))
        j = i // 2
        if i % 2 == 0:
            x, s_ctx = _retention_layer(x, s_ctx, mods, cos, sin, ret_w_in[j], ret_log_decay[j],
                                        ret_gn[j], ret_w_out[j])
        else:
            x, s_ctx = _lru_layer(x, s_ctx, mods, lru_w_in[j], lru_conv_w[j], lru_conv_b[j],
                                  lru_w_a[j], lru_b_a[j], lru_w_x[j], lru_b_x[j],
                                  lru_lambda[j], lru_w_out[j])
    return x
```

```python
import functools

import jax
import jax.numpy as jnp
from jax import lax
from jax.experimental import pallas as pl
from jax.experimental.pallas import tpu as pltpu

F32 = jnp.float32
BF16 = jnp.bfloat16

EPS = 1e-6
GRID_W = 64
ROPE_BASE = 10000.0

RET_HEADS = 4
RET_DK = 256
RET_DV = 512
RET_QK = RET_HEADS * RET_DK
RET_V = RET_HEADS * RET_DV
RET_IN = 2 * RET_QK + 2 * RET_V
HALF_DK = RET_DK // 2

LRU_BLOCKS = 10
LRU_BW = 128
LRU_WIDTH = LRU_BLOCKS * LRU_BW
CONV_W = 4
LRU_C = 8.0

LANES = 128
SUBLANES = 8
MIB = 1024 * 1024

RET_CHUNK = 256
RET_STEP_CHUNKS = 2
ROW_TILE = 512


def _params(n_axes, vmem_mib):
    return pltpu.CompilerParams(
        dimension_semantics=("arbitrary",) * n_axes,
        vmem_limit_bytes=vmem_mib * MIB)


def _smem_spec():
    return pl.BlockSpec(memory_space=pltpu.SMEM)


def _mod_kernel(cc_ref, w_ref, b_ref, o_ref):
    cc = cc_ref[...]
    act = cc * jax.nn.sigmoid(cc)
    o_ref[...] = jnp.dot(act.astype(BF16), w_ref[...].astype(BF16),
                         preferred_element_type=F32) + b_ref[...]


def _modulation(cc, mod_w, mod_b):
    depth, d, d3 = mod_w.shape
    rows = cc.shape[0]
    return pl.pallas_call(
        _mod_kernel,
        grid=(depth, d3 // d),
        in_specs=[pl.BlockSpec((rows, d), lambda i, j: (0, 0)),
                  pl.BlockSpec((None, d, d), lambda i, j: (i, 0, j)),
                  pl.BlockSpec((None, 1, d), lambda i, j: (i, 0, j))],
        out_specs=pl.BlockSpec((None, rows, d), lambda i, j: (i, 0, j)),
        out_shape=jax.ShapeDtypeStruct((depth, rows, d3), F32),
        compiler_params=_params(2, 32),
        name="modulation",
    )(cc, mod_w, mod_b.reshape(depth, 1, d3))


def _norm_modulate(x_ref, mul_ref, add_ref, h_scr):
    x = x_ref[...]
    ms = jnp.mean(x * x, axis=-1, keepdims=True)
    h = x * lax.rsqrt(ms + EPS) * mul_ref[...] + add_ref[...]
    h_scr[...] = h.astype(BF16)


def _ret_inproj_kernel(x_ref, mul_ref, add_ref, w_ref, cos_ref, sin_ref, o_ref, h_scr, *, rope):
    _norm_modulate(x_ref, mul_ref, add_ref, h_scr)
    hb = h_scr[...]
    scale = RET_DK ** -0.5
    for j in range(2 * RET_HEADS):
        cols = slice(j * RET_DK, (j + 1) * RET_DK)
        acc = jnp.dot(hb, w_ref[:, cols], preferred_element_type=F32)
        if j >= RET_HEADS:
            acc = acc * scale
        if rope:
            e, o = acc[:, :HALF_DK], acc[:, HALF_DK:]
            c, s = cos_ref[...], sin_ref[...]
            acc = jnp.concatenate([e * c - o * s, e * s + o * c], axis=1)
        o_ref[:, cols] = acc.astype(BF16)
    for j in range(2 * RET_QK // RET_DV, RET_IN // RET_DV):
        cols = slice(j * RET_DV, (j + 1) * RET_DV)
        o_ref[:, cols] = jnp.dot(hb, w_ref[:, cols], preferred_element_type=F32).astype(BF16)


def _ret_inproj(x, mul, add, w, cos, sin, *, rope, tm):
    b, t, d = x.shape
    kern = functools.partial(_ret_inproj_kernel, rope=rope)
    return pl.pallas_call(
        kern,
        grid=(b, t // tm),
        in_specs=[pl.BlockSpec((None, tm, d), lambda i, j: (i, j, 0)),
                  pl.BlockSpec((None, 1, d), lambda i, j: (i, 0, 0)),
                  pl.BlockSpec((None, 1, d), lambda i, j: (i, 0, 0)),
                  pl.BlockSpec((d, RET_IN), lambda i, j: (0, 0)),
                  pl.BlockSpec((tm, HALF_DK), lambda i, j: (j, 0)),
                  pl.BlockSpec((tm, HALF_DK), lambda i, j: (j, 0))],
        out_specs=pl.BlockSpec((None, tm, RET_IN), lambda i, j: (i, j, 0)),
        out_shape=jax.ShapeDtypeStruct((b, t, RET_IN), BF16),
        scratch_shapes=[pltpu.VMEM((tm, d), BF16)],
        compiler_params=_params(2, 56),
        name="ret_inproj_rope" if rope else "ret_inproj",
    )(x, mul, add, w, cos, sin)


def _store_interleaved(dst_ref, blk, val):
    seg = val.shape[0] // SUBLANES
    for s in range(SUBLANES):
        dst_ref[blk, pl.ds(s, seg, stride=SUBLANES), :] = val[s * seg:(s + 1) * seg, :]


def _lru_inproj_kernel(x_ref, mul_ref, add_ref, w_ref, xr_ref, g_ref, edge_ref, h_scr, hs_scr):
    x = x_ref[...]
    ms = jnp.mean(x * x, axis=-1, keepdims=True)
    h = x * lax.rsqrt(ms + EPS) * mul_ref[...] + add_ref[...]
    tm, d = h.shape
    for s in range(d // LANES):
        _store_interleaved(hs_scr, s, h[:, s * LANES:(s + 1) * LANES])
    for s in range(d // LANES):
        h_scr[:, s * LANES:(s + 1) * LANES] = hs_scr[s].astype(BF16)
    hb = h_scr[...]
    last = tm - 1
    for j in range(LRU_WIDTH // (2 * LANES)):
        cols = slice(j * 2 * LANES, (j + 1) * 2 * LANES)
        acc = jnp.dot(hb, w_ref[:, cols], preferred_element_type=F32)
        gcols = slice(LRU_WIDTH + j * 2 * LANES, LRU_WIDTH + (j + 1) * 2 * LANES)
        gacc = jnp.dot(hb, w_ref[:, gcols], preferred_element_type=F32).astype(BF16)
        for half in range(2):
            blk = 2 * j + half
            lanes = slice(half * LANES, (half + 1) * LANES)
            xr_ref[blk] = acc[:, lanes]
            g_ref[blk] = gacc[:, lanes]
            edge_ref[blk, 0:1, :] = acc[last - SUBLANES:last - SUBLANES + 1, lanes]
            edge_ref[blk, 1:2, :] = acc[last:last + 1, lanes]
            edge_ref[blk, 2:3, :] = acc[0:1, lanes]
            edge_ref[blk, 3:SUBLANES, :] = jnp.zeros((SUBLANES - 3, LANES), F32)


def _lru_inproj(x, mul, add, w, *, tm):
    b, t, d = x.shape
    slab = pl.BlockSpec((None, LRU_BLOCKS, tm, LRU_BW), lambda i, j: (i, 0, j, 0))
    return pl.pallas_call(
        _lru_inproj_kernel,
        grid=(b, t // tm),
        in_specs=[pl.BlockSpec((None, tm, d), lambda i, j: (i, j, 0)),
                  pl.BlockSpec((None, 1, d), lambda i, j: (i, 0, 0)),
                  pl.BlockSpec((None, 1, d), lambda i, j: (i, 0, 0)),
                  pl.BlockSpec((d, 2 * LRU_WIDTH), lambda i, j: (0, 0))],
        out_specs=[slab, slab,
                   pl.BlockSpec((None, None, LRU_BLOCKS, SUBLANES, LRU_BW),
                                lambda i, j: (i, j, 0, 0, 0))],
        out_shape=[jax.ShapeDtypeStruct((b, LRU_BLOCKS, t, LRU_BW), F32),
                   jax.ShapeDtypeStruct((b, LRU_BLOCKS, t, LRU_BW), BF16),
                   jax.ShapeDtypeStruct((b, t // tm, LRU_BLOCKS, SUBLANES, LRU_BW), F32)],
        scratch_shapes=[pltpu.VMEM((tm, d), BF16),
                        pltpu.VMEM((d // LANES, tm, LANES), F32)],
        compiler_params=_params(2, 48),
        name="lru_inproj",
    )(x, mul, add, w)


def _gated_norm_proj(y, w_ref, gp_ref):
    z = jnp.dot(y, w_ref[...], preferred_element_type=F32)
    ms = jnp.mean(z * z, axis=-1, keepdims=True)
    return z * lax.rsqrt(ms + EPS) * gp_ref[...]


def _residual_out(x_ref, zn, o_ref, rows):
    o_ref[rows, :] = x_ref[rows, :] + zn


def _residual_out_interleaved(x_ref, zn, o_ref, z_scr):
    n_slab = zn.shape[1] // LANES
    seg = zn.shape[0] // SUBLANES
    for d in range(n_slab):
        z_scr[d] = zn[:, d * LANES:(d + 1) * LANES]
    for d in range(n_slab):
        cols = slice(d * LANES, (d + 1) * LANES)
        for s in range(SUBLANES):
            rows = slice(s * seg, (s + 1) * seg)
            o_ref[rows, cols] = x_ref[rows, cols] + z_scr[d, pl.ds(s, seg, stride=SUBLANES), :]


def _row_scale(x, dec):
    n = x.shape[1] // LANES
    return jnp.concatenate([x[:, i * LANES:(i + 1) * LANES] * dec for i in range(n)], axis=1)


def _chunk_pos():
    return lax.broadcasted_iota(jnp.int32, (RET_CHUNK, LANES), 0).astype(F32)


def _ret_bwd_kernel(lg_ref, k_ref, v_ref, s0_ref, sb_ref, sfin_ref, s_scr, dec_scr, *, nch):
    c = pl.program_id(1)

    @pl.when(c == 0)
    def _():
        s_scr[...] = s0_ref[...]
        pos = _chunk_pos()
        for h in range(RET_HEADS):
            dec_scr[h] = jnp.exp(lg_ref[1, h] * pos).astype(BF16)

    for ci in reversed(range(nch)):
        rows = slice(ci * RET_CHUNK, (ci + 1) * RET_CHUNK)
        for h in range(RET_HEADS):
            k = k_ref[rows, h * RET_DK:(h + 1) * RET_DK]
            v = v_ref[rows, h * RET_DV:(h + 1) * RET_DV]
            st = s_scr[h]
            sb_ref[ci, h] = st.astype(BF16)
            kd = _row_scale(k, dec_scr[h])
            kv = lax.dot_general(kd, v, (((0,), (0,)), ((), ())), preferred_element_type=F32)
            s_scr[h] = st * jnp.exp(lg_ref[1, h] * RET_CHUNK) + kv

    @pl.when(c == pl.num_programs(1) - 1)
    def _():
        sfin_ref[...] = s_scr[...]


def _ret_fwd_kernel(lg_ref, q_ref, k_ref, v_ref, g_ref, sb_ref, s0_ref, w_ref, x_ref, gp_ref,
                    xo_ref, sfin_ref, s_scr, m_scr, dec_scr, y_scr, *, nch):
    c = pl.program_id(1)

    @pl.when(c == 0)
    def _():
        s_scr[...] = s0_ref[...]
        ri = lax.broadcasted_iota(jnp.int32, (RET_CHUNK, RET_CHUNK), 0)
        ci = lax.broadcasted_iota(jnp.int32, (RET_CHUNK, RET_CHUNK), 1)
        rel = (ri - ci).astype(F32)
        for h in range(RET_HEADS):
            mf = jnp.where(rel >= 0, jnp.exp(lg_ref[0, h] * jnp.maximum(rel, 0.0)), 0.0)
            mb = jnp.where(rel <= 0, jnp.exp(lg_ref[1, h] * jnp.maximum(-rel, 0.0)), 0.0)
            m_scr[h] = mf + mb
        pos = _chunk_pos()
        for h in range(RET_HEADS):
            dec_scr[h, 0] = jnp.exp(lg_ref[0, h] * (pos + 1.0)).astype(BF16)
            dec_scr[h, 1] = jnp.exp(lg_ref[1, h] * (RET_CHUNK - pos)).astype(BF16)
            dec_scr[h, 2] = jnp.exp(lg_ref[0, h] * (RET_CHUNK - 1.0 - pos)).astype(BF16)

    for ci, h in [(ci, h) for ci in range(nch) for h in range(RET_HEADS)]:
        rows = slice(ci * RET_CHUNK, (ci + 1) * RET_CHUNK)
        vcols = slice(h * RET_DV, (h + 1) * RET_DV)
        q = q_ref[rows, h * RET_DK:(h + 1) * RET_DK]
        k = k_ref[rows, h * RET_DK:(h + 1) * RET_DK]
        v = v_ref[rows, vcols]
        st = s_scr[h]
        s = lax.dot_general(q, k, (((1,), (1,)), ((), ())), preferred_element_type=F32)
        o = jnp.dot((s * m_scr[h]).astype(BF16), v, preferred_element_type=F32)
        o = jnp.dot(_row_scale(q, dec_scr[h, 0]), st.astype(BF16), preferred_element_type=F32) + o
        o = jnp.dot(_row_scale(q, dec_scr[h, 1]), sb_ref[ci, h], preferred_element_type=F32) + o
        kv = lax.dot_general(_row_scale(k, dec_scr[h, 2]), v, (((0,), (0,)), ((), ())),
                             preferred_element_type=F32)
        s_scr[h] = st * jnp.exp(lg_ref[0, h] * RET_CHUNK) + kv
        dlt = o - jnp.mean(o, axis=-1, keepdims=True)
        var = jnp.mean(dlt * dlt, axis=-1, keepdims=True)
        gh = g_ref[rows, vcols]
        y_scr[rows, vcols] = ((dlt * lax.rsqrt(var + EPS)).astype(BF16)
                              * (gh * (jnp.tanh(gh) + 1.0)))
        if h == RET_HEADS - 1:
            _residual_out(x_ref, _gated_norm_proj(y_scr[rows, :], w_ref, gp_ref), xo_ref, rows)

    @pl.when(c == pl.num_programs(1) - 1)
    def _():
        sfin_ref[...] = s_scr[...]


def _state_spec():
    return pl.BlockSpec((None, RET_HEADS, RET_DK, RET_DV), lambda i, j: (i, 0, 0, 0))


def _step_chunks(n_chunks):
    return RET_STEP_CHUNKS if n_chunks % RET_STEP_CHUNKS == 0 else 1


def _ret_bwd(p, lg, s0):
    b, t, _ = p.shape
    nc = t // RET_CHUNK
    nch = _step_chunks(nc)
    ns = nc // nch
    rev = lambda blk: (lambda i, j: (i, ns - 1 - j, blk))
    hist = pl.BlockSpec((None, nch, RET_HEADS, RET_DK, RET_DV),
                        lambda i, j: (i, ns - 1 - j, 0, 0, 0))
    return pl.pallas_call(
        functools.partial(_ret_bwd_kernel, nch=nch),
        grid=(b, ns),
        in_specs=[_smem_spec(),
                  pl.BlockSpec((None, nch * RET_CHUNK, RET_QK), rev(1)),
                  pl.BlockSpec((None, nch * RET_CHUNK, RET_V), rev(1)),
                  _state_spec()],
        out_specs=[hist, _state_spec()],
        out_shape=[jax.ShapeDtypeStruct((b, nc, RET_HEADS, RET_DK, RET_DV), BF16),
                   jax.ShapeDtypeStruct(s0.shape, F32)],
        scratch_shapes=[pltpu.VMEM((RET_HEADS, RET_DK, RET_DV), F32),
                        pltpu.VMEM((RET_HEADS, RET_CHUNK, LANES), BF16)],
        compiler_params=_params(2, 48),
        name="ret_bwd",
    )(lg, p, p, s0)


def _ret_fwd(p, sb, lg, s0, w_out, x, gp):
    b, t, _ = p.shape
    d = x.shape[-1]
    nc = t // RET_CHUNK
    nch = _step_chunks(nc)
    rows = nch * RET_CHUNK
    fwd = lambda blk: (lambda i, j: (i, j, blk))
    resid = pl.BlockSpec((None, rows, d), fwd(0))
    return pl.pallas_call(
        functools.partial(_ret_fwd_kernel, nch=nch),
        grid=(b, nc // nch),
        in_specs=[_smem_spec(),
                  pl.BlockSpec((None, rows, RET_QK), fwd(0)),
                  pl.BlockSpec((None, rows, RET_QK), fwd(1)),
                  pl.BlockSpec((None, rows, RET_V), fwd(1)),
                  pl.BlockSpec((None, rows, RET_V), fwd(2)),
                  pl.BlockSpec((None, nch, RET_HEADS, RET_DK, RET_DV),
                               lambda i, j: (i, j, 0, 0, 0)),
                  _state_spec(),
                  pl.BlockSpec((RET_V, d), lambda i, j: (0, 0)),
                  resid,
                  pl.BlockSpec((None, 1, d), lambda i, j: (i, 0, 0))],
        out_specs=[resid, _state_spec()],
        out_shape=[jax.ShapeDtypeStruct((b, t, d), F32),
                   jax.ShapeDtypeStruct(s0.shape, F32)],
        scratch_shapes=[pltpu.VMEM((RET_HEADS, RET_DK, RET_DV), F32),
                        pltpu.VMEM((RET_HEADS, RET_CHUNK, RET_CHUNK), F32),
                        pltpu.VMEM((RET_HEADS, 3, RET_CHUNK, LANES), BF16),
                        pltpu.VMEM((rows, RET_V), BF16)],
        compiler_params=_params(2, 56),
        name="ret_fwd",
    )(lg, p, p, p, p, sb, s0, w_out, x, gp)


def _lru_conv(xr_ref, eprev_ref, enext_ref, has_prev, has_next, cw_ref, cb_ref, ext_scr, xc_ref,
              tt):
    sub = lax.broadcasted_iota(jnp.int32, (SUBLANES, LRU_BW), 0)
    left = CONV_W // 2
    for nb in range(LRU_BLOCKS):
        ext_scr[nb, left * SUBLANES:left * SUBLANES + tt, :] = xr_ref[nb]
        for k in range(left):
            src = xr_ref[nb, tt - (left - k) * SUBLANES:tt - (left - k - 1) * SUBLANES, :]
            before = jnp.where(has_prev, eprev_ref[nb, k:k + 1, :], 0.0)
            ext_scr[nb, k * SUBLANES:(k + 1) * SUBLANES, :] = jnp.where(
                sub == 0, before, pltpu.roll(src, 1, 0))
        for k in range(CONV_W - 1 - left):
            src = xr_ref[nb, k * SUBLANES:(k + 1) * SUBLANES, :]
            after = jnp.where(has_next, enext_ref[nb, left + k:left + k + 1, :], 0.0)
            ext_scr[nb, (left + k) * SUBLANES + tt:(left + k + 1) * SUBLANES + tt, :] = jnp.where(
                sub == SUBLANES - 1, after, pltpu.roll(src, SUBLANES - 1, 0))
        xc = cb_ref[nb:nb + 1, :]
        for j in range(CONV_W):
            xc = xc + ext_scr[nb, j * SUBLANES:j * SUBLANES + tt, :] * cw_ref[j, nb:nb + 1, :]
        xc_ref[nb] = xc


def _lru_coeffs(xc_ref, wg_ref, bg_ref, kh_ref, a_scr, u_scr):
    for nb in range(LRU_BLOCKS):
        xc = xc_ref[nb]
        res = jnp.dot(xc.astype(BF16), wg_ref[nb], preferred_element_type=F32)
        t_r = jnp.tanh(res[:, :LRU_BW] + bg_ref[0, nb:nb + 1, :])
        t_g = jnp.tanh(res[:, LRU_BW:] + bg_ref[1, nb:nb + 1, :])
        kh = kh_ref[nb:nb + 1, :]
        p = jnp.tanh(t_r * kh + kh)
        w = 1.0 / (1.0 + p)
        a_scr[nb] = (1.0 - p) * w
        sqrt_p = jnp.where(p > 0.0, p * lax.rsqrt(p), 0.0)
        u_scr[nb] = sqrt_p * w * (t_g + 1.0) * xc


def _lru_scan(a_scr, u_scr, dst_ref, carry_scr, tt, reverse):
    n = tt // SUBLANES
    shape = (LRU_BLOCKS, SUBLANES, LRU_BW)
    sub = lax.broadcasted_iota(jnp.int32, shape, 1)
    steps = [slice(g * SUBLANES, (g + 1) * SUBLANES) for g in range(n)]
    if reverse:
        steps = steps[::-1]

    a_seg = a_scr[:, steps[0], :]
    h_seg = u_scr[:, steps[0], :]
    for rows in steps[1:]:
        a = a_scr[:, rows, :]
        h_seg = a * h_seg + u_scr[:, rows, :]
        a_seg = a_seg * a
    for d in (1, 2, 4):
        ok = (sub < SUBLANES - d) if reverse else (sub >= d)
        shift = SUBLANES - d if reverse else d
        a_s = jnp.where(ok, pltpu.roll(a_seg, shift, 1), 1.0)
        h_s = jnp.where(ok, pltpu.roll(h_seg, shift, 1), 0.0)
        h_seg = a_seg * h_s + h_seg
        a_seg = a_seg * a_s
    carry = carry_scr[...]
    h_end = a_seg * carry + h_seg
    if reverse:
        h_in = jnp.where(sub == SUBLANES - 1, carry, pltpu.roll(h_end, SUBLANES - 1, 1))
        last = h_end[:, 0:1, :]
    else:
        h_in = jnp.where(sub == 0, carry, pltpu.roll(h_end, 1, 1))
        last = h_end[:, SUBLANES - 1:SUBLANES, :]
    carry_scr[...] = jnp.broadcast_to(last, shape)

    h = h_in
    for rows in steps:
        h = a_scr[:, rows, :] * h + u_scr[:, rows, :]
        dst_ref[:, rows, :] = h


def _lru_bwd_kernel(xr_ref, eprev_ref, enext_ref, cw_ref, cb_ref, wg_ref, bg_ref, kh_ref, h0_ref,
                    hb_ref, xc_ref, hfin_ref, ext_scr, a_scr, u_scr, carry_scr, *, tt):
    i = pl.program_id(1)

    @pl.when(i == 0)
    def _():
        carry_scr[...] = h0_ref[...]

    _lru_conv(xr_ref, eprev_ref, enext_ref, i < pl.num_programs(1) - 1, i > 0,
              cw_ref, cb_ref, ext_scr, xc_ref, tt)
    _lru_coeffs(xc_ref, wg_ref, bg_ref, kh_ref, a_scr, u_scr)
    _lru_scan(a_scr, u_scr, hb_ref, carry_scr, tt, True)

    @pl.when(i == pl.num_programs(1) - 1)
    def _():
        hfin_ref[...] = carry_scr[...]


def _lru_fwd_kernel(xc_ref, wg_ref, bg_ref, kh_ref, h0_ref, hb_ref, g_ref, w_ref, x_ref, gp_ref,
                    xo_ref, hfin_ref, a_scr, u_scr, carry_scr, y_scr, z_scr, *, tt):
    i = pl.program_id(1)

    @pl.when(i == 0)
    def _():
        carry_scr[...] = h0_ref[...]

    _lru_coeffs(xc_ref, wg_ref, bg_ref, kh_ref, a_scr, u_scr)
    _lru_scan(a_scr, u_scr, u_scr, carry_scr, tt, False)
    for nb in range(LRU_BLOCKS):
        gh = g_ref[nb]
        y = (u_scr[nb] + hb_ref[nb]).astype(BF16) * (gh * (jnp.tanh(gh) + 1.0))
        y_scr[:, nb * LRU_BW:(nb + 1) * LRU_BW] = y
    _residual_out_interleaved(x_ref, _gated_norm_proj(y_scr[...], w_ref, gp_ref), xo_ref, z_scr)

    @pl.when(i == pl.num_programs(1) - 1)
    def _():
        hfin_ref[...] = carry_scr[...]


def _lru_specs(b, t, tt, reverse):
    nblk = t // tt
    tidx = (lambda j: nblk - 1 - j) if reverse else (lambda j: j)
    slab = pl.BlockSpec((None, LRU_BLOCKS, tt, LRU_BW), lambda i, j: (i, 0, tidx(j), 0))
    state = pl.BlockSpec((None, LRU_BLOCKS, SUBLANES, LRU_BW), lambda i, j: (i, 0, 0, 0))
    edge_block = (None, None, LRU_BLOCKS, SUBLANES, LRU_BW)
    edges = [pl.BlockSpec(edge_block, lambda i, j: (i, jnp.maximum(tidx(j) - 1, 0), 0, 0, 0)),
             pl.BlockSpec(edge_block, lambda i, j: (i, jnp.minimum(tidx(j) + 1, nblk - 1), 0, 0, 0))]
    gate = [pl.BlockSpec((LRU_BLOCKS, LRU_BW, 2 * LRU_BW), lambda i, j: (0, 0, 0)),
            pl.BlockSpec((2, LRU_BLOCKS, LRU_BW), lambda i, j: (0, 0, 0)),
            pl.BlockSpec((LRU_BLOCKS, LRU_BW), lambda i, j: (0, 0))]
    return nblk, slab, state, edges, gate


def _lru_bwd(xr, edges, cw, cb, wg, bg, kh, h0, *, tt):
    b, _, t, _ = xr.shape
    nblk, slab, state, edge_specs, gate = _lru_specs(b, t, tt, True)
    slab_shape = jax.ShapeDtypeStruct(xr.shape, F32)
    tile = (LRU_BLOCKS, tt, LRU_BW)
    return pl.pallas_call(
        functools.partial(_lru_bwd_kernel, tt=tt),
        grid=(b, nblk),
        in_specs=[slab] + edge_specs + [
                  pl.BlockSpec((CONV_W, LRU_BLOCKS, LRU_BW), lambda i, j: (0, 0, 0)),
                  pl.BlockSpec((LRU_BLOCKS, LRU_BW), lambda i, j: (0, 0))] + gate + [state],
        out_specs=[slab, slab, state],
        out_shape=[slab_shape, slab_shape, jax.ShapeDtypeStruct(h0.shape, F32)],
        scratch_shapes=[pltpu.VMEM((LRU_BLOCKS, tt + (CONV_W - 1) * SUBLANES, LRU_BW), F32),
                        pltpu.VMEM(tile, F32), pltpu.VMEM(tile, F32),
                        pltpu.VMEM((LRU_BLOCKS, SUBLANES, LRU_BW), F32)],
        compiler_params=_params(2, 48),
        name="lru_bwd",
    )(xr, edges, edges, cw, cb, wg, bg, kh, h0)


def _lru_fwd(xc, wg, bg, kh, h0, hb, g, w_out, x, gp, *, tt):
    b, _, t, _ = xc.shape
    d = x.shape[-1]
    nblk, slab, state, _, gate = _lru_specs(b, t, tt, False)
    tile = (LRU_BLOCKS, tt, LRU_BW)
    resid = pl.BlockSpec((None, tt, d), lambda i, j: (i, j, 0))
    return pl.pallas_call(
        functools.partial(_lru_fwd_kernel, tt=tt),
        grid=(b, nblk),
        in_specs=[slab] + gate + [state, slab, slab,
                                  pl.BlockSpec((LRU_WIDTH, d), lambda i, j: (0, 0)),
                                  resid,
                                  pl.BlockSpec((None, 1, d), lambda i, j: (i, 0, 0))],
        out_specs=[resid, state],
        out_shape=[jax.ShapeDtypeStruct((b, t, d), F32),
                   jax.ShapeDtypeStruct(h0.shape, F32)],
        scratch_shapes=[pltpu.VMEM(tile, F32), pltpu.VMEM(tile, F32),
                        pltpu.VMEM((LRU_BLOCKS, SUBLANES, LRU_BW), F32),
                        pltpu.VMEM((tt, LRU_WIDTH), BF16),
                        pltpu.VMEM((d // LANES, tt, LANES), F32)],
        compiler_params=_params(2, 56),
        name="lru_fwd",
    )(xc, wg, bg, kh, h0, hb, g, w_out, x, gp)


def _rope_tables(n_tok):
    n_rows = n_tok // GRID_W
    row = jnp.repeat(jnp.arange(n_rows, dtype=F32), GRID_W)
    col = jnp.tile(jnp.arange(GRID_W, dtype=F32), n_rows)
    n_freq = RET_DK // 4
    inv = ROPE_BASE ** (-jnp.arange(n_freq, dtype=F32) / n_freq)
    ang = jnp.concatenate([row[:, None] * inv, col[:, None] * inv], axis=-1)
    return jnp.cos(ang), jnp.sin(ang)


def _deinterleave_heads(w):
    d = w.shape[0]
    return w.reshape(d, RET_HEADS, HALF_DK, 2).transpose(0, 1, 3, 2).reshape(d, RET_QK)


def _retention_layer(x, s_ctx, mods, cos, sin, w_in, log_decay, gn_g, w_out):
    (mul_l, add_l, gp_l), (mul_c, add_c, gp_c) = mods
    b = x.shape[0]
    lc = s_ctx.shape[1]
    w = jnp.concatenate([_deinterleave_heads(w_in[:, :RET_QK]),
                         _deinterleave_heads(w_in[:, RET_QK:2 * RET_QK]),
                         w_in[:, 2 * RET_QK:2 * RET_QK + RET_V],
                         0.5 * w_in[:, 2 * RET_QK + RET_V:]], axis=1).astype(BF16)
    lg = -jnp.abs(log_decay.astype(F32))
    w_o = (gn_g.astype(F32)[:, None] * w_out).astype(BF16)

    p_l = _ret_inproj(x, mul_l, add_l, w, cos, sin, rope=True, tm=ROW_TILE)
    p_c = _ret_inproj(s_ctx, mul_c, add_c, w, cos[:lc], sin[:lc], rope=False, tm=lc)
    zero = jnp.zeros((b, RET_HEADS, RET_DK, RET_DV), F32)
    sb_c, sb_fin = _ret_bwd(p_c, lg, zero)
    sb_l, _ = _ret_bwd(p_l, lg, sb_fin)
    s_ctx, sf_c = _ret_fwd(p_c, sb_c, lg, zero, w_o, s_ctx, gp_c)
    x, _ = _ret_fwd(p_l, sb_l, lg, sf_c, w_o, x, gp_l)
    return x, s_ctx


def _lru_layer(x, s_ctx, mods, w_in, conv_w, conv_b, w_a, b_a, w_x, b_x, lam, w_out):
    (mul_l, add_l, gp_l), (mul_c, add_c, gp_c) = mods
    b = x.shape[0]
    lc = s_ctx.shape[1]
    w = jnp.concatenate([w_in[:, :LRU_WIDTH], 0.5 * w_in[:, LRU_WIDTH:]], axis=1).astype(BF16)
    w_o = w_out.astype(BF16)
    cw = conv_w.astype(F32).reshape(CONV_W, LRU_BLOCKS, LRU_BW)
    cb = conv_b.astype(F32).reshape(LRU_BLOCKS, LRU_BW)
    kh = (0.25 * LRU_C) * jax.nn.softplus(-lam.astype(F32))

    def gate_params(d):
        wg = (0.5 * jnp.concatenate([w_a[d], w_x[d]], axis=-1)).astype(BF16)
        bg = (0.5 * jnp.stack([b_a[d], b_x[d]])).astype(F32).reshape(2, LRU_BLOCKS, LRU_BW)
        return wg, bg, kh[d].reshape(LRU_BLOCKS, LRU_BW)

    xr_l, g_l, edges_l = _lru_inproj(x, mul_l, add_l, w, tm=ROW_TILE)
    xr_c, g_c, edges_c = _lru_inproj(s_ctx, mul_c, add_c, w, tm=lc)
    zero = jnp.zeros((b, LRU_BLOCKS, SUBLANES, LRU_BW), F32)
    hb_c, xc_c, hb_fin = _lru_bwd(xr_c, edges_c, cw, cb, *gate_params(1), zero, tt=lc)
    hb_l, xc_l, _ = _lru_bwd(xr_l, edges_l, cw, cb, *gate_params(1), hb_fin, tt=ROW_TILE)
    s_ctx, hf_fin = _lru_fwd(xc_c, *gate_params(0), zero, hb_c, g_c, w_o, s_ctx, gp_c, tt=lc)
    x, _ = _lru_fwd(xc_l, *gate_params(0), hf_fin, hb_l, g_l, w_o, x, gp_l, tt=ROW_TILE)
    return x, s_ctx


def kernel(x, c, ctx, c_ctx, mod_w, mod_b, norm_pre, norm_post, ret_w_in, ret_log_decay, ret_gn,
           ret_w_out, lru_w_in, lru_conv_w, lru_conv_b, lru_w_a, lru_b_a, lru_w_x, lru_b_x,
           lru_lambda, lru_w_out):
    b, n_tok, d = x.shape
    depth = mod_w.shape[0]
    assert n_tok % ROW_TILE == 0 and n_tok % RET_CHUNK == 0
    assert ctx.shape[1] % RET_CHUNK == 0 and ctx.shape[1] <= ROW_TILE
    cos, sin = _rope_tables(n_tok)

    rows = -(-(b + 1) // SUBLANES) * SUBLANES
    cc = jnp.concatenate([c, c_ctx[None, :], jnp.zeros((rows - b - 1, d), F32)], axis=0)
    mod = _modulation(cc, mod_w, mod_b)

    s_ctx = ctx
    for i in range(depth):
        shift, scale, gate = jnp.split(mod[i], 3, axis=-1)
        mods = []
        for sel in (slice(0, b), slice(b, b + 1)):
            mul = jnp.broadcast_to(norm_pre[i] * (1.0 + scale[sel]), (b, d))[:, None, :]
            add = jnp.broadcast_to(shift[sel], (b, d))[:, None, :]
            gp = jnp.broadcast_to(gate[sel] * norm_post[i], (b, d))[:, None, :]
            mods.append((mul, add, gp))
        j = i // 2
        if i % 2 == 0:
            x, s_ctx = _retention_layer(x, s_ctx, mods, cos, sin, ret_w_in[j], ret_log_decay[j],
                                        ret_gn[j], ret_w_out[j])
        else:
            x, s_ctx = _lru_layer(x, s_ctx, mods, lru_w_in[j], lru_conv_w[j], lru_conv_b[j],
                                  lru_w_a[j], lru_b_a[j], lru_w_x[j], lru_b_x[j],
                                  lru_lambda[j], lru_w_out[j])
    return x
```

```python
import functools

import jax
import jax.numpy as jnp
from jax import lax
from jax.experimental import pallas as pl
from jax.experimental.pallas import tpu as pltpu

F32 = jnp.float32
BF16 = jnp.bfloat16

EPS = 1e-6
GRID_W = 64
ROPE_BASE = 10000.0

RET_HEADS = 4
RET_DK = 256
RET_DV = 512
RET_QK = RET_HEADS * RET_DK
RET_V = RET_HEADS * RET_DV
RET_IN = 2 * RET_QK + 2 * RET_V
HALF_DK = RET_DK // 2

LRU_BLOCKS = 10
LRU_BW = 128
LRU_WIDTH = LRU_BLOCKS * LRU_BW
CONV_W = 4
LRU_C = 8.0

LANES = 128
SUBLANES = 8
MIB = 1024 * 1024

RET_CHUNK = 256
RET_STEP_CHUNKS = 2
ROW_TILE = 512


def _params(n_axes, vmem_mib):
    return pltpu.CompilerParams(
        dimension_semantics=("arbitrary",) * n_axes,
        vmem_limit_bytes=vmem_mib * MIB)


def _smem_spec():
    return pl.BlockSpec(memory_space=pltpu.SMEM)


def _mod_kernel(cc_ref, w_ref, b_ref, o_ref):
    cc = cc_ref[...]
    act = cc * jax.nn.sigmoid(cc)
    o_ref[...] = jnp.dot(act.astype(BF16), w_ref[...].astype(BF16),
                         preferred_element_type=F32) + b_ref[...]


def _modulation(cc, mod_w, mod_b):
    depth, d, d3 = mod_w.shape
    rows = cc.shape[0]
    return pl.pallas_call(
        _mod_kernel,
        grid=(depth, d3 // d),
        in_specs=[pl.BlockSpec((rows, d), lambda i, j: (0, 0)),
                  pl.BlockSpec((None, d, d), lambda i, j: (i, 0, j)),
                  pl.BlockSpec((None, 1, d), lambda i, j: (i, 0, j))],
        out_specs=pl.BlockSpec((None, rows, d), lambda i, j: (i, 0, j)),
        out_shape=jax.ShapeDtypeStruct((depth, rows, d3), F32),
        compiler_params=_params(2, 32),
        name="modulation",
    )(cc, mod_w, mod_b.reshape(depth, 1, d3))


def _norm_modulate(x_ref, mul_ref, add_ref, h_scr):
    x = x_ref[...]
    ms = jnp.mean(x * x, axis=-1, keepdims=True)
    h = x * lax.rsqrt(ms + EPS) * mul_ref[...] + add_ref[...]
    h_scr[...] = h.astype(BF16)


def _ret_inproj_kernel(x_ref, mul_ref, add_ref, w_ref, cos_ref, sin_ref, o_ref, h_scr, *, rope):
    _norm_modulate(x_ref, mul_ref, add_ref, h_scr)
    hb = h_scr[...]
    scale = RET_DK ** -0.5
    for j in range(2 * RET_HEADS):
        cols = slice(j * RET_DK, (j + 1) * RET_DK)
        acc = jnp.dot(hb, w_ref[:, cols], preferred_element_type=F32)
        if j >= RET_HEADS:
            acc = acc * scale
        if rope:
            e, o = acc[:, :HALF_DK], acc[:, HALF_DK:]
            c, s = cos_ref[...], sin_ref[...]
            acc = jnp.concatenate([e * c - o * s, e * s + o * c], axis=1)
        o_ref[:, cols] = acc.astype(BF16)
    for j in range(2 * RET_QK // RET_DV, RET_IN // RET_DV):
        cols = slice(j * RET_DV, (j + 1) * RET_DV)
        o_ref[:, cols] = jnp.dot(hb, w_ref[:, cols], preferred_element_type=F32).astype(BF16)


def _ret_inproj(x, mul, add, w, cos, sin, *, rope, tm):
    b, t, d = x.shape
    kern = functools.partial(_ret_inproj_kernel, rope=rope)
    return pl.pallas_call(
        kern,
        grid=(b, t // tm),
        in_specs=[pl.BlockSpec((None, tm, d), lambda i, j: (i, j, 0)),
                  pl.BlockSpec((None, 1, d), lambda i, j: (i, 0, 0)),
                  pl.BlockSpec((None, 1, d), lambda i, j: (i, 0, 0)),
                  pl.BlockSpec((d, RET_IN), lambda i, j: (0, 0)),
                  pl.BlockSpec((tm, HALF_DK), lambda i, j: (j, 0)),
                  pl.BlockSpec((tm, HALF_DK), lambda i, j: (j, 0))],
        out_specs=pl.BlockSpec((None, tm, RET_IN), lambda i, j: (i, j, 0)),
        out_shape=jax.ShapeDtypeStruct((b, t, RET_IN), BF16),
        scratch_shapes=[pltpu.VMEM((tm, d), BF16)],
        compiler_params=_params(2, 56),
        name="ret_inproj_rope" if rope else "ret_inproj",
    )(x, mul, add, w, cos, sin)


def _store_interleaved(dst_ref, blk, val):
    seg = val.shape[0] // SUBLANES
    for s in range(SUBLANES):
        dst_ref[blk, pl.ds(s, seg, stride=SUBLANES), :] = val[s * seg:(s + 1) * seg, :]


def _lru_inproj_kernel(x_ref, mul_ref, add_ref, w_ref, xr_ref, g_ref, edge_ref, h_scr, hs_scr):
    x = x_ref[...]
    ms = jnp.mean(x * x, axis=-1, keepdims=True)
    h = x * lax.rsqrt(ms + EPS) * mul_ref[...] + add_ref[...]
    tm, d = h.shape
    for s in range(d // LANES):
        _store_interleaved(hs_scr, s, h[:, s * LANES:(s + 1) * LANES])
    for s in range(d // LANES):
        h_scr[:, s * LANES:(s + 1) * LANES] = hs_scr[s].astype(BF16)
    hb = h_scr[...]
    last = tm - 1
    for j in range(LRU_WIDTH // (2 * LANES)):
        cols = slice(j * 2 * LANES, (j + 1) * 2 * LANES)
        acc = jnp.dot(hb, w_ref[:, cols], preferred_element_type=F32)
        gcols = slice(LRU_WIDTH + j * 2 * LANES, LRU_WIDTH + (j + 1) * 2 * LANES)
        gacc = jnp.dot(hb, w_ref[:, gcols], preferred_element_type=F32).astype(BF16)
        for half in range(2):
            blk = 2 * j + half
            lanes = slice(half * LANES, (half + 1) * LANES)
            xr_ref[blk] = acc[:, lanes]
            g_ref[blk] = gacc[:, lanes]
            edge_ref[blk, 0:1, :] = acc[last - SUBLANES:last - SUBLANES + 1, lanes]
            edge_ref[blk, 1:2, :] = acc[last:last + 1, lanes]
            edge_ref[blk, 2:3, :] = acc[0:1, lanes]
            edge_ref[blk, 3:SUBLANES, :] = jnp.zeros((SUBLANES - 3, LANES), F32)


def _lru_inproj(x, mul, add, w, *, tm):
    b, t, d = x.shape
    slab = pl.BlockSpec((None, LRU_BLOCKS, tm, LRU_BW), lambda i, j: (i, 0, j, 0))
    return pl.pallas_call(
        _lru_inproj_kernel,
        grid=(b, t // tm),
        in_specs=[pl.BlockSpec((None, tm, d), lambda i, j: (i, j, 0)),
                  pl.BlockSpec((None, 1, d), lambda i, j: (i, 0, 0)),
                  pl.BlockSpec((None, 1, d), lambda i, j: (i, 0, 0)),
                  pl.BlockSpec((d, 2 * LRU_WIDTH), lambda i, j: (0, 0))],
        out_specs=[slab, slab,
                   pl.BlockSpec((None, None, LRU_BLOCKS, SUBLANES, LRU_BW),
                                lambda i, j: (i, j, 0, 0, 0))],
        out_shape=[jax.ShapeDtypeStruct((b, LRU_BLOCKS, t, LRU_BW), F32),
                   jax.ShapeDtypeStruct((b, LRU_BLOCKS, t, LRU_BW), BF16),
                   jax.ShapeDtypeStruct((b, t // tm, LRU_BLOCKS, SUBLANES, LRU_BW), F32)],
        scratch_shapes=[pltpu.VMEM((tm, d), BF16),
                        pltpu.VMEM((d // LANES, tm, LANES), F32)],
        compiler_params=_params(2, 48),
        name="lru_inproj",
    )(x, mul, add, w)


def _gated_norm_proj(y, w_ref, gp_ref):
    z = jnp.dot(y, w_ref[...], preferred_element_type=F32)
    ms = jnp.mean(z * z, axis=-1, keepdims=True)
    return z * lax.rsqrt(ms + EPS) * gp_ref[...]


def _residual_out(x_ref, zn, o_ref, rows):
    o_ref[rows, :] = x_ref[rows, :] + zn


def _residual_out_interleaved(x_ref, zn, o_ref, z_scr):
    n_slab = zn.shape[1] // LANES
    seg = zn.shape[0] // SUBLANES
    for d in range(n_slab):
        z_scr[d] = zn[:, d * LANES:(d + 1) * LANES]
    for d in range(n_slab):
        cols = slice(d * LANES, (d + 1) * LANES)
        for s in range(SUBLANES):
            rows = slice(s * seg, (s + 1) * seg)
            o_ref[rows, cols] = x_ref[rows, cols] + z_scr[d, pl.ds(s, seg, stride=SUBLANES), :]


def _row_scale(x, dec):
    n = x.shape[1] // LANES
    return jnp.concatenate([x[:, i * LANES:(i + 1) * LANES] * dec for i in range(n)], axis=1)


def _chunk_pos():
    return lax.broadcasted_iota(jnp.int32, (RET_CHUNK, LANES), 0).astype(F32)


def _ret_bwd_kernel(lg_ref, k_ref, v_ref, s0_ref, sb_ref, sfin_ref, s_scr, dec_scr, *, nch):
    c = pl.program_id(1)

    @pl.when(c == 0)
    def _():
        s_scr[...] = s0_ref[...]
        pos = _chunk_pos()
        for h in range(RET_HEADS):
            dec_scr[h] = jnp.exp(lg_ref[1, h] * pos).astype(BF16)

    for ci in reversed(range(nch)):
        rows = slice(ci * RET_CHUNK, (ci + 1) * RET_CHUNK)
        for h in range(RET_HEADS):
            k = k_ref[rows, h * RET_DK:(h + 1) * RET_DK]
            v = v_ref[rows, h * RET_DV:(h + 1) * RET_DV]
            st = s_scr[h]
            sb_ref[ci, h] = st.astype(BF16)
            kd = _row_scale(k, dec_scr[h])
            kv = lax.dot_general(kd, v, (((0,), (0,)), ((), ())), preferred_element_type=F32)
            s_scr[h] = st * jnp.exp(lg_ref[1, h] * RET_CHUNK) + kv

    @pl.when(c == pl.num_programs(1) - 1)
    def _():
        sfin_ref[...] = s_scr[...]


def _ret_fwd_kernel(lg_ref, q_ref, k_ref, v_ref, g_ref, sb_ref, s0_ref, w_ref, x_ref, gp_ref,
                    xo_ref, sfin_ref, s_scr, m_scr, dec_scr, y_scr, *, nch):
    c = pl.program_id(1)

    @pl.when(c == 0)
    def _():
        s_scr[...] = s0_ref[...]
        ri = lax.broadcasted_iota(jnp.int32, (RET_CHUNK, RET_CHUNK), 0)
        ci = lax.broadcasted_iota(jnp.int32, (RET_CHUNK, RET_CHUNK), 1)
        rel = (ri - ci).astype(F32)
        for h in range(RET_HEADS):
            mf = jnp.where(rel >= 0, jnp.exp(lg_ref[0, h] * jnp.maximum(rel, 0.0)), 0.0)
            mb = jnp.where(rel <= 0, jnp.exp(lg_ref[1, h] * jnp.maximum(-rel, 0.0)), 0.0)
            m_scr[h] = mf + mb
        pos = _chunk_pos()
        for h in range(RET_HEADS):
            dec_scr[h, 0] = jnp.exp(lg_ref[0, h] * (pos + 1.0)).astype(BF16)
            dec_scr[h, 1] = jnp.exp(lg_ref[1, h] * (RET_CHUNK - pos)).astype(BF16)
            dec_scr[h, 2] = jnp.exp(lg_ref[0, h] * (RET_CHUNK - 1.0 - pos)).astype(BF16)

    for ci, h in [(ci, h) for ci in range(nch) for h in range(RET_HEADS)]:
        rows = slice(ci * RET_CHUNK, (ci + 1) * RET_CHUNK)
        vcols = slice(h * RET_DV, (h + 1) * RET_DV)
        q = q_ref[rows, h * RET_DK:(h + 1) * RET_DK]
        k = k_ref[rows, h * RET_DK:(h + 1) * RET_DK]
        v = v_ref[rows, vcols]
        st = s_scr[h]
        s = lax.dot_general(q, k, (((1,), (1,)), ((), ())), preferred_element_type=F32)
        o = jnp.dot((s * m_scr[h]).astype(BF16), v, preferred_element_type=F32)
        o = jnp.dot(_row_scale(q, dec_scr[h, 0]), st.astype(BF16), preferred_element_type=F32) + o
        o = jnp.dot(_row_scale(q, dec_scr[h, 1]), sb_ref[ci, h], preferred_element_type=F32) + o
        kv = lax.dot_general(_row_scale(k, dec_scr[h, 2]), v, (((0,), (0,)), ((), ())),
                             preferred_element_type=F32)
        s_scr[h] = st * jnp.exp(lg_ref[0, h] * RET_CHUNK) + kv
        dlt = o - jnp.mean(o, axis=-1, keepdims=True)
        var = jnp.mean(dlt * dlt, axis=-1, keepdims=True)
        gh = g_ref[rows, vcols]
        y_scr[rows, vcols] = ((dlt * lax.rsqrt(var + EPS)).astype(BF16)
                              * (gh * (jnp.tanh(gh) + 1.0)))
        if h == RET_HEADS - 1:
            _residual_out(x_ref, _gated_norm_proj(y_scr[rows, :], w_ref, gp_ref), xo_ref, rows)

    @pl.when(c == pl.num_programs(1) - 1)
    def _():
        sfin_ref[...] = s_scr[...]


def _state_spec():
    return pl.BlockSpec((None, RET_HEADS, RET_DK, RET_DV), lambda i, j: (i, 0, 0, 0))


def _step_chunks(n_chunks):
    return RET_STEP_CHUNKS if n_chunks % RET_STEP_CHUNKS == 0 else 1


def _ret_bwd(p, lg, s0):
    b, t, _ = p.shape
    nc = t // RET_CHUNK
    nch = _step_chunks(nc)
    ns = nc // nch
    rev = lambda blk: (lambda i, j: (i, ns - 1 - j, blk))
    hist = pl.BlockSpec((None, nch, RET_HEADS, RET_DK, RET_DV),
                        lambda i, j: (i, ns - 1 - j, 0, 0, 0))
    return pl.pallas_call(
        functools.partial(_ret_bwd_kernel, nch=nch),
        grid=(b, ns),
        in_specs=[_smem_spec(),
                  pl.BlockSpec((None, nch * RET_CHUNK, RET_QK), rev(1)),
                  pl.BlockSpec((None, nch * RET_CHUNK, RET_V), rev(1)),
                  _state_spec()],
        out_specs=[hist, _state_spec()],
        out_shape=[jax.ShapeDtypeStruct((b, nc, RET_HEADS, RET_DK, RET_DV), BF16),
                   jax.ShapeDtypeStruct(s0.shape, F32)],
        scratch_shapes=[pltpu.VMEM((RET_HEADS, RET_DK, RET_DV), F32),
                        pltpu.VMEM((RET_HEADS, RET_CHUNK, LANES), BF16)],
        compiler_params=_params(2, 48),
        name="ret_bwd",
    )(lg, p, p, s0)


def _ret_fwd(p, sb, lg, s0, w_out, x, gp):
    b, t, _ = p.shape
    d = x.shape[-1]
    nc = t // RET_CHUNK
    nch = _step_chunks(nc)
    rows = nch * RET_CHUNK
    fwd = lambda blk: (lambda i, j: (i, j, blk))
    resid = pl.BlockSpec((None, rows, d), fwd(0))
    return pl.pallas_call(
        functools.partial(_ret_fwd_kernel, nch=nch),
        grid=(b, nc // nch),
        in_specs=[_smem_spec(),
                  pl.BlockSpec((None, rows, RET_QK), fwd(0)),
                  pl.BlockSpec((None, rows, RET_QK), fwd(1)),
                  pl.BlockSpec((None, rows, RET_V), fwd(1)),
                  pl.BlockSpec((None, rows, RET_V), fwd(2)),
                  pl.BlockSpec((None, nch, RET_HEADS, RET_DK, RET_DV),
                               lambda i, j: (i, j, 0, 0, 0)),
                  _state_spec(),
                  pl.BlockSpec((RET_V, d), lambda i, j: (0, 0)),
                  resid,
                  pl.BlockSpec((None, 1, d), lambda i, j: (i, 0, 0))],
        out_specs=[resid, _state_spec()],
        out_shape=[jax.ShapeDtypeStruct((b, t, d), F32),
                   jax.ShapeDtypeStruct(s0.shape, F32)],
        scratch_shapes=[pltpu.VMEM((RET_HEADS, RET_DK, RET_DV), F32),
                        pltpu.VMEM((RET_HEADS, RET_CHUNK, RET_CHUNK), F32),
                        pltpu.VMEM((RET_HEADS, 3, RET_CHUNK, LANES), BF16),
                        pltpu.VMEM((rows, RET_V), BF16)],
        compiler_params=_params(2, 56),
        name="ret_fwd",
    )(lg, p, p, p, p, sb, s0, w_out, x, gp)


def _lru_conv(xr_ref, eprev_ref, enext_ref, has_prev, has_next, cw_ref, cb_ref, ext_scr, xc_ref,
              tt):
    sub = lax.broadcasted_iota(jnp.int32, (SUBLANES, LRU_BW), 0)
    left = CONV_W // 2
    for nb in range(LRU_BLOCKS):
        ext_scr[nb, left * SUBLANES:left * SUBLANES + tt, :] = xr_ref[nb]
        for k in range(left):
            src = xr_ref[nb, tt - (left - k) * SUBLANES:tt - (left - k - 1) * SUBLANES, :]
            before = jnp.where(has_prev, eprev_ref[nb, k:k + 1, :], 0.0)
            ext_scr[nb, k * SUBLANES:(k + 1) * SUBLANES, :] = jnp.where(
                sub == 0, before, pltpu.roll(src, 1, 0))
        for k in range(CONV_W - 1 - left):
            src = xr_ref[nb, k * SUBLANES:(k + 1) * SUBLANES, :]
            after = jnp.where(has_next, enext_ref[nb, left + k:left + k + 1, :], 0.0)
            ext_scr[nb, (left + k) * SUBLANES + tt:(left + k + 1) * SUBLANES + tt, :] = jnp.where(
                sub == SUBLANES - 1, after, pltpu.roll(src, SUBLANES - 1, 0))
        xc = cb_ref[nb:nb + 1, :]
        for j in range(CONV_W):
            xc = xc + ext_scr[nb, j * SUBLANES:j * SUBLANES + tt, :] * cw_ref[j, nb:nb + 1, :]
        xc_ref[nb] = xc


def _lru_coeffs(xc_ref, wg_ref, bg_ref, kh_ref, a_scr, u_scr):
    for nb in range(LRU_BLOCKS):
        xc = xc_ref[nb]
        res = jnp.dot(xc.astype(BF16), wg_ref[nb], preferred_element_type=F32)
        t_r = jnp.tanh(res[:, :LRU_BW] + bg_ref[0, nb:nb + 1, :])
        t_g = jnp.tanh(res[:, LRU_BW:] + bg_ref[1, nb:nb + 1, :])
        kh = kh_ref[nb:nb + 1, :]
        p = jnp.tanh(t_r * kh + kh)
        w = 1.0 / (1.0 + p)
        a_scr[nb] = (1.0 - p) * w
        sqrt_p = jnp.where(p > 0.0, p * lax.rsqrt(p), 0.0)
        u_scr[nb] = sqrt_p * w * (t_g + 1.0) * xc


def _lru_scan(a_scr, u_scr, dst_ref, carry_scr, tt, reverse):
    n = tt // SUBLANES
    shape = (LRU_BLOCKS, SUBLANES, LRU_BW)
    sub = lax.broadcasted_iota(jnp.int32, shape, 1)
    steps = [slice(g * SUBLANES, (g + 1) * SUBLANES) for g in range(n)]
    if reverse:
        steps = steps[::-1]

    a_seg = a_scr[:, steps[0], :]
    h_seg = u_scr[:, steps[0], :]
    for rows in steps[1:]:
        a = a_scr[:, rows, :]
        h_seg = a * h_seg + u_scr[:, rows, :]
        a_seg = a_seg * a
    for d in (1, 2, 4):
        ok = (sub < SUBLANES - d) if reverse else (sub >= d)
        shift = SUBLANES - d if reverse else d
        a_s = jnp.where(ok, pltpu.roll(a_seg, shift, 1), 1.0)
        h_s = jnp.where(ok, pltpu.roll(h_seg, shift, 1), 0.0)
        h_seg = a_seg * h_s + h_seg
        a_seg = a_seg * a_s
    carry = carry_scr[...]
    h_end = a_seg * carry + h_seg
    if reverse:
        h_in = jnp.where(sub == SUBLANES - 1, carry, pltpu.roll(h_end, SUBLANES - 1, 1))
        last = h_end[:, 0:1, :]
    else:
        h_in = jnp.where(sub == 0, carry, pltpu.roll(h_end, 1, 1))
        last = h_end[:, SUBLANES - 1:SUBLANES, :]
    carry_scr[...] = jnp.broadcast_to(last, shape)

    h = h_in
    for rows in steps:
        h = a_scr[:, rows, :] * h + u_scr[:, rows, :]
        dst_ref[:, rows, :] = h


def _lru_bwd_kernel(xr_ref, eprev_ref, enext_ref, cw_ref, cb_ref, wg_ref, bg_ref, kh_ref, h0_ref,
                    hb_ref, xc_ref, hfin_ref, ext_scr, a_scr, u_scr, carry_scr, *, tt):
    i = pl.program_id(1)

    @pl.when(i == 0)
    def _():
        carry_scr[...] = h0_ref[...]

    _lru_conv(xr_ref, eprev_ref, enext_ref, i < pl.num_programs(1) - 1, i > 0,
              cw_ref, cb_ref, ext_scr, xc_ref, tt)
    _lru_coeffs(xc_ref, wg_ref, bg_ref, kh_ref, a_scr, u_scr)
    _lru_scan(a_scr, u_scr, hb_ref, carry_scr, tt, True)

    @pl.when(i == pl.num_programs(1) - 1)
    def _():
        hfin_ref[...] = carry_scr[...]


def _lru_fwd_kernel(xc_ref, wg_ref, bg_ref, kh_ref, h0_ref, hb_ref, g_ref, w_ref, x_ref, gp_ref,
                    xo_ref, hfin_ref, a_scr, u_scr, carry_scr, y_scr, z_scr, *, tt):
    i = pl.program_id(1)

    @pl.when(i == 0)
    def _():
        carry_scr[...] = h0_ref[...]

    _lru_coeffs(xc_ref, wg_ref, bg_ref, kh_ref, a_scr, u_scr)
    _lru_scan(a_scr, u_scr, u_scr, carry_scr, tt, False)
    for nb in range(LRU_BLOCKS):
        gh = g_ref[nb]
        y = (u_scr[nb] + hb_ref[nb]).astype(BF16) * (gh * (jnp.tanh(gh) + 1.0))
        y_scr[:, nb * LRU_BW:(nb + 1) * LRU_BW] = y
    _residual_out_interleaved(x_ref, _gated_norm_proj(y_scr[...], w_ref, gp_ref), xo_ref, z_scr)

    @pl.when(i == pl.num_programs(1) - 1)
    def _():
        hfin_ref[...] = carry_scr[...]


def _lru_specs(b, t, tt, reverse):
    nblk = t // tt
    tidx = (lambda j: nblk - 1 - j) if reverse else (lambda j: j)
    slab = pl.BlockSpec((None, LRU_BLOCKS, tt, LRU_BW), lambda i, j: (i, 0, tidx(j), 0))
    state = pl.BlockSpec((None, LRU_BLOCKS, SUBLANES, LRU_BW), lambda i, j: (i, 0, 0, 0))
    edge_block = (None, None, LRU_BLOCKS, SUBLANES, LRU_BW)
    edges = [pl.BlockSpec(edge_block, lambda i, j: (i, jnp.maximum(tidx(j) - 1, 0), 0, 0, 0)),
             pl.BlockSpec(edge_block, lambda i, j: (i, jnp.minimum(tidx(j) + 1, nblk - 1), 0, 0, 0))]
    gate = [pl.BlockSpec((LRU_BLOCKS, LRU_BW, 2 * LRU_BW), lambda i, j: (0, 0, 0)),
            pl.BlockSpec((2, LRU_BLOCKS, LRU_BW), lambda i, j: (0, 0, 0)),
            pl.BlockSpec((LRU_BLOCKS, LRU_BW), lambda i, j: (0, 0))]
    return nblk, slab, state, edges, gate


def _lru_bwd(xr, edges, cw, cb, wg, bg, kh, h0, *, tt):
    b, _, t, _ = xr.shape
    nblk, slab, state, edge_specs, gate = _lru_specs(b, t, tt, True)
    slab_shape = jax.ShapeDtypeStruct(xr.shape, F32)
    tile = (LRU_BLOCKS, tt, LRU_BW)
    return pl.pallas_call(
        functools.partial(_lru_bwd_kernel, tt=tt),
        grid=(b, nblk),
        in_specs=[slab] + edge_specs + [
                  pl.BlockSpec((CONV_W, LRU_BLOCKS, LRU_BW), lambda i, j: (0, 0, 0)),
                  pl.BlockSpec((LRU_BLOCKS, LRU_BW), lambda i, j: (0, 0))] + gate + [state],
        out_specs=[slab, slab, state],
        out_shape=[slab_shape, slab_shape, jax.ShapeDtypeStruct(h0.shape, F32)],
        scratch_shapes=[pltpu.VMEM((LRU_BLOCKS, tt + (CONV_W - 1) * SUBLANES, LRU_BW), F32),
                        pltpu.VMEM(tile, F32), pltpu.VMEM(tile, F32),
                        pltpu.VMEM((LRU_BLOCKS, SUBLANES, LRU_BW), F32)],
        compiler_params=_params(2, 48),
        name="lru_bwd",
    )(xr, edges, edges, cw, cb, wg, bg, kh, h0)


def _lru_fwd(xc, wg, bg, kh, h0, hb, g, w_out, x, gp, *, tt):
    b, _, t, _ = xc.shape
    d = x.shape[-1]
    nblk, slab, state, _, gate = _lru_specs(b, t, tt, False)
    tile = (LRU_BLOCKS, tt, LRU_BW)
    resid = pl.BlockSpec((None, tt, d), lambda i, j: (i, j, 0))
    return pl.pallas_call(
        functools.partial(_lru_fwd_kernel, tt=tt),
        grid=(b, nblk),
        in_specs=[slab] + gate + [state, slab, slab,
                                  pl.BlockSpec((LRU_WIDTH, d), lambda i, j: (0, 0)),
                                  resid,
                                  pl.BlockSpec((None, 1, d), lambda i, j: (i, 0, 0))],
        out_specs=[resid, state],
        out_shape=[jax.ShapeDtypeStruct((b, t, d), F32),
                   jax.ShapeDtypeStruct(h0.shape, F32)],
        scratch_shapes=[pltpu.VMEM(tile, F32), pltpu.VMEM(tile, F32),
                        pltpu.VMEM((LRU_BLOCKS, SUBLANES, LRU_BW), F32),
                        pltpu.VMEM((tt, LRU_WIDTH), BF16),
                        pltpu.VMEM((d // LANES, tt, LANES), F32)],
        compiler_params=_params(2, 56),
        name="lru_fwd",
    )(xc, wg, bg, kh, h0, hb, g, w_out, x, gp)


def _rope_tables(n_tok):
    n_rows = n_tok // GRID_W
    row = jnp.repeat(jnp.arange(n_rows, dtype=F32), GRID_W)
    col = jnp.tile(jnp.arange(GRID_W, dtype=F32), n_rows)
    n_freq = RET_DK // 4
    inv = ROPE_BASE ** (-jnp.arange(n_freq, dtype=F32) / n_freq)
    ang = jnp.concatenate([row[:, None] * inv, col[:, None] * inv], axis=-1)
    return jnp.cos(ang), jnp.sin(ang)


def _ret_w_prep_kernel(w_ref, m_ref, o_ref):
    o_ref[...] = jnp.dot(w_ref[...].astype(BF16), m_ref[...],
                         preferred_element_type=F32).astype(BF16)


def _ret_w_prep(w_in_all, layer):
    d = w_in_all.shape[1]
    src = jnp.arange(RET_DK)
    dst = (src % 2) * HALF_DK + src // 2
    perm = jnp.zeros((RET_DK, RET_DK), F32).at[src, dst].set(1.0)
    eye = jnp.eye(RET_DK, dtype=F32)
    mats = jnp.stack([perm, eye, 0.5 * eye]).astype(BF16)
    qk_blocks = 2 * RET_QK // RET_DK
    v_blocks = RET_V // RET_DK

    def which(j):
        return (j >= qk_blocks).astype(jnp.int32) + (j >= qk_blocks + v_blocks).astype(jnp.int32)

    return pl.pallas_call(
        _ret_w_prep_kernel,
        grid=(RET_IN // RET_DK,),
        in_specs=[pl.BlockSpec((None, d, RET_DK), lambda j: (layer, 0, j)),
                  pl.BlockSpec((None, RET_DK, RET_DK), lambda j: (which(j), 0, 0))],
        out_specs=pl.BlockSpec((d, RET_DK), lambda j: (0, j)),
        out_shape=jax.ShapeDtypeStruct((d, RET_IN), BF16),
        compiler_params=_params(1, 32),
        name="ret_w_prep",
    )(w_in_all, mats)


def _retention_layer(x, s_ctx, mods, cos, sin, w_in_all, layer, log_decay, gn_g, w_out):
    (mul_l, add_l, gp_l), (mul_c, add_c, gp_c) = mods
    b = x.shape[0]
    lc = s_ctx.shape[1]
    w = _ret_w_prep(w_in_all, layer)
    lg =-jnp.abs(log_decay.astype(F32))
    w_o = (gn_g.astype(F32)[:, None] * w_out).astype(BF16)

    p_l = _ret_inproj(x, mul_l, add_l, w, cos, sin, rope=True, tm=ROW_TILE)
    p_c = _ret_inproj(s_ctx, mul_c, add_c, w, cos[:lc], sin[:lc], rope=False, tm=lc)
    zero = jnp.zeros((b, RET_HEADS, RET_DK, RET_DV), F32)
    sb_c, sb_fin = _ret_bwd(p_c, lg, zero)
    sb_l, _ = _ret_bwd(p_l, lg, sb_fin)
    s_ctx, sf_c = _ret_fwd(p_c, sb_c, lg, zero, w_o, s_ctx, gp_c)
    x, _ = _ret_fwd(p_l, sb_l, lg, sf_c, w_o, x, gp_l)
    return x, s_ctx


def _lru_layer(x, s_ctx, mods, w_in, conv_w, conv_b, w_a, b_a, w_x, b_x, lam, w_out):
    (mul_l, add_l, gp_l), (mul_c, add_c, gp_c) = mods
    b = x.shape[0]
    lc = s_ctx.shape[1]
    w = jnp.concatenate([w_in[:, :LRU_WIDTH], 0.5 * w_in[:, LRU_WIDTH:]], axis=1).astype(BF16)
    w_o = w_out.astype(BF16)
    cw = conv_w.astype(F32).reshape(CONV_W, LRU_BLOCKS, LRU_BW)
    cb = conv_b.astype(F32).reshape(LRU_BLOCKS, LRU_BW)
    kh = (0.25 * LRU_C) * jax.nn.softplus(-lam.astype(F32))

    def gate_params(d):
        wg = (0.5 * jnp.concatenate([w_a[d], w_x[d]], axis=-1)).astype(BF16)
        bg = (0.5 * jnp.stack([b_a[d], b_x[d]])).astype(F32).reshape(2, LRU_BLOCKS, LRU_BW)
        return wg, bg, kh[d].reshape(LRU_BLOCKS, LRU_BW)

    xr_l, g_l, edges_l = _lru_inproj(x, mul_l, add_l, w, tm=ROW_TILE)
    xr_c, g_c, edges_c = _lru_inproj(s_ctx, mul_c, add_c, w, tm=lc)
    zero = jnp.zeros((b, LRU_BLOCKS, SUBLANES, LRU_BW), F32)
    hb_c, xc_c, hb_fin = _lru_bwd(xr_c, edges_c, cw, cb, *gate_params(1), zero, tt=lc)
    hb_l, xc_l, _ = _lru_bwd(xr_l, edges_l, cw, cb, *gate_params(1), hb_fin, tt=ROW_TILE)
    s_ctx, hf_fin = _lru_fwd(xc_c, *gate_params(0), zero, hb_c, g_c, w_o, s_ctx, gp_c, tt=lc)
    x, _ = _lru_fwd(xc_l, *gate_params(0), hf_fin, hb_l, g_l, w_o, x, gp_l, tt=ROW_TILE)
    return x, s_ctx


def kernel(x, c, ctx, c_ctx, mod_w, mod_b, norm_pre, norm_post, ret_w_in, ret_log_decay, ret_gn,
           ret_w_out, lru_w_in, lru_conv_w, lru_conv_b, lru_w_a, lru_b_a, lru_w_x, lru_b_x,
           lru_lambda, lru_w_out):
    b, n_tok, d = x.shape
    depth = mod_w.shape[0]
    assert n_tok % ROW_TILE == 0 and n_tok % RET_CHUNK == 0
    assert ctx.shape[1] % RET_CHUNK == 0 and ctx.shape[1] <= ROW_TILE
    cos, sin = _rope_tables(n_tok)

    rows = -(-(b + 1) // SUBLANES) * SUBLANES
    cc = jnp.concatenate([c, c_ctx[None, :], jnp.zeros((rows - b - 1, d), F32)], axis=0)
    mod = _modulation(cc, mod_w, mod_b)

    s_ctx = ctx
    for i in range(depth):
        shift, scale, gate = jnp.split(mod[i], 3, axis=-1)
        mods = []
        for sel in (slice(0, b), slice(b, b + 1)):
            mul = jnp.broadcast_to(norm_pre[i] * (1.0 + scale[sel]), (b, d))[:, None, :]
            add = jnp.broadcast_to(shift[sel], (b, d))[:, None, :]
            gp = jnp.broadcast_to(gate[sel] * norm_post[i], (b, d))[:, None, :]
            mods.append((mul, add, gp))
        j = i // 2
        if i % 2 == 0:
            x, s_ctx = _retention_layer(x, s_ctx, mods, cos, sin, ret_w_in, j, ret_log_decay[j],
                                        ret_gn[j], ret_w_out[j])
        else:
            x, s_ctx = _lru_layer(x, s_ctx, mods, lru_w_in[j], lru_conv_w[j], lru_conv_b[j],
                                  lru_w_a[j], lru_b_a[j], lru_w_x[j], lru_b_x[j],
                                  lru_lambda[j], lru_w_out[j])
    return x
```

```python
import functools

import jax
import jax.numpy as jnp
import numpy as np
from jax import lax
from jax.experimental import pallas as pl
from jax.experimental.pallas import tpu as pltpu

F32 = jnp.float32
BF16 = jnp.bfloat16

EPS = 1e-6
GRID_W = 64
ROPE_BASE = 10000.0

RET_HEADS = 4
RET_DK = 256
RET_DV = 512
RET_QK = RET_HEADS * RET_DK
RET_V = RET_HEADS * RET_DV
RET_IN = 2 * RET_QK + 2 * RET_V
HALF_DK = RET_DK // 2

LRU_BLOCKS = 10
LRU_BW = 128
LRU_WIDTH = LRU_BLOCKS * LRU_BW
CONV_W = 4
LRU_C = 8.0

LANES = 128
SUBLANES = 8
MIB = 1024 * 1024

RET_CHUNK = 256
RET_STEP_CHUNKS = 2
ROW_TILE = 512


def _params(n_axes, vmem_mib):
    return pltpu.CompilerParams(
        dimension_semantics=("arbitrary",) * n_axes,
        vmem_limit_bytes=vmem_mib * MIB)


def _smem_spec():
    return pl.BlockSpec(memory_space=pltpu.SMEM)


def _mod_kernel(cc_ref, w_ref, b_ref, o_ref):
    cc = cc_ref[...]
    act = cc * jax.nn.sigmoid(cc)
    o_ref[...] = jnp.dot(act.astype(BF16), w_ref[...].astype(BF16),
                         preferred_element_type=F32) + b_ref[...]


def _modulation(cc, mod_w, mod_b):
    depth, d, d3 = mod_w.shape
    rows = cc.shape[0]
    return pl.pallas_call(
        _mod_kernel,
        grid=(depth, d3 // d),
        in_specs=[pl.BlockSpec((rows, d), lambda i, j: (0, 0)),
                  pl.BlockSpec((None, d, d), lambda i, j: (i, 0, j)),
                  pl.BlockSpec((None, 1, d), lambda i, j: (i, 0, j))],
        out_specs=pl.BlockSpec((None, rows, d), lambda i, j: (i, 0, j)),
        out_shape=jax.ShapeDtypeStruct((depth, rows, d3), F32),
        compiler_params=_params(2, 32),
        name="modulation",
    )(cc, mod_w, mod_b.reshape(depth, 1, d3))


def _norm_modulate(x_ref, mul_ref, add_ref, h_scr):
    x = x_ref[...]
    ms = jnp.mean(x * x, axis=-1, keepdims=True)
    h = x * lax.rsqrt(ms + EPS) * mul_ref[...] + add_ref[...]
    h_scr[...] = h.astype(BF16)


def _ret_inproj_kernel(x_ref, mul_ref, add_ref, w_ref, cos_ref, sin_ref, o_ref, h_scr, *, rope):
    _norm_modulate(x_ref, mul_ref, add_ref, h_scr)
    hb = h_scr[...]
    scale = RET_DK ** -0.5
    for j in range(2 * RET_HEADS):
        cols = slice(j * RET_DK, (j + 1) * RET_DK)
        acc = jnp.dot(hb, w_ref[:, cols], preferred_element_type=F32)
        if j >= RET_HEADS:
            acc = acc * scale
        if rope:
            e, o = acc[:, :HALF_DK], acc[:, HALF_DK:]
            c, s = cos_ref[...], sin_ref[...]
            acc = jnp.concatenate([e * c - o * s, e * s + o * c], axis=1)
        o_ref[:, cols] = acc.astype(BF16)
    for j in range(2 * RET_QK // RET_DV, RET_IN // RET_DV):
        cols = slice(j * RET_DV, (j + 1) * RET_DV)
        o_ref[:, cols] = jnp.dot(hb, w_ref[:, cols], preferred_element_type=F32).astype(BF16)


def _ret_inproj(x, mul, add, w, cos, sin, *, rope, tm):
    b, t, d = x.shape
    kern = functools.partial(_ret_inproj_kernel, rope=rope)
    return pl.pallas_call(
        kern,
        grid=(b, t // tm),
        in_specs=[pl.BlockSpec((None, tm, d), lambda i, j: (i, j, 0)),
                  pl.BlockSpec((None, 1, d), lambda i, j: (i, 0, 0)),
                  pl.BlockSpec((None, 1, d), lambda i, j: (i, 0, 0)),
                  pl.BlockSpec((d, RET_IN), lambda i, j: (0, 0)),
                  pl.BlockSpec((tm, HALF_DK), lambda i, j: (j, 0)),
                  pl.BlockSpec((tm, HALF_DK), lambda i, j: (j, 0))],
        out_specs=pl.BlockSpec((None, tm, RET_IN), lambda i, j: (i, j, 0)),
        out_shape=jax.ShapeDtypeStruct((b, t, RET_IN), BF16),
        scratch_shapes=[pltpu.VMEM((tm, d), BF16)],
        compiler_params=_params(2, 56),
        name="ret_inproj_rope" if rope else "ret_inproj",
    )(x, mul, add, w, cos, sin)


def _store_interleaved(dst_ref, blk, val):
    seg = val.shape[0] // SUBLANES
    for s in range(SUBLANES):
        dst_ref[blk, pl.ds(s, seg, stride=SUBLANES), :] = val[s * seg:(s + 1) * seg, :]


def _lru_inproj_kernel(x_ref, xh_ref, mul_ref, add_ref, w_ref, cw_ref, cb_ref, xc_ref, g_ref,
                       h_scr, hs_scr, ext_scr):
    i = pl.program_id(1)
    has_prev = i > 0
    has_next = i < pl.num_programs(1) - 1
    tm, d = x_ref.shape
    halo_rows = xh_ref.shape[0]

    def norm_modulate(x):
        ms = jnp.mean(x * x, axis=-1, keepdims=True)
        return x * lax.rsqrt(ms + EPS) * mul_ref[...] + add_ref[...]

    h = norm_modulate(x_ref[...])
    for s in range(d // LANES):
        _store_interleaved(hs_scr, s, h[:, s * LANES:(s + 1) * LANES])
    for s in range(d // LANES):
        h_scr[0:tm, s * LANES:(s + 1) * LANES] = hs_scr[s].astype(BF16)
    h_scr[tm:tm + halo_rows, :] = norm_modulate(xh_ref[...]).astype(BF16)

    sub = lax.broadcasted_iota(jnp.int32, (SUBLANES, LRU_BW), 0)
    left = CONV_W // 2
    for j in range(LRU_WIDTH // (2 * LANES)):
        cols = slice(j * 2 * LANES, (j + 1) * 2 * LANES)
        acc = jnp.dot(h_scr[...], w_ref[:, cols], preferred_element_type=F32)
        gcols = slice(LRU_WIDTH + j * 2 * LANES, LRU_WIDTH + (j + 1) * 2 * LANES)
        gacc = jnp.dot(h_scr[0:tm, :], w_ref[:, gcols], preferred_element_type=F32).astype(BF16)
        for half in range(2):
            blk = 2 * j + half
            lanes = slice(half * LANES, (half + 1) * LANES)
            g_ref[blk] = gacc[:, lanes]
            ext = ext_scr.at[half]
            ext[left * SUBLANES:left * SUBLANES + tm, :] = acc[0:tm, lanes]
            for k in range(left):
                src = acc[tm - (left - k) * SUBLANES:tm - (left - k - 1) * SUBLANES, lanes]
                before = jnp.where(has_prev, acc[tm + k:tm + k + 1, lanes], 0.0)
                ext[k * SUBLANES:(k + 1) * SUBLANES, :] = jnp.where(
                    sub == 0, before, pltpu.roll(src, 1, 0))
            for k in range(CONV_W - 1 - left):
                src = acc[k * SUBLANES:(k + 1) * SUBLANES, lanes]
                after = jnp.where(has_next, acc[tm + left + k:tm + left + k + 1, lanes], 0.0)
                ext[(left + k) * SUBLANES + tm:(left + k + 1) * SUBLANES + tm, :] = jnp.where(
                    sub == SUBLANES - 1, after, pltpu.roll(src, SUBLANES - 1, 0))
            xc = cb_ref[blk:blk + 1, :]
            for tap in range(CONV_W):
                xc = xc + ext[tap * SUBLANES:tap * SUBLANES + tm, :] * cw_ref[tap, blk:blk + 1, :]
            xc_ref[blk] = xc


def _conv_halo_rows(x, tm):
    b, t, d = x.shape
    nblk = t // tm
    r = x.reshape(b, nblk, tm, d)
    left = CONV_W // 2
    prev = jnp.concatenate([jnp.zeros((b, 1, left, d), F32), r[:, :-1, tm - left:, :]], axis=1)
    nxt = jnp.concatenate([r[:, 1:, 0:CONV_W - 1 - left, :],
                           jnp.zeros((b, 1, CONV_W - 1 - left, d), F32)], axis=1)
    pad = jnp.zeros((b, nblk, 2 * SUBLANES - (CONV_W - 1), d), F32)
    return jnp.concatenate([prev, nxt, pad], axis=2)


def _lru_inproj(x, mul, add, w, cw, cb, *, tm):
    b, t, d = x.shape
    halo_rows = 2 * SUBLANES
    slab = pl.BlockSpec((None, LRU_BLOCKS, tm, LRU_BW), lambda i, j: (i, 0, j, 0))
    return pl.pallas_call(
        _lru_inproj_kernel,
        grid=(b, t // tm),
        in_specs=[pl.BlockSpec((None, tm, d), lambda i, j: (i, j, 0)),
                  pl.BlockSpec((None, None, halo_rows, d), lambda i, j: (i, j, 0, 0)),
                  pl.BlockSpec((None, 1, d), lambda i, j: (i, 0, 0)),
                  pl.BlockSpec((None, 1, d), lambda i, j: (i, 0, 0)),
                  pl.BlockSpec((d, 2 * LRU_WIDTH), lambda i, j: (0, 0)),
                  pl.BlockSpec((CONV_W, LRU_BLOCKS, LRU_BW), lambda i, j: (0, 0, 0)),
                  pl.BlockSpec((LRU_BLOCKS, LRU_BW), lambda i, j: (0, 0))],
        out_specs=[slab, slab],
        out_shape=[jax.ShapeDtypeStruct((b, LRU_BLOCKS, t, LRU_BW), F32),
                   jax.ShapeDtypeStruct((b, LRU_BLOCKS, t, LRU_BW), BF16)],
        scratch_shapes=[pltpu.VMEM((tm + halo_rows, d), BF16),
                        pltpu.VMEM((d // LANES, tm, LANES), F32),
                        pltpu.VMEM((2, tm + (CONV_W - 1) * SUBLANES, LRU_BW), F32)],
        compiler_params=_params(2, 48),
        name="lru_inproj",
    )(x, _conv_halo_rows(x, tm), mul, add, w, cw, cb)


def _gated_norm_proj(y, w_ref, gp_ref):
    z = jnp.dot(y, w_ref[...], preferred_element_type=F32)
    ms = jnp.mean(z * z, axis=-1, keepdims=True)
    return z * lax.rsqrt(ms + EPS) * gp_ref[...]


def _residual_out(x_ref, zn, o_ref, rows):
    o_ref[rows, :] = x_ref[rows, :] + zn


def _residual_out_interleaved(x_ref, zn, o_ref, z_scr):
    n_slab = zn.shape[1] // LANES
    seg = zn.shape[0] // SUBLANES
    for d in range(n_slab):
        z_scr[d] = zn[:, d * LANES:(d + 1) * LANES]
    for d in range(n_slab):
        cols = slice(d * LANES, (d + 1) * LANES)
        for s in range(SUBLANES):
            rows = slice(s * seg, (s + 1) * seg)
            o_ref[rows, cols] = x_ref[rows, cols] + z_scr[d, pl.ds(s, seg, stride=SUBLANES), :]


def _row_scale(x, dec):
    n = x.shape[1] // LANES
    return jnp.concatenate([x[:, i * LANES:(i + 1) * LANES] * dec for i in range(n)], axis=1)


def _chunk_pos():
    return lax.broadcasted_iota(jnp.int32, (RET_CHUNK, LANES), 0).astype(F32)


def _ret_bwd_kernel(lg_ref, k_ref, v_ref, s0_ref, sb_ref, sfin_ref, s_scr, dec_scr, *, nch):
    c = pl.program_id(1)

    @pl.when(c == 0)
    def _():
        s_scr[...] = s0_ref[...]
        pos = _chunk_pos()
        for h in range(RET_HEADS):
            dec_scr[h] = jnp.exp(lg_ref[1, h] * pos).astype(BF16)

    for ci in reversed(range(nch)):
        rows = slice(ci * RET_CHUNK, (ci + 1) * RET_CHUNK)
        for h in range(RET_HEADS):
            k = k_ref[rows, h * RET_DK:(h + 1) * RET_DK]
            v = v_ref[rows, h * RET_DV:(h + 1) * RET_DV]
            st = s_scr[h]
            sb_ref[ci, h] = st.astype(BF16)
            kd = _row_scale(k, dec_scr[h])
            kv = lax.dot_general(kd, v, (((0,), (0,)), ((), ())), preferred_element_type=F32)
            s_scr[h] = st * jnp.exp(lg_ref[1, h] * RET_CHUNK) + kv

    @pl.when(c == pl.num_programs(1) - 1)
    def _():
        sfin_ref[...] = s_scr[...]


def _ret_fwd_kernel(lg_ref, q_ref, k_ref, v_ref, g_ref, sb_ref, s0_ref, w_ref, x_ref, gp_ref,
                    xo_ref, sfin_ref, s_scr, m_scr, dec_scr, y_scr, *, nch):
    c = pl.program_id(1)

    @pl.when(c == 0)
    def _():
        s_scr[...] = s0_ref[...]
        ri = lax.broadcasted_iota(jnp.int32, (RET_CHUNK, RET_CHUNK), 0)
        ci = lax.broadcasted_iota(jnp.int32, (RET_CHUNK, RET_CHUNK), 1)
        rel = (ri - ci).astype(F32)
        for h in range(RET_HEADS):
            mf = jnp.where(rel >= 0, jnp.exp(lg_ref[0, h] * jnp.maximum(rel, 0.0)), 0.0)
            mb = jnp.where(rel <= 0, jnp.exp(lg_ref[1, h] * jnp.maximum(-rel, 0.0)), 0.0)
            m_scr[h] = mf + mb
        pos = _chunk_pos()
        for h in range(RET_HEADS):
            dec_scr[h, 0] = jnp.exp(lg_ref[0, h] * (pos + 1.0)).astype(BF16)
            dec_scr[h, 1] = jnp.exp(lg_ref[1, h] * (RET_CHUNK - pos)).astype(BF16)
            dec_scr[h, 2] = jnp.exp(lg_ref[0, h] * (RET_CHUNK - 1.0 - pos)).astype(BF16)

    for ci, h in [(ci, h) for ci in range(nch) for h in range(RET_HEADS)]:
        rows = slice(ci * RET_CHUNK, (ci + 1) * RET_CHUNK)
        vcols = slice(h * RET_DV, (h + 1) * RET_DV)
        q = q_ref[rows, h * RET_DK:(h + 1) * RET_DK]
        k = k_ref[rows, h * RET_DK:(h + 1) * RET_DK]
        v = v_ref[rows, vcols]
        st = s_scr[h]
        s = lax.dot_general(q, k, (((1,), (1,)), ((), ())), preferred_element_type=F32)
        o = jnp.dot((s * m_scr[h]).astype(BF16), v, preferred_element_type=F32)
        o = jnp.dot(_row_scale(q, dec_scr[h, 0]), st.astype(BF16), preferred_element_type=F32) + o
        o = jnp.dot(_row_scale(q, dec_scr[h, 1]), sb_ref[ci, h], preferred_element_type=F32) + o
        kv = lax.dot_general(_row_scale(k, dec_scr[h, 2]), v, (((0,), (0,)), ((), ())),
                             preferred_element_type=F32)
        s_scr[h] = st * jnp.exp(lg_ref[0, h] * RET_CHUNK) + kv
        dlt = o - jnp.mean(o, axis=-1, keepdims=True)
        var = jnp.mean(dlt * dlt, axis=-1, keepdims=True)
        gh = g_ref[rows, vcols]
        y_scr[rows, vcols] = ((dlt * lax.rsqrt(var + EPS)).astype(BF16)
                              * (gh * (jnp.tanh(gh) + 1.0)))
        if h == RET_HEADS - 1:
            _residual_out(x_ref, _gated_norm_proj(y_scr[rows, :], w_ref, gp_ref), xo_ref, rows)

    @pl.when(c == pl.num_programs(1) - 1)
    def _():
        sfin_ref[...] = s_scr[...]


def _state_spec():
    return pl.BlockSpec((None, RET_HEADS, RET_DK, RET_DV), lambda i, j: (i, 0, 0, 0))


def _step_chunks(n_chunks):
    return RET_STEP_CHUNKS if n_chunks % RET_STEP_CHUNKS == 0 else 1


def _ret_bwd(p, lg, s0):
    b, t, _ = p.shape
    nc = t // RET_CHUNK
    nch = _step_chunks(nc)
    ns = nc // nch
    rev = lambda blk: (lambda i, j: (i, ns - 1 - j, blk))
    hist = pl.BlockSpec((None, nch, RET_HEADS, RET_DK, RET_DV),
                        lambda i, j: (i, ns - 1 - j, 0, 0, 0))
    return pl.pallas_call(
        functools.partial(_ret_bwd_kernel, nch=nch),
        grid=(b, ns),
        in_specs=[_smem_spec(),
                  pl.BlockSpec((None, nch * RET_CHUNK, RET_QK), rev(1)),
                  pl.BlockSpec((None, nch * RET_CHUNK, RET_V), rev(1)),
                  _state_spec()],
        out_specs=[hist, _state_spec()],
        out_shape=[jax.ShapeDtypeStruct((b, nc, RET_HEADS, RET_DK, RET_DV), BF16),
                   jax.ShapeDtypeStruct(s0.shape, F32)],
        scratch_shapes=[pltpu.VMEM((RET_HEADS, RET_DK, RET_DV), F32),
                        pltpu.VMEM((RET_HEADS, RET_CHUNK, LANES), BF16)],
        compiler_params=_params(2, 48),
        name="ret_bwd",
    )(lg, p, p, s0)


def _ret_fwd(p, sb, lg, s0, w_out, x, gp):
    b, t, _ = p.shape
    d = x.shape[-1]
    nc = t // RET_CHUNK
    nch = _step_chunks(nc)
    rows = nch * RET_CHUNK
    fwd = lambda blk: (lambda i, j: (i, j, blk))
    resid = pl.BlockSpec((None, rows, d), fwd(0))
    return pl.pallas_call(
        functools.partial(_ret_fwd_kernel, nch=nch),
        grid=(b, nc // nch),
        in_specs=[_smem_spec(),
                  pl.BlockSpec((None, rows, RET_QK), fwd(0)),
                  pl.BlockSpec((None, rows, RET_QK), fwd(1)),
                  pl.BlockSpec((None, rows, RET_V), fwd(1)),
                  pl.BlockSpec((None, rows, RET_V), fwd(2)),
                  pl.BlockSpec((None, nch, RET_HEADS, RET_DK, RET_DV),
                               lambda i, j: (i, j, 0, 0, 0)),
                  _state_spec(),
                  pl.BlockSpec((RET_V, d), lambda i, j: (0, 0)),
                  resid,
                  pl.BlockSpec((None, 1, d), lambda i, j: (i, 0, 0))],
        out_specs=[resid, _state_spec()],
        out_shape=[jax.ShapeDtypeStruct((b, t, d), F32),
                   jax.ShapeDtypeStruct(s0.shape, F32)],
        scratch_shapes=[pltpu.VMEM((RET_HEADS, RET_DK, RET_DV), F32),
                        pltpu.VMEM((RET_HEADS, RET_CHUNK, RET_CHUNK), F32),
                        pltpu.VMEM((RET_HEADS, 3, RET_CHUNK, LANES), BF16),
                        pltpu.VMEM((rows, RET_V), BF16)],
        compiler_params=_params(2, 56),
        name="ret_fwd",
    )(lg, p, p, p, p, sb, s0, w_out, x, gp)


def _lru_coeffs(xc_ref, wg_ref, bg_ref, kh_ref, a_scr, u_scr):
    for nb in range(LRU_BLOCKS):
        xc = xc_ref[nb]
        res = jnp.dot(xc.astype(BF16), wg_ref[nb], preferred_element_type=F32)
        t_r = jnp.tanh(res[:, :LRU_BW] + bg_ref[0, nb:nb + 1, :])
        t_g = jnp.tanh(res[:, LRU_BW:] + bg_ref[1, nb:nb + 1, :])
        kh = kh_ref[nb:nb + 1, :]
        p = jnp.tanh(t_r * kh + kh)
        w = 1.0 / (1.0 + p)
        a_scr[nb] = (1.0 - p) * w
        sqrt_p = jnp.where(p > 0.0, p * lax.rsqrt(p), 0.0)
        u_scr[nb] = sqrt_p * w * (t_g + 1.0) * xc


def _lru_scan(a_scr, u_scr, dst_ref, carry_scr, tt, reverse):
    n = tt // SUBLANES
    shape = (LRU_BLOCKS, SUBLANES, LRU_BW)
    sub = lax.broadcasted_iota(jnp.int32, shape, 1)
    steps = [slice(g * SUBLANES, (g + 1) * SUBLANES) for g in range(n)]
    if reverse:
        steps = steps[::-1]

    a_seg = a_scr[:, steps[0], :]
    h_seg = u_scr[:, steps[0], :]
    for rows in steps[1:]:
        a = a_scr[:, rows, :]
        h_seg = a * h_seg + u_scr[:, rows, :]
        a_seg = a_seg * a
    for d in (1, 2, 4):
        ok = (sub < SUBLANES - d) if reverse else (sub >= d)
        shift = SUBLANES - d if reverse else d
        a_s = jnp.where(ok, pltpu.roll(a_seg, shift, 1), 1.0)
        h_s = jnp.where(ok, pltpu.roll(h_seg, shift, 1), 0.0)
        h_seg = a_seg * h_s + h_seg
        a_seg = a_seg * a_s
    carry = carry_scr[...]
    h_end = a_seg * carry + h_seg
    if reverse:
        h_in = jnp.where(sub == SUBLANES - 1, carry, pltpu.roll(h_end, SUBLANES - 1, 1))
        last = h_end[:, 0:1, :]
    else:
        h_in = jnp.where(sub == 0, carry, pltpu.roll(h_end, 1, 1))
        last = h_end[:, SUBLANES - 1:SUBLANES, :]
    carry_scr[...] = jnp.broadcast_to(last, shape)

    h = h_in
    for rows in steps:
        h = a_scr[:, rows, :] * h + u_scr[:, rows, :]
        dst_ref[:, rows, :] = h


def _lru_bwd_kernel(xc_ref, wg_ref, bg_ref, kh_ref, h0_ref, hb_ref, hfin_ref,
                    a_scr, u_scr, carry_scr, *, tt):
    i = pl.program_id(1)

    @pl.when(i == 0)
    def _():
        carry_scr[...] = h0_ref[...]

    _lru_coeffs(xc_ref, wg_ref, bg_ref, kh_ref, a_scr, u_scr)
    _lru_scan(a_scr, u_scr, hb_ref, carry_scr, tt, True)

    @pl.when(i == pl.num_programs(1) - 1)
    def _():
        hfin_ref[...] = carry_scr[...]


def _lru_fwd_kernel(xc_ref, wg_ref, bg_ref, kh_ref, h0_ref, hb_ref, g_ref, w_ref, x_ref, gp_ref,
                    xo_ref, hfin_ref, a_scr, u_scr, carry_scr, y_scr, z_scr, *, tt):
    i = pl.program_id(1)

    @pl.when(i == 0)
    def _():
        carry_scr[...] = h0_ref[...]

    _lru_coeffs(xc_ref, wg_ref, bg_ref, kh_ref, a_scr, u_scr)
    _lru_scan(a_scr, u_scr, u_scr, carry_scr, tt, False)
    for nb in range(LRU_BLOCKS):
        gh = g_ref[nb]
        y = (u_scr[nb] + hb_ref[nb]).astype(BF16) * (gh * (jnp.tanh(gh) + 1.0))
        y_scr[:, nb * LRU_BW:(nb + 1) * LRU_BW] = y
    _residual_out_interleaved(x_ref, _gated_norm_proj(y_scr[...], w_ref, gp_ref), xo_ref, z_scr)

    @pl.when(i == pl.num_programs(1) - 1)
    def _():
        hfin_ref[...] = carry_scr[...]


def _lru_specs(b, t, tt, reverse):
    nblk = t // tt
    tidx = (lambda j: nblk - 1 - j) if reverse else (lambda j: j)
    slab = pl.BlockSpec((None, LRU_BLOCKS, tt, LRU_BW), lambda i, j: (i, 0, tidx(j), 0))
    state = pl.BlockSpec((None, LRU_BLOCKS, SUBLANES, LRU_BW), lambda i, j: (i, 0, 0, 0))
    gate = [pl.BlockSpec((LRU_BLOCKS, LRU_BW, 2 * LRU_BW), lambda i, j: (0, 0, 0)),
            pl.BlockSpec((2, LRU_BLOCKS, LRU_BW), lambda i, j: (0, 0, 0)),
            pl.BlockSpec((LRU_BLOCKS, LRU_BW), lambda i, j: (0, 0))]
    return nblk, slab, state, gate


def _lru_bwd(xc, wg, bg, kh, h0, *, tt):
    b, _, t, _ = xc.shape
    nblk, slab, state, gate = _lru_specs(b, t, tt, True)
    tile = (LRU_BLOCKS, tt, LRU_BW)
    return pl.pallas_call(
        functools.partial(_lru_bwd_kernel, tt=tt),
        grid=(b, nblk),
        in_specs=[slab] + gate + [state],
        out_specs=[slab, state],
        out_shape=[jax.ShapeDtypeStruct(xc.shape, F32), jax.ShapeDtypeStruct(h0.shape, F32)],
        scratch_shapes=[pltpu.VMEM(tile, F32), pltpu.VMEM(tile, F32),
                        pltpu.VMEM((LRU_BLOCKS, SUBLANES, LRU_BW), F32)],
        compiler_params=_params(2, 48),
        name="lru_bwd",
    )(xc, wg, bg, kh, h0)


def _lru_fwd(xc, wg, bg, kh, h0, hb, g, w_out, x, gp, *, tt):
    b, _, t, _ = xc.shape
    d = x.shape[-1]
    nblk, slab, state, gate = _lru_specs(b, t, tt, False)
    tile = (LRU_BLOCKS, tt, LRU_BW)
    resid = pl.BlockSpec((None, tt, d), lambda i, j: (i, j, 0))
    return pl.pallas_call(
        functools.partial(_lru_fwd_kernel, tt=tt),
        grid=(b, nblk),
        in_specs=[slab] + gate + [state, slab, slab,
                                  pl.BlockSpec((LRU_WIDTH, d), lambda i, j: (0, 0)),
                                  resid,
                                  pl.BlockSpec((None, 1, d), lambda i, j: (i, 0, 0))],
        out_specs=[resid, state],
        out_shape=[jax.ShapeDtypeStruct((b, t, d), F32),
                   jax.ShapeDtypeStruct(h0.shape, F32)],
        scratch_shapes=[pltpu.VMEM(tile, F32), pltpu.VMEM(tile, F32),
                        pltpu.VMEM((LRU_BLOCKS, SUBLANES, LRU_BW), F32),
                        pltpu.VMEM((tt, LRU_WIDTH), BF16),
                        pltpu.VMEM((d // LANES, tt, LANES), F32)],
        compiler_params=_params(2, 56),
        name="lru_fwd",
    )(xc, wg, bg, kh, h0, hb, g, w_out, x, gp)


def _rope_tables(n_tok):
    n_rows = n_tok // GRID_W
    row = jnp.repeat(jnp.arange(n_rows, dtype=F32), GRID_W)
    col = jnp.tile(jnp.arange(GRID_W, dtype=F32), n_rows)
    n_freq = RET_DK // 4
    inv = ROPE_BASE ** (-jnp.arange(n_freq, dtype=F32) / n_freq)
    ang = jnp.concatenate([row[:, None] * inv, col[:, None] * inv], axis=-1)
    return jnp.cos(ang), jnp.sin(ang)


def _ret_w_prep_kernel(w_ref, m_ref, o_ref):
    o_ref[...] = jnp.dot(w_ref[...].astype(BF16), m_ref[...],
                         preferred_element_type=F32).astype(BF16)


def _ret_w_prep(w_in_all, layer):
    d = w_in_all.shape[1]
    src = np.arange(RET_DK)
    perm = np.zeros((RET_DK, RET_DK), np.float32)
    perm[src, (src % 2) * HALF_DK + src // 2] = 1.0
    eye = np.eye(RET_DK, dtype=np.float32)
    mats = jnp.asarray(np.stack([perm, eye, 0.5 * eye]), BF16)
    qk_blocks = 2 * RET_QK // RET_DK
    v_blocks = RET_V // RET_DK

    def which(j):
        return (j >= qk_blocks).astype(jnp.int32) + (j >= qk_blocks + v_blocks).astype(jnp.int32)

    return pl.pallas_call(
        _ret_w_prep_kernel,
        grid=(RET_IN // RET_DK,),
        in_specs=[pl.BlockSpec((None, d, RET_DK), lambda j: (layer, 0, j)),
                  pl.BlockSpec((None, RET_DK, RET_DK), lambda j: (which(j), 0, 0))],
        out_specs=pl.BlockSpec((d, RET_DK), lambda j: (0, j)),
        out_shape=jax.ShapeDtypeStruct((d, RET_IN), BF16),
        compiler_params=_params(1, 32),
        name="ret_w_prep",
    )(w_in_all, mats)


def _retention_layer(x, s_ctx, mods, cos, sin, w_in_all, layer, log_decay, gn_g, w_out):
    (mul_l, add_l, gp_l), (mul_c, add_c, gp_c) = mods
    b = x.shape[0]
    lc = s_ctx.shape[1]
    w = _ret_w_prep(w_in_all, layer)
    lg =-jnp.abs(log_decay.astype(F32))
    w_o = (gn_g.astype(F32)[:, None] * w_out).astype(BF16)

    p_l = _ret_inproj(x, mul_l, add_l, w, cos, sin, rope=True, tm=ROW_TILE)
    p_c = _ret_inproj(s_ctx, mul_c, add_c, w, cos[:lc], sin[:lc], rope=False, tm=lc)
    zero = jnp.zeros((b, RET_HEADS, RET_DK, RET_DV), F32)
    sb_c, sb_fin = _ret_bwd(p_c, lg, zero)
    sb_l, _ = _ret_bwd(p_l, lg, sb_fin)
    s_ctx, sf_c = _ret_fwd(p_c, sb_c, lg, zero, w_o, s_ctx, gp_c)
    x, _ = _ret_fwd(p_l, sb_l, lg, sf_c, w_o, x, gp_l)
    return x, s_ctx


def _lru_layer(x, s_ctx, mods, w_in, conv_w, conv_b, w_a, b_a, w_x, b_x, lam, w_out):
    (mul_l, add_l, gp_l), (mul_c, add_c, gp_c) = mods
    b = x.shape[0]
    lc = s_ctx.shape[1]
    w = jnp.concatenate([w_in[:, :LRU_WIDTH], 0.5 * w_in[:, LRU_WIDTH:]], axis=1).astype(BF16)
    w_o = w_out.astype(BF16)
    cw = conv_w.astype(F32).reshape(CONV_W, LRU_BLOCKS, LRU_BW)
    cb = conv_b.astype(F32).reshape(LRU_BLOCKS, LRU_BW)
    kh = (0.25 * LRU_C) * jax.nn.softplus(-lam.astype(F32))

    def gate_params(d):
        wg = (0.5 * jnp.concatenate([w_a[d], w_x[d]], axis=-1)).astype(BF16)
        bg = (0.5 * jnp.stack([b_a[d], b_x[d]])).astype(F32).reshape(2, LRU_BLOCKS, LRU_BW)
        return wg, bg, kh[d].reshape(LRU_BLOCKS, LRU_BW)

    xc_l, g_l = _lru_inproj(x, mul_l, add_l, w, cw, cb, tm=ROW_TILE)
    xc_c, g_c = _lru_inproj(s_ctx, mul_c, add_c, w, cw, cb, tm=lc)
    zero = jnp.zeros((b, LRU_BLOCKS, SUBLANES, LRU_BW), F32)
    hb_c, hb_fin = _lru_bwd(xc_c, *gate_params(1), zero, tt=lc)
    hb_l, _ = _lru_bwd(xc_l, *gate_params(1), hb_fin, tt=ROW_TILE)
    s_ctx, hf_fin = _lru_fwd(xc_c, *gate_params(0), zero, hb_c, g_c, w_o, s_ctx, gp_c, tt=lc)
    x, _ = _lru_fwd(xc_l, *gate_params(0), hf_fin, hb_l, g_l, w_o, x, gp_l, tt=ROW_TILE)
    return x, s_ctx


def kernel(x, c, ctx, c_ctx, mod_w, mod_b, norm_pre, norm_post, ret_w_in, ret_log_decay, ret_gn,
           ret_w_out, lru_w_in, lru_conv_w, lru_conv_b, lru_w_a, lru_b_a, lru_w_x, lru_b_x,
           lru_lambda, lru_w_out):
    b, n_tok, d = x.shape
    depth = mod_w.shape[0]
    assert n_tok % ROW_TILE == 0 and n_tok % RET_CHUNK == 0
    assert ctx.shape[1] % RET_CHUNK == 0 and ctx.shape[1] <= ROW_TILE
    cos, sin = _rope_tables(n_tok)

    rows = -(-(b + 1) // SUBLANES) * SUBLANES
    cc = jnp.concatenate([c, c_ctx[None, :], jnp.zeros((rows - b - 1, d), F32)], axis=0)
    mod = _modulation(cc, mod_w, mod_b)

    s_ctx = ctx
    for i in range(depth):
        shift, scale, gate = jnp.split(mod[i], 3, axis=-1)
        mods = []
        for sel in (slice(0, b), slice(b, b + 1)):
            mul = jnp.broadcast_to(norm_pre[i] * (1.0 + scale[sel]), (b, d))[:, None, :]
            add = jnp.broadcast_to(shift[sel], (b, d))[:, None, :]
            gp = jnp.broadcast_to(gate[sel] * norm_post[i], (b, d))[:, None, :]
            mods.append((mul, add, gp))
        j = i // 2
        if i % 2 == 0:
            x, s_ctx = _retention_layer(x, s_ctx, mods, cos, sin, ret_w_in, j, ret_log_decay[j],
                                        ret_gn[j], ret_w_out[j])
        else:
            x, s_ctx = _lru_layer(x, s_ctx, mods, lru_w_in[j], lru_conv_w[j], lru_conv_b[j],
                                  lru_w_a[j], lru_b_a[j], lru_w_x[j], lru_b_x[j],
                                  lru_lambda[j], lru_w_out[j])
    return x
```

```python
import functools

import jax
import jax.numpy as jnp
import numpy as np
from jax import lax
from jax.experimental import pallas as pl
from jax.experimental.pallas import tpu as pltpu

F32 = jnp.float32
BF16 = jnp.bfloat16

EPS = 1e-6
GRID_W = 64
ROPE_BASE = 10000.0

RET_HEADS = 4
RET_DK = 256
RET_DV = 512
RET_QK = RET_HEADS * RET_DK
RET_V = RET_HEADS * RET_DV
RET_IN = 2 * RET_QK + 2 * RET_V
HALF_DK = RET_DK // 2

LRU_BLOCKS = 10
LRU_BW = 128
LRU_WIDTH = LRU_BLOCKS * LRU_BW
CONV_W = 4
LRU_C = 8.0

LANES = 128
SUBLANES = 8
MIB = 1024 * 1024

RET_CHUNK = 256
RET_FWD_STEP_CHUNKS = 2
RET_BWD_STEP_CHUNKS = 4
RET_ROW_TILE = 1024
ROW_TILE = 512


def _params(n_axes, vmem_mib):
    return pltpu.CompilerParams(
        dimension_semantics=("arbitrary",) * n_axes,
        vmem_limit_bytes=vmem_mib * MIB)


def _smem_spec():
    return pl.BlockSpec(memory_space=pltpu.SMEM)


def _mod_kernel(cc_ref, w_ref, b_ref, o_ref):
    cc = cc_ref[...]
    act = cc * jax.nn.sigmoid(cc)
    o_ref[...] = jnp.dot(act.astype(BF16), w_ref[...].astype(BF16),
                         preferred_element_type=F32) + b_ref[...]


def _modulation(cc, mod_w, mod_b):
    depth, d, d3 = mod_w.shape
    rows = cc.shape[0]
    return pl.pallas_call(
        _mod_kernel,
        grid=(depth, d3 // d),
        in_specs=[pl.BlockSpec((rows, d), lambda i, j: (0, 0)),
                  pl.BlockSpec((None, d, d), lambda i, j: (i, 0, j)),
                  pl.BlockSpec((None, 1, d), lambda i, j: (i, 0, j))],
        out_specs=pl.BlockSpec((None, rows, d), lambda i, j: (i, 0, j)),
        out_shape=jax.ShapeDtypeStruct((depth, rows, d3), F32),
        compiler_params=_params(2, 32),
        name="modulation",
    )(cc, mod_w, mod_b.reshape(depth, 1, d3))


def _norm_modulate(x_ref, mul_ref, add_ref, h_scr):
    x = x_ref[...]
    ms = jnp.mean(x * x, axis=-1, keepdims=True)
    h = x * lax.rsqrt(ms + EPS) * mul_ref[...] + add_ref[...]
    h_scr[...] = h.astype(BF16)


def _ret_inproj_kernel(x_ref, mul_ref, add_ref, w_ref, cos_ref, sin_ref, o_ref, h_scr, *, rope):
    _norm_modulate(x_ref, mul_ref, add_ref, h_scr)
    hb = h_scr[...]
    scale = RET_DK ** -0.5
    for j in range(2 * RET_HEADS):
        cols = slice(j * RET_DK, (j + 1) * RET_DK)
        acc = jnp.dot(hb, w_ref[:, cols], preferred_element_type=F32)
        if j >= RET_HEADS:
            acc = acc * scale
        if rope:
            e, o = acc[:, :HALF_DK], acc[:, HALF_DK:]
            c, s = cos_ref[...], sin_ref[...]
            acc = jnp.concatenate([e * c - o * s, e * s + o * c], axis=1)
        o_ref[:, cols] = acc.astype(BF16)
    for j in range(2 * RET_QK // RET_DV, RET_IN // RET_DV):
        cols = slice(j * RET_DV, (j + 1) * RET_DV)
        o_ref[:, cols] = jnp.dot(hb, w_ref[:, cols], preferred_element_type=F32).astype(BF16)


def _ret_inproj(x, mul, add, w, cos, sin, *, rope, tm):
    b, t, d = x.shape
    kern = functools.partial(_ret_inproj_kernel, rope=rope)
    return pl.pallas_call(
        kern,
        grid=(b, t // tm),
        in_specs=[pl.BlockSpec((None, tm, d), lambda i, j: (i, j, 0)),
                  pl.BlockSpec((None, 1, d), lambda i, j: (i, 0, 0)),
                  pl.BlockSpec((None, 1, d), lambda i, j: (i, 0, 0)),
                  pl.BlockSpec((d, RET_IN), lambda i, j: (0, 0), pipeline_mode=pl.Buffered(1)),
                  pl.BlockSpec((tm, HALF_DK), lambda i, j: (j, 0)),
                  pl.BlockSpec((tm, HALF_DK), lambda i, j: (j, 0))],
        out_specs=pl.BlockSpec((None, tm, RET_IN), lambda i, j: (i, j, 0)),
        out_shape=jax.ShapeDtypeStruct((b, t, RET_IN), BF16),
        scratch_shapes=[pltpu.VMEM((tm, d), BF16)],
        compiler_params=_params(2, 56),
        name="ret_inproj_rope" if rope else "ret_inproj",
    )(x, mul, add, w, cos, sin)


def _store_interleaved(dst_ref, blk, val):
    seg = val.shape[0] // SUBLANES
    for s in range(SUBLANES):
        dst_ref[blk, pl.ds(s, seg, stride=SUBLANES), :] = val[s * seg:(s + 1) * seg, :]


def _lru_inproj_kernel(x_ref, xh_ref, mul_ref, add_ref, w_ref, cw_ref, cb_ref, xc_ref, g_ref,
                       h_scr, hs_scr, ext_scr):
    i = pl.program_id(1)
    has_prev = i > 0
    has_next = i < pl.num_programs(1) - 1
    tm, d = x_ref.shape
    halo_rows = xh_ref.shape[0]

    def norm_modulate(x):
        ms = jnp.mean(x * x, axis=-1, keepdims=True)
        return x * lax.rsqrt(ms + EPS) * mul_ref[...] + add_ref[...]

    h = norm_modulate(x_ref[...])
    for s in range(d // LANES):
        _store_interleaved(hs_scr, s, h[:, s * LANES:(s + 1) * LANES])
    for s in range(d // LANES):
        h_scr[0:tm, s * LANES:(s + 1) * LANES] = hs_scr[s].astype(BF16)
    h_scr[tm:tm + halo_rows, :] = norm_modulate(xh_ref[...]).astype(BF16)

    sub = lax.broadcasted_iota(jnp.int32, (SUBLANES, LRU_BW), 0)
    left = CONV_W // 2
    for j in range(LRU_WIDTH // (2 * LANES)):
        cols = slice(j * 2 * LANES, (j + 1) * 2 * LANES)
        acc = jnp.dot(h_scr[...], w_ref[:, cols], preferred_element_type=F32)
        gcols = slice(LRU_WIDTH + j * 2 * LANES, LRU_WIDTH + (j + 1) * 2 * LANES)
        gacc = jnp.dot(h_scr[0:tm, :], w_ref[:, gcols], preferred_element_type=F32).astype(BF16)
        for half in range(2):
            blk = 2 * j + half
            lanes = slice(half * LANES, (half + 1) * LANES)
            g_ref[blk] = gacc[:, lanes]
            ext = ext_scr.at[half]
            ext[left * SUBLANES:left * SUBLANES + tm, :] = acc[0:tm, lanes]
            for k in range(left):
                src = acc[tm - (left - k) * SUBLANES:tm - (left - k - 1) * SUBLANES, lanes]
                before = jnp.where(has_prev, acc[tm + k:tm + k + 1, lanes], 0.0)
                ext[k * SUBLANES:(k + 1) * SUBLANES, :] = jnp.where(
                    sub == 0, before, pltpu.roll(src, 1, 0))
            for k in range(CONV_W - 1 - left):
                src = acc[k * SUBLANES:(k + 1) * SUBLANES, lanes]
                after = jnp.where(has_next, acc[tm + left + k:tm + left + k + 1, lanes], 0.0)
                ext[(left + k) * SUBLANES + tm:(left + k + 1) * SUBLANES + tm, :] = jnp.where(
                    sub == SUBLANES - 1, after, pltpu.roll(src, SUBLANES - 1, 0))
            xc = cb_ref[blk:blk + 1, :]
            for tap in range(CONV_W):
                xc = xc + ext[tap * SUBLANES:tap * SUBLANES + tm, :] * cw_ref[tap, blk:blk + 1, :]
            xc_ref[blk] = xc


def _conv_halo_rows(x, tm):
    b, t, d = x.shape
    nblk = t // tm
    r = x.reshape(b, nblk, tm, d)
    left = CONV_W // 2
    prev = jnp.concatenate([jnp.zeros((b, 1, left, d), F32), r[:, :-1, tm - left:, :]], axis=1)
    nxt = jnp.concatenate([r[:, 1:, 0:CONV_W - 1 - left, :],
                           jnp.zeros((b, 1, CONV_W - 1 - left, d), F32)], axis=1)
    pad = jnp.zeros((b, nblk, 2 * SUBLANES - (CONV_W - 1), d), F32)
    return jnp.concatenate([prev, nxt, pad], axis=2)


def _lru_inproj(x, mul, add, w, cw, cb, *, tm):
    b, t, d = x.shape
    halo_rows = 2 * SUBLANES
    slab = pl.BlockSpec((None, LRU_BLOCKS, tm, LRU_BW), lambda i, j: (i, 0, j, 0))
    return pl.pallas_call(
        _lru_inproj_kernel,
        grid=(b, t // tm),
        in_specs=[pl.BlockSpec((None, tm, d), lambda i, j: (i, j, 0)),
                  pl.BlockSpec((None, None, halo_rows, d), lambda i, j: (i, j, 0, 0)),
                  pl.BlockSpec((None, 1, d), lambda i, j: (i, 0, 0)),
                  pl.BlockSpec((None, 1, d), lambda i, j: (i, 0, 0)),
                  pl.BlockSpec((d, 2 * LRU_WIDTH), lambda i, j: (0, 0)),
                  pl.BlockSpec((CONV_W, LRU_BLOCKS, LRU_BW), lambda i, j: (0, 0, 0)),
                  pl.BlockSpec((LRU_BLOCKS, LRU_BW), lambda i, j: (0, 0))],
        out_specs=[slab, slab],
        out_shape=[jax.ShapeDtypeStruct((b, LRU_BLOCKS, t, LRU_BW), F32),
                   jax.ShapeDtypeStruct((b, LRU_BLOCKS, t, LRU_BW), BF16)],
        scratch_shapes=[pltpu.VMEM((tm + halo_rows, d), BF16),
                        pltpu.VMEM((d // LANES, tm, LANES), F32),
                        pltpu.VMEM((2, tm + (CONV_W - 1) * SUBLANES, LRU_BW), F32)],
        compiler_params=_params(2, 48),
        name="lru_inproj",
    )(x, _conv_halo_rows(x, tm), mul, add, w, cw, cb)


def _gated_norm_proj(y, w_ref, gp_ref):
    z = jnp.dot(y, w_ref[...], preferred_element_type=F32)
    ms = jnp.mean(z * z, axis=-1, keepdims=True)
    return z * lax.rsqrt(ms + EPS) * gp_ref[...]


def _residual_out(x_ref, zn, o_ref, rows):
    o_ref[rows, :] = x_ref[rows, :] + zn


def _residual_out_interleaved(x_ref, zn, o_ref, z_scr):
    n_slab = zn.shape[1] // LANES
    seg = zn.shape[0] // SUBLANES
    for d in range(n_slab):
        z_scr[d] = zn[:, d * LANES:(d + 1) * LANES]
    for d in range(n_slab):
        cols = slice(d * LANES, (d + 1) * LANES)
        for s in range(SUBLANES):
            rows = slice(s * seg, (s + 1) * seg)
            o_ref[rows, cols] = x_ref[rows, cols] + z_scr[d, pl.ds(s, seg, stride=SUBLANES), :]


def _row_scale(x, dec):
    n = x.shape[1] // LANES
    return jnp.concatenate([x[:, i * LANES:(i + 1) * LANES] * dec for i in range(n)], axis=1)


def _chunk_pos():
    return lax.broadcasted_iota(jnp.int32, (RET_CHUNK, LANES), 0).astype(F32)


def _ret_bwd_kernel(lg_ref, k_ref, v_ref, s0_ref, sb_ref, sfin_ref, s_scr, dec_scr, *, nch):
    c = pl.program_id(1)

    @pl.when(c == 0)
    def _():
        s_scr[...] = s0_ref[...]
        pos = _chunk_pos()
        for h in range(RET_HEADS):
            dec_scr[h] = jnp.exp(lg_ref[1, h] * pos).astype(BF16)

    for ci in reversed(range(nch)):
        rows = slice(ci * RET_CHUNK, (ci + 1) * RET_CHUNK)
        for h in range(RET_HEADS):
            k = k_ref[rows, h * RET_DK:(h + 1) * RET_DK]
            v = v_ref[rows, h * RET_DV:(h + 1) * RET_DV]
            st = s_scr[h]
            sb_ref[ci, h] = st.astype(BF16)
            kd = _row_scale(k, dec_scr[h])
            kv = lax.dot_general(kd, v, (((0,), (0,)), ((), ())), preferred_element_type=F32)
            s_scr[h] = st * jnp.exp(lg_ref[1, h] * RET_CHUNK) + kv

    @pl.when(c == pl.num_programs(1) - 1)
    def _():
        sfin_ref[...] = s_scr[...]


def _ret_fwd_kernel(lg_ref, q_ref, k_ref, v_ref, g_ref, sb_ref, s0_ref, w_ref, x_ref, gp_ref,
                    xo_ref, sfin_ref, s_scr, m_scr, dec_scr, y_scr, *, nch):
    c = pl.program_id(1)

    @pl.when(c == 0)
    def _():
        s_scr[...] = s0_ref[...]
        ri = lax.broadcasted_iota(jnp.int32, (RET_CHUNK, RET_CHUNK), 0)
        ci = lax.broadcasted_iota(jnp.int32, (RET_CHUNK, RET_CHUNK), 1)
        rel = (ri - ci).astype(F32)
        for h in range(RET_HEADS):
            mf = jnp.where(rel >= 0, jnp.exp(lg_ref[0, h] * jnp.maximum(rel, 0.0)), 0.0)
            mb = jnp.where(rel <= 0, jnp.exp(lg_ref[1, h] * jnp.maximum(-rel, 0.0)), 0.0)
            m_scr[h] = mf + mb
        pos = _chunk_pos()
        for h in range(RET_HEADS):
            dec_scr[h, 0] = jnp.exp(lg_ref[0, h] * (pos + 1.0)).astype(BF16)
            dec_scr[h, 1] = jnp.exp(lg_ref[1, h] * (RET_CHUNK - pos)).astype(BF16)
            dec_scr[h, 2] = jnp.exp(lg_ref[0, h] * (RET_CHUNK - 1.0 - pos)).astype(BF16)

    for ci, h in [(ci, h) for ci in range(nch) for h in range(RET_HEADS)]:
        rows = slice(ci * RET_CHUNK, (ci + 1) * RET_CHUNK)
        vcols = slice(h * RET_DV, (h + 1) * RET_DV)
        q = q_ref[rows, h * RET_DK:(h + 1) * RET_DK]
        k = k_ref[rows, h * RET_DK:(h + 1) * RET_DK]
        v = v_ref[rows, vcols]
        st = s_scr[h]
        s = lax.dot_general(q, k, (((1,), (1,)), ((), ())), preferred_element_type=F32)
        o = jnp.dot((s * m_scr[h]).astype(BF16), v, preferred_element_type=F32)
        o = jnp.dot(_row_scale(q, dec_scr[h, 0]), st.astype(BF16), preferred_element_type=F32) + o
        o = jnp.dot(_row_scale(q, dec_scr[h, 1]), sb_ref[ci, h], preferred_element_type=F32) + o
        kv = lax.dot_general(_row_scale(k, dec_scr[h, 2]), v, (((0,), (0,)), ((), ())),
                             preferred_element_type=F32)
        s_scr[h] = st * jnp.exp(lg_ref[0, h] * RET_CHUNK) + kv
        dlt = o - jnp.mean(o, axis=-1, keepdims=True)
        var = jnp.mean(dlt * dlt, axis=-1, keepdims=True)
        gh = g_ref[rows, vcols]
        y_scr[rows, vcols] = ((dlt * lax.rsqrt(var + EPS)).astype(BF16)
                              * (gh * (jnp.tanh(gh) + 1.0)))
        if h == RET_HEADS - 1:
            _residual_out(x_ref, _gated_norm_proj(y_scr[rows, :], w_ref, gp_ref), xo_ref, rows)

    @pl.when(c == pl.num_programs(1) - 1)
    def _():
        sfin_ref[...] = s_scr[...]


def _state_spec():
    return pl.BlockSpec((None, RET_HEADS, RET_DK, RET_DV), lambda i, j: (i, 0, 0, 0))


def _step_chunks(n_chunks, wanted):
    return wanted if n_chunks % wanted == 0 else 1


def _ret_bwd(p, lg, s0):
    b, t, _ = p.shape
    nc = t // RET_CHUNK
    nch = _step_chunks(nc, RET_BWD_STEP_CHUNKS)
    ns = nc // nch
    rev = lambda blk: (lambda i, j: (i, ns - 1 - j, blk))
    hist = pl.BlockSpec((None, nch, RET_HEADS, RET_DK, RET_DV),
                        lambda i, j: (i, ns - 1 - j, 0, 0, 0))
    return pl.pallas_call(
        functools.partial(_ret_bwd_kernel, nch=nch),
        grid=(b, ns),
        in_specs=[_smem_spec(),
                  pl.BlockSpec((None, nch * RET_CHUNK, RET_QK), rev(1)),
                  pl.BlockSpec((None, nch * RET_CHUNK, RET_V), rev(1)),
                  _state_spec()],
        out_specs=[hist, _state_spec()],
        out_shape=[jax.ShapeDtypeStruct((b, nc, RET_HEADS, RET_DK, RET_DV), BF16),
                   jax.ShapeDtypeStruct(s0.shape, F32)],
        scratch_shapes=[pltpu.VMEM((RET_HEADS, RET_DK, RET_DV), F32),
                        pltpu.VMEM((RET_HEADS, RET_CHUNK, LANES), BF16)],
        compiler_params=_params(2, 48),
        name="ret_bwd",
    )(lg, p, p, s0)


def _ret_fwd(p, sb, lg, s0, w_out, x, gp):
    b, t, _ = p.shape
    d = x.shape[-1]
    nc = t // RET_CHUNK
    nch = _step_chunks(nc, RET_FWD_STEP_CHUNKS)
    rows = nch * RET_CHUNK
    fwd = lambda blk: (lambda i, j: (i, j, blk))
    resid = pl.BlockSpec((None, rows, d), fwd(0))
    return pl.pallas_call(
        functools.partial(_ret_fwd_kernel, nch=nch),
        grid=(b, nc // nch),
        in_specs=[_smem_spec(),
                  pl.BlockSpec((None, rows, RET_QK), fwd(0)),
                  pl.BlockSpec((None, rows, RET_QK), fwd(1)),
                  pl.BlockSpec((None, rows, RET_V), fwd(1)),
                  pl.BlockSpec((None, rows, RET_V), fwd(2)),
                  pl.BlockSpec((None, nch, RET_HEADS, RET_DK, RET_DV),
                               lambda i, j: (i, j, 0, 0, 0)),
                  _state_spec(),
                  pl.BlockSpec((RET_V, d), lambda i, j: (0, 0)),
                  resid,
                  pl.BlockSpec((None, 1, d), lambda i, j: (i, 0, 0))],
        out_specs=[resid, _state_spec()],
        out_shape=[jax.ShapeDtypeStruct((b, t, d), F32),
                   jax.ShapeDtypeStruct(s0.shape, F32)],
        scratch_shapes=[pltpu.VMEM((RET_HEADS, RET_DK, RET_DV), F32),
                        pltpu.VMEM((RET_HEADS, RET_CHUNK, RET_CHUNK), F32),
                        pltpu.VMEM((RET_HEADS, 3, RET_CHUNK, LANES), BF16),
                        pltpu.VMEM((rows, RET_V), BF16)],
        compiler_params=_params(2, 56),
        name="ret_fwd",
    )(lg, p, p, p, p, sb, s0, w_out, x, gp)


def _lru_coeffs(xc_ref, wg_ref, bg_ref, kh_ref, a_scr, u_scr):
    for nb in range(LRU_BLOCKS):
        xc = xc_ref[nb]
        res = jnp.dot(xc.astype(BF16), wg_ref[nb], preferred_element_type=F32)
        t_r = jnp.tanh(res[:, :LRU_BW] + bg_ref[0, nb:nb + 1, :])
        t_g = jnp.tanh(res[:, LRU_BW:] + bg_ref[1, nb:nb + 1, :])
        kh = kh_ref[nb:nb + 1, :]
        p = jnp.tanh(t_r * kh + kh)
        w = 1.0 / (1.0 + p)
        a_scr[nb] = (1.0 - p) * w
        sqrt_p = jnp.where(p > 0.0, p * lax.rsqrt(p), 0.0)
        u_scr[nb] = sqrt_p * w * (t_g + 1.0) * xc


def _lru_scan(a_scr, u_scr, dst_ref, carry_scr, tt, reverse):
    n = tt // SUBLANES
    shape = (LRU_BLOCKS, SUBLANES, LRU_BW)
    sub = lax.broadcasted_iota(jnp.int32, shape, 1)
    steps = [slice(g * SUBLANES, (g + 1) * SUBLANES) for g in range(n)]
    if reverse:
        steps = steps[::-1]

    a_seg = a_scr[:, steps[0], :]
    h_seg = u_scr[:, steps[0], :]
    for rows in steps[1:]:
        a = a_scr[:, rows, :]
        h_seg = a * h_seg + u_scr[:, rows, :]
        a_seg = a_seg * a
    for d in (1, 2, 4):
        ok = (sub < SUBLANES - d) if reverse else (sub >= d)
        shift = SUBLANES - d if reverse else d
        a_s = jnp.where(ok, pltpu.roll(a_seg, shift, 1), 1.0)
        h_s = jnp.where(ok, pltpu.roll(h_seg, shift, 1), 0.0)
        h_seg = a_seg * h_s + h_seg
        a_seg = a_seg * a_s
    carry = carry_scr[...]
    h_end = a_seg * carry + h_seg
    if reverse:
        h_in = jnp.where(sub == SUBLANES - 1, carry, pltpu.roll(h_end, SUBLANES - 1, 1))
        last = h_end[:, 0:1, :]
    else:
        h_in = jnp.where(sub == 0, carry, pltpu.roll(h_end, 1, 1))
        last = h_end[:, SUBLANES - 1:SUBLANES, :]
    carry_scr[...] = jnp.broadcast_to(last, shape)

    h = h_in
    for rows in steps:
        h = a_scr[:, rows, :] * h + u_scr[:, rows, :]
        dst_ref[:, rows, :] = h


def _lru_bwd_kernel(xc_ref, wg_ref, bg_ref, kh_ref, h0_ref, hb_ref, hfin_ref,
                    a_scr, u_scr, carry_scr, *, tt):
    i = pl.program_id(1)

    @pl.when(i == 0)
    def _():
        carry_scr[...] = h0_ref[...]

    _lru_coeffs(xc_ref, wg_ref, bg_ref, kh_ref, a_scr, u_scr)
    _lru_scan(a_scr, u_scr, hb_ref, carry_scr, tt, True)

    @pl.when(i == pl.num_programs(1) - 1)
    def _():
        hfin_ref[...] = carry_scr[...]


def _lru_fwd_kernel(xc_ref, wg_ref, bg_ref, kh_ref, h0_ref, hb_ref, g_ref, w_ref, x_ref, gp_ref,
                    xo_ref, hfin_ref, a_scr, u_scr, carry_scr, y_scr, z_scr, *, tt):
    i = pl.program_id(1)

    @pl.when(i == 0)
    def _():
        carry_scr[...] = h0_ref[...]

    _lru_coeffs(xc_ref, wg_ref, bg_ref, kh_ref, a_scr, u_scr)
    _lru_scan(a_scr, u_scr, u_scr, carry_scr, tt, False)
    for nb in range(LRU_BLOCKS):
        gh = g_ref[nb]
        y = (u_scr[nb] + hb_ref[nb]).astype(BF16) * (gh * (jnp.tanh(gh) + 1.0))
        y_scr[:, nb * LRU_BW:(nb + 1) * LRU_BW] = y
    _residual_out_interleaved(x_ref, _gated_norm_proj(y_scr[...], w_ref, gp_ref), xo_ref, z_scr)

    @pl.when(i == pl.num_programs(1) - 1)
    def _():
        hfin_ref[...] = carry_scr[...]


def _lru_specs(b, t, tt, reverse):
    nblk = t // tt
    tidx = (lambda j: nblk - 1 - j) if reverse else (lambda j: j)
    slab = pl.BlockSpec((None, LRU_BLOCKS, tt, LRU_BW), lambda i, j: (i, 0, tidx(j), 0))
    state = pl.BlockSpec((None, LRU_BLOCKS, SUBLANES, LRU_BW), lambda i, j: (i, 0, 0, 0))
    gate = [pl.BlockSpec((LRU_BLOCKS, LRU_BW, 2 * LRU_BW), lambda i, j: (0, 0, 0)),
            pl.BlockSpec((2, LRU_BLOCKS, LRU_BW), lambda i, j: (0, 0, 0)),
            pl.BlockSpec((LRU_BLOCKS, LRU_BW), lambda i, j: (0, 0))]
    return nblk, slab, state, gate


def _lru_bwd(xc, wg, bg, kh, h0, *, tt):
    b, _, t, _ = xc.shape
    nblk, slab, state, gate = _lru_specs(b, t, tt, True)
    tile = (LRU_BLOCKS, tt, LRU_BW)
    return pl.pallas_call(
        functools.partial(_lru_bwd_kernel, tt=tt),
        grid=(b, nblk),
        in_specs=[slab] + gate + [state],
        out_specs=[slab, state],
        out_shape=[jax.ShapeDtypeStruct(xc.shape, F32), jax.ShapeDtypeStruct(h0.shape, F32)],
        scratch_shapes=[pltpu.VMEM(tile, F32), pltpu.VMEM(tile, F32),
                        pltpu.VMEM((LRU_BLOCKS, SUBLANES, LRU_BW), F32)],
        compiler_params=_params(2, 48),
        name="lru_bwd",
    )(xc, wg, bg, kh, h0)


def _lru_fwd(xc, wg, bg, kh, h0, hb, g, w_out, x, gp, *, tt):
    b, _, t, _ = xc.shape
    d = x.shape[-1]
    nblk, slab, state, gate = _lru_specs(b, t, tt, False)
    tile = (LRU_BLOCKS, tt, LRU_BW)
    resid = pl.BlockSpec((None, tt, d), lambda i, j: (i, j, 0))
    return pl.pallas_call(
        functools.partial(_lru_fwd_kernel, tt=tt),
        grid=(b, nblk),
        in_specs=[slab] + gate + [state, slab, slab,
                                  pl.BlockSpec((LRU_WIDTH, d), lambda i, j: (0, 0)),
                                  resid,
                                  pl.BlockSpec((None, 1, d), lambda i, j: (i, 0, 0))],
        out_specs=[resid, state],
        out_shape=[jax.ShapeDtypeStruct((b, t, d), F32),
                   jax.ShapeDtypeStruct(h0.shape, F32)],
        scratch_shapes=[pltpu.VMEM(tile, F32), pltpu.VMEM(tile, F32),
                        pltpu.VMEM((LRU_BLOCKS, SUBLANES, LRU_BW), F32),
                        pltpu.VMEM((tt, LRU_WIDTH), BF16),
                        pltpu.VMEM((d // LANES, tt, LANES), F32)],
        compiler_params=_params(2, 56),
        name="lru_fwd",
    )(xc, wg, bg, kh, h0, hb, g, w_out, x, gp)


def _rope_tables(n_tok):
    n_rows = n_tok // GRID_W
    row = jnp.repeat(jnp.arange(n_rows, dtype=F32), GRID_W)
    col = jnp.tile(jnp.arange(GRID_W, dtype=F32), n_rows)
    n_freq = RET_DK // 4
    inv = ROPE_BASE ** (-jnp.arange(n_freq, dtype=F32) / n_freq)
    ang = jnp.concatenate([row[:, None] * inv, col[:, None] * inv], axis=-1)
    return jnp.cos(ang), jnp.sin(ang)


def _ret_w_prep_kernel(w_ref, m_ref, o_ref):
    o_ref[...] = jnp.dot(w_ref[...].astype(BF16), m_ref[...],
                         preferred_element_type=F32).astype(BF16)


def _ret_w_prep(w_in_all, layer):
    d = w_in_all.shape[1]
    src = np.arange(RET_DK)
    perm = np.zeros((RET_DK, RET_DK), np.float32)
    perm[src, (src % 2) * HALF_DK + src // 2] = 1.0
    heads = RET_DV // RET_DK
    cols = heads * RET_DK
    eye = np.eye(cols, dtype=np.float32)
    mats = jnp.asarray(np.stack([np.kron(np.eye(heads, dtype=np.float32), perm), eye, 0.5 * eye]),
                       BF16)
    qk_blocks = 2 * RET_QK // cols
    v_blocks = RET_V // cols

    def which(j):
        return (j >= qk_blocks).astype(jnp.int32) + (j >= qk_blocks + v_blocks).astype(jnp.int32)

    return pl.pallas_call(
        _ret_w_prep_kernel,
        grid=(RET_IN // cols,),
        in_specs=[pl.BlockSpec((None, d, cols), lambda j: (layer, 0, j)),
                  pl.BlockSpec((None, cols, cols), lambda j: (which(j), 0, 0))],
        out_specs=pl.BlockSpec((d, cols), lambda j: (0, j)),
        out_shape=jax.ShapeDtypeStruct((d, RET_IN), BF16),
        compiler_params=_params(1, 32),
        name="ret_w_prep",
    )(w_in_all, mats)


def _retention_layer(x, s_ctx, mods, cos, sin, w_in_all, layer, log_decay, gn_g, w_out):
    (mul_l, add_l, gp_l), (mul_c, add_c, gp_c) = mods
    b = x.shape[0]
    lc = s_ctx.shape[1]
    w = _ret_w_prep(w_in_all, layer)
    lg =-jnp.abs(log_decay.astype(F32))
    w_o = (gn_g.astype(F32)[:, None] * w_out).astype(BF16)

    p_l = _ret_inproj(x, mul_l, add_l, w, cos, sin, rope=True, tm=RET_ROW_TILE)
    p_c = _ret_inproj(s_ctx, mul_c, add_c, w, cos[:lc], sin[:lc], rope=False, tm=lc)
    zero = jnp.zeros((b, RET_HEADS, RET_DK, RET_DV), F32)
    sb_c, sb_fin = _ret_bwd(p_c, lg, zero)
    sb_l, _ = _ret_bwd(p_l, lg, sb_fin)
    s_ctx, sf_c = _ret_fwd(p_c, sb_c, lg, zero, w_o, s_ctx, gp_c)
    x, _ = _ret_fwd(p_l, sb_l, lg, sf_c, w_o, x, gp_l)
    return x, s_ctx


def _lru_layer(x, s_ctx, mods, w_in, conv_w, conv_b, w_a, b_a, w_x, b_x, lam, w_out):
    (mul_l, add_l, gp_l), (mul_c, add_c, gp_c) = mods
    b = x.shape[0]
    lc = s_ctx.shape[1]
    w = jnp.concatenate([w_in[:, :LRU_WIDTH], 0.5 * w_in[:, LRU_WIDTH:]], axis=1).astype(BF16)
    w_o = w_out.astype(BF16)
    cw = conv_w.astype(F32).reshape(CONV_W, LRU_BLOCKS, LRU_BW)
    cb = conv_b.astype(F32).reshape(LRU_BLOCKS, LRU_BW)
    kh = (0.25 * LRU_C) * jax.nn.softplus(-lam.astype(F32))

    def gate_params(d):
        wg = (0.5 * jnp.concatenate([w_a[d], w_x[d]], axis=-1)).astype(BF16)
        bg = (0.5 * jnp.stack([b_a[d], b_x[d]])).astype(F32).reshape(2, LRU_BLOCKS, LRU_BW)
        return wg, bg, kh[d].reshape(LRU_BLOCKS, LRU_BW)

    xc_l, g_l = _lru_inproj(x, mul_l, add_l, w, cw, cb, tm=ROW_TILE)
    xc_c, g_c = _lru_inproj(s_ctx, mul_c, add_c, w, cw, cb, tm=lc)
    zero = jnp.zeros((b, LRU_BLOCKS, SUBLANES, LRU_BW), F32)
    hb_c, hb_fin = _lru_bwd(xc_c, *gate_params(1), zero, tt=lc)
    hb_l, _ = _lru_bwd(xc_l, *gate_params(1), hb_fin, tt=ROW_TILE)
    s_ctx, hf_fin = _lru_fwd(xc_c, *gate_params(0), zero, hb_c, g_c, w_o, s_ctx, gp_c, tt=lc)
    x, _ = _lru_fwd(xc_l, *gate_params(0), hf_fin, hb_l, g_l, w_o, x, gp_l, tt=ROW_TILE)
    return x, s_ctx


def kernel(x, c, ctx, c_ctx, mod_w, mod_b, norm_pre, norm_post, ret_w_in, ret_log_decay, ret_gn,
           ret_w_out, lru_w_in, lru_conv_w, lru_conv_b, lru_w_a, lru_b_a, lru_w_x, lru_b_x,
           lru_lambda, lru_w_out):
    b, n_tok, d = x.shape
    depth = mod_w.shape[0]
    assert n_tok % ROW_TILE == 0 and n_tok % RET_ROW_TILE == 0 and n_tok % RET_CHUNK == 0
    assert ctx.shape[1] % RET_CHUNK == 0 and ctx.shape[1] <= ROW_TILE
    cos, sin = _rope_tables(n_tok)

    rows = -(-(b + 1) // SUBLANES) * SUBLANES
    cc = jnp.concatenate([c, c_ctx[None, :], jnp.zeros((rows - b - 1, d), F32)], axis=0)
    mod = _modulation(cc, mod_w, mod_b)

    s_ctx = ctx
    for i in range(depth):
        shift, scale, gate = jnp.split(mod[i], 3, axis=-1)
        mods = []
        for sel in (slice(0, b), slice(b, b + 1)):
            mul = jnp.broadcast_to(norm_pre[i] * (1.0 + scale[sel]), (b, d))[:, None, :]
            add = jnp.broadcast_to(shift[sel], (b, d))[:, None, :]
            gp = jnp.broadcast_to(gate[sel] * norm_post[i], (b, d))[:, None, :]
            mods.append((mul, add, gp))
        j = i // 2
        if i % 2 == 0:
            x, s_ctx = _retention_layer(x, s_ctx, mods, cos, sin, ret_w_in, j, ret_log_decay[j],
                                        ret_gn[j], ret_w_out[j])
        else:
            x, s_ctx = _lru_layer(x, s_ctx, mods, lru_w_in[j], lru_conv_w[j], lru_conv_b[j],
                                  lru_w_a[j], lru_b_a[j], lru_w_x[j], lru_b_x[j],
                                  lru_lambda[j], lru_w_out[j])
    return x
```

```python
import functools

import jax
import jax.numpy as jnp
import numpy as np
from jax import lax
from jax.experimental import pallas as pl
from jax.experimental.pallas import tpu as pltpu

F32 = jnp.float32
BF16 = jnp.bfloat16

EPS = 1e-6
GRID_W = 64
ROPE_BASE = 10000.0

RET_HEADS = 4
RET_DK = 256
RET_DV = 512
RET_QK = RET_HEADS * RET_DK
RET_V = RET_HEADS * RET_DV
RET_IN = 2 * RET_QK + 2 * RET_V
HALF_DK = RET_DK // 2

LRU_BLOCKS = 10
LRU_BW = 128
LRU_WIDTH = LRU_BLOCKS * LRU_BW
CONV_W = 4
LRU_C = 8.0

LANES = 128
SUBLANES = 8
MIB = 1024 * 1024

RET_CHUNK = 256
RET_FWD_STEP_CHUNKS = 2
ROW_TILE = 512


def _params(n_axes, vmem_mib):
    return pltpu.CompilerParams(
        dimension_semantics=("arbitrary",) * n_axes,
        vmem_limit_bytes=vmem_mib * MIB)


def _smem_spec():
    return pl.BlockSpec(memory_space=pltpu.SMEM)


def _mod_kernel(cc_ref, w_ref, b_ref, o_ref):
    cc = cc_ref[...]
    act = cc * jax.nn.sigmoid(cc)
    o_ref[...] = jnp.dot(act.astype(BF16), w_ref[...].astype(BF16),
                         preferred_element_type=F32) + b_ref[...]


def _modulation(cc, mod_w, mod_b):
    depth, d, d3 = mod_w.shape
    rows = cc.shape[0]
    return pl.pallas_call(
        _mod_kernel,
        grid=(depth, d3 // d),
        in_specs=[pl.BlockSpec((rows, d), lambda i, j: (0, 0)),
                  pl.BlockSpec((None, d, d), lambda i, j: (i, 0, j)),
                  pl.BlockSpec((None, 1, d), lambda i, j: (i, 0, j))],
        out_specs=pl.BlockSpec((None, rows, d), lambda i, j: (i, 0, j)),
        out_shape=jax.ShapeDtypeStruct((depth, rows, d3), F32),
        compiler_params=_params(2, 32),
        name="modulation",
    )(cc, mod_w, mod_b.reshape(depth, 1, d3))


def _norm_modulate(x_ref, mul_ref, add_ref, h_scr):
    x = x_ref[...]
    ms = jnp.mean(x * x, axis=-1, keepdims=True)
    h = x * lax.rsqrt(ms + EPS) * mul_ref[...] + add_ref[...]
    h_scr[...] = h.astype(BF16)


def _ret_inproj_kernel(lg_ref, x_ref, mul_ref, add_ref, w_ref, cos_ref, sin_ref, s0_ref,
                       o_ref, sb_ref, sfin_ref, h_scr, s_scr, dec_scr, *, rope, nch):
    t = pl.program_id(1)

    @pl.when(t == 0)
    def _():
        s_scr[...] = s0_ref[...]
        pos = _chunk_pos()
        for h in range(RET_HEADS):
            dec_scr[h] = jnp.exp(lg_ref[1, h] * pos).astype(BF16)

    _norm_modulate(x_ref, mul_ref, add_ref, h_scr)
    hb = h_scr[...]
    scale = RET_DK ** -0.5
    for j in range(2 * RET_HEADS):
        cols = slice(j * RET_DK, (j + 1) * RET_DK)
        acc = jnp.dot(hb, w_ref[:, cols], preferred_element_type=F32)
        if j >= RET_HEADS:
            acc = acc * scale
        if rope:
            e, o = acc[:, :HALF_DK], acc[:, HALF_DK:]
            c, s = cos_ref[...], sin_ref[...]
            acc = jnp.concatenate([e * c - o * s, e * s + o * c], axis=1)
        o_ref[:, cols] = acc.astype(BF16)
    for j in range(2 * RET_QK // RET_DV, RET_IN // RET_DV):
        cols = slice(j * RET_DV, (j + 1) * RET_DV)
        o_ref[:, cols] = jnp.dot(hb, w_ref[:, cols], preferred_element_type=F32).astype(BF16)

    for ci in reversed(range(nch)):
        rows = slice(ci * RET_CHUNK, (ci + 1) * RET_CHUNK)
        for h in range(RET_HEADS):
            k = o_ref[rows, RET_QK + h * RET_DK:RET_QK + (h + 1) * RET_DK]
            v = o_ref[rows, 2 * RET_QK + h * RET_DV:2 * RET_QK + (h + 1) * RET_DV]
            st = s_scr[h]
            sb_ref[ci, h] = st.astype(BF16)
            kv = lax.dot_general(_row_scale(k, dec_scr[h]), v, (((0,), (0,)), ((), ())),
                                 preferred_element_type=F32)
            s_scr[h] = st * jnp.exp(lg_ref[1, h] * RET_CHUNK) + kv

    @pl.when(t == pl.num_programs(1) - 1)
    def _():
        sfin_ref[...] = s_scr[...]


def _ret_inproj(x, mul, add, w, cos, sin, lg, s0, *, rope, tm):
    b, t, d = x.shape
    nt = t // tm
    nch = tm // RET_CHUNK
    kern = functools.partial(_ret_inproj_kernel, rope=rope, nch=nch)
    rev = lambda i, j: (i, nt - 1 - j, 0)
    return pl.pallas_call(
        kern,
        grid=(b, nt),
        in_specs=[_smem_spec(),
                  pl.BlockSpec((None, tm, d), rev),
                  pl.BlockSpec((None, 1, d), lambda i, j: (i, 0, 0)),
                  pl.BlockSpec((None, 1, d), lambda i, j: (i, 0, 0)),
                  pl.BlockSpec((d, RET_IN), lambda i, j: (0, 0), pipeline_mode=pl.Buffered(1)),
                  pl.BlockSpec((tm, HALF_DK), lambda i, j: (nt - 1 - j, 0)),
                  pl.BlockSpec((tm, HALF_DK), lambda i, j: (nt - 1 - j, 0)),
                  _state_spec()],
        out_specs=[pl.BlockSpec((None, tm, RET_IN), rev),
                   pl.BlockSpec((None, nch, RET_HEADS, RET_DK, RET_DV),
                                lambda i, j: (i, nt - 1 - j, 0, 0, 0)),
                   _state_spec()],
        out_shape=[jax.ShapeDtypeStruct((b, t, RET_IN), BF16),
                   jax.ShapeDtypeStruct((b, t // RET_CHUNK, RET_HEADS, RET_DK, RET_DV), BF16),
                   jax.ShapeDtypeStruct(s0.shape, F32)],
        scratch_shapes=[pltpu.VMEM((tm, d), BF16),
                        pltpu.VMEM((RET_HEADS, RET_DK, RET_DV), F32),
                        pltpu.VMEM((RET_HEADS, RET_CHUNK, LANES), BF16)],
        compiler_params=_params(2, 56),
        name="ret_inproj_rope" if rope else "ret_inproj",
    )(lg, x, mul, add, w, cos, sin, s0)


def _store_interleaved(dst_ref, blk, val):
    seg = val.shape[0] // SUBLANES
    for s in range(SUBLANES):
        dst_ref[blk, pl.ds(s, seg, stride=SUBLANES), :] = val[s * seg:(s + 1) * seg, :]


def _lru_inproj_kernel(x_ref, xh_ref, mul_ref, add_ref, w_ref, cw_ref, cb_ref, xc_ref, g_ref,
                       h_scr, hs_scr, ext_scr):
    i = pl.program_id(1)
    has_prev = i > 0
    has_next = i < pl.num_programs(1) - 1
    tm, d = x_ref.shape
    halo_rows = xh_ref.shape[0]

    def norm_modulate(x):
        ms = jnp.mean(x * x, axis=-1, keepdims=True)
        return x * lax.rsqrt(ms + EPS) * mul_ref[...] + add_ref[...]

    h = norm_modulate(x_ref[...])
    for s in range(d // LANES):
        _store_interleaved(hs_scr, s, h[:, s * LANES:(s + 1) * LANES])
    for s in range(d // LANES):
        h_scr[0:tm, s * LANES:(s + 1) * LANES] = hs_scr[s].astype(BF16)
    h_scr[tm:tm + halo_rows, :] = norm_modulate(xh_ref[...]).astype(BF16)

    sub = lax.broadcasted_iota(jnp.int32, (SUBLANES, LRU_BW), 0)
    left = CONV_W // 2
    for j in range(LRU_WIDTH // (2 * LANES)):
        cols = slice(j * 2 * LANES, (j + 1) * 2 * LANES)
        acc = jnp.dot(h_scr[...], w_ref[:, cols], preferred_element_type=F32)
        gcols = slice(LRU_WIDTH + j * 2 * LANES, LRU_WIDTH + (j + 1) * 2 * LANES)
        gacc = jnp.dot(h_scr[0:tm, :], w_ref[:, gcols], preferred_element_type=F32).astype(BF16)
        for half in range(2):
            blk = 2 * j + half
            lanes = slice(half * LANES, (half + 1) * LANES)
            g_ref[blk] = gacc[:, lanes]
            ext = ext_scr.at[half]
            ext[left * SUBLANES:left * SUBLANES + tm, :] = acc[0:tm, lanes]
            for k in range(left):
                src = acc[tm - (left - k) * SUBLANES:tm - (left - k - 1) * SUBLANES, lanes]
                before = jnp.where(has_prev, acc[tm + k:tm + k + 1, lanes], 0.0)
                ext[k * SUBLANES:(k + 1) * SUBLANES, :] = jnp.where(
                    sub == 0, before, pltpu.roll(src, 1, 0))
            for k in range(CONV_W - 1 - left):
                src = acc[k * SUBLANES:(k + 1) * SUBLANES, lanes]
                after = jnp.where(has_next, acc[tm + left + k:tm + left + k + 1, lanes], 0.0)
                ext[(left + k) * SUBLANES + tm:(left + k + 1) * SUBLANES + tm, :] = jnp.where(
                    sub == SUBLANES - 1, after, pltpu.roll(src, SUBLANES - 1, 0))
            xc = cb_ref[blk:blk + 1, :]
            for tap in range(CONV_W):
                xc = xc + ext[tap * SUBLANES:tap * SUBLANES + tm, :] * cw_ref[tap, blk:blk + 1, :]
            xc_ref[blk] = xc


def _conv_halo_rows(x, tm):
    b, t, d = x.shape
    nblk = t // tm
    r = x.reshape(b, nblk, tm, d)
    left = CONV_W // 2
    prev = jnp.concatenate([jnp.zeros((b, 1, left, d), F32), r[:, :-1, tm - left:, :]], axis=1)
    nxt = jnp.concatenate([r[:, 1:, 0:CONV_W - 1 - left, :],
                           jnp.zeros((b, 1, CONV_W - 1 - left, d), F32)], axis=1)
    pad = jnp.zeros((b, nblk, 2 * SUBLANES - (CONV_W - 1), d), F32)
    return jnp.concatenate([prev, nxt, pad], axis=2)


def _lru_inproj(x, mul, add, w, cw, cb, *, tm):
    b, t, d = x.shape
    halo_rows = 2 * SUBLANES
    slab = pl.BlockSpec((None, LRU_BLOCKS, tm, LRU_BW), lambda i, j: (i, 0, j, 0))
    return pl.pallas_call(
        _lru_inproj_kernel,
        grid=(b, t // tm),
        in_specs=[pl.BlockSpec((None, tm, d), lambda i, j: (i, j, 0)),
                  pl.BlockSpec((None, None, halo_rows, d), lambda i, j: (i, j, 0, 0)),
                  pl.BlockSpec((None, 1, d), lambda i, j: (i, 0, 0)),
                  pl.BlockSpec((None, 1, d), lambda i, j: (i, 0, 0)),
                  pl.BlockSpec((d, 2 * LRU_WIDTH), lambda i, j: (0, 0)),
                  pl.BlockSpec((CONV_W, LRU_BLOCKS, LRU_BW), lambda i, j: (0, 0, 0)),
                  pl.BlockSpec((LRU_BLOCKS, LRU_BW), lambda i, j: (0, 0))],
        out_specs=[slab, slab],
        out_shape=[jax.ShapeDtypeStruct((b, LRU_BLOCKS, t, LRU_BW), F32),
                   jax.ShapeDtypeStruct((b, LRU_BLOCKS, t, LRU_BW), BF16)],
        scratch_shapes=[pltpu.VMEM((tm + halo_rows, d), BF16),
                        pltpu.VMEM((d // LANES, tm, LANES), F32),
                        pltpu.VMEM((2, tm + (CONV_W - 1) * SUBLANES, LRU_BW), F32)],
        compiler_params=_params(2, 48),
        name="lru_inproj",
    )(x, _conv_halo_rows(x, tm), mul, add, w, cw, cb)


def _gated_norm_proj(y, w_ref, gp_ref):
    z = jnp.dot(y, w_ref[...], preferred_element_type=F32)
    ms = jnp.mean(z * z, axis=-1, keepdims=True)
    return z * lax.rsqrt(ms + EPS) * gp_ref[...]


def _residual_out(x_ref, zn, o_ref, rows):
    o_ref[rows, :] = x_ref[rows, :] + zn


def _residual_out_interleaved(x_ref, zn, o_ref, z_scr):
    n_slab = zn.shape[1] // LANES
    seg = zn.shape[0] // SUBLANES
    for d in range(n_slab):
        z_scr[d] = zn[:, d * LANES:(d + 1) * LANES]
    for d in range(n_slab):
        cols = slice(d * LANES, (d + 1) * LANES)
        for s in range(SUBLANES):
            rows = slice(s * seg, (s + 1) * seg)
            o_ref[rows, cols] = x_ref[rows, cols] + z_scr[d, pl.ds(s, seg, stride=SUBLANES), :]


def _row_scale(x, dec):
    n = x.shape[1] // LANES
    return jnp.concatenate([x[:, i * LANES:(i + 1) * LANES] * dec for i in range(n)], axis=1)


def _chunk_pos():
    return lax.broadcasted_iota(jnp.int32, (RET_CHUNK, LANES), 0).astype(F32)


def _ret_fwd_kernel(lg_ref, q_ref, k_ref, v_ref, g_ref, sb_ref, s0_ref, w_ref, x_ref, gp_ref,
                    xo_ref, sfin_ref, s_scr, m_scr, dec_scr, y_scr, *, nch):
    c = pl.program_id(1)

    @pl.when(c == 0)
    def _():
        s_scr[...] = s0_ref[...]
        ri = lax.broadcasted_iota(jnp.int32, (RET_CHUNK, RET_CHUNK), 0)
        ci = lax.broadcasted_iota(jnp.int32, (RET_CHUNK, RET_CHUNK), 1)
        rel = (ri - ci).astype(F32)
        for h in range(RET_HEADS):
            mf = jnp.where(rel >= 0, jnp.exp(lg_ref[0, h] * jnp.maximum(rel, 0.0)), 0.0)
            mb = jnp.where(rel <= 0, jnp.exp(lg_ref[1, h] * jnp.maximum(-rel, 0.0)), 0.0)
            m_scr[h] = mf + mb
        pos = _chunk_pos()
        for h in range(RET_HEADS):
            dec_scr[h, 0] = jnp.exp(lg_ref[0, h] * (pos + 1.0)).astype(BF16)
            dec_scr[h, 1] = jnp.exp(lg_ref[1, h] * (RET_CHUNK - pos)).astype(BF16)
            dec_scr[h, 2] = jnp.exp(lg_ref[0, h] * (RET_CHUNK - 1.0 - pos)).astype(BF16)

    for ci, h in [(ci, h) for ci in range(nch) for h in range(RET_HEADS)]:
        rows = slice(ci * RET_CHUNK, (ci + 1) * RET_CHUNK)
        vcols = slice(h * RET_DV, (h + 1) * RET_DV)
        q = q_ref[rows, h * RET_DK:(h + 1) * RET_DK]
        k = k_ref[rows, h * RET_DK:(h + 1) * RET_DK]
        v = v_ref[rows, vcols]
        st = s_scr[h]
        s = lax.dot_general(q, k, (((1,), (1,)), ((), ())), preferred_element_type=F32)
        o = jnp.dot((s * m_scr[h]).astype(BF16), v, preferred_element_type=F32)
        o = jnp.dot(_row_scale(q, dec_scr[h, 0]), st.astype(BF16), preferred_element_type=F32) + o
        o = jnp.dot(_row_scale(q, dec_scr[h, 1]), sb_ref[ci, h], preferred_element_type=F32) + o
        kv = lax.dot_general(_row_scale(k, dec_scr[h, 2]), v, (((0,), (0,)), ((), ())),
                             preferred_element_type=F32)
        s_scr[h] = st * jnp.exp(lg_ref[0, h] * RET_CHUNK) + kv
        dlt = o - jnp.mean(o, axis=-1, keepdims=True)
        var = jnp.mean(dlt * dlt, axis=-1, keepdims=True)
        gh = g_ref[rows, vcols]
        y_scr[rows, vcols] = ((dlt * lax.rsqrt(var + EPS)).astype(BF16)
                              * (gh * (jnp.tanh(gh) + 1.0)))
        if h == RET_HEADS - 1:
            _residual_out(x_ref, _gated_norm_proj(y_scr[rows, :], w_ref, gp_ref), xo_ref, rows)

    @pl.when(c == pl.num_programs(1) - 1)
    def _():
        sfin_ref[...] = s_scr[...]


def _state_spec():
    return pl.BlockSpec((None, RET_HEADS, RET_DK, RET_DV), lambda i, j: (i, 0, 0, 0))


def _step_chunks(n_chunks, wanted):
    return wanted if n_chunks % wanted == 0 else 1


def _ret_fwd(p, sb, lg, s0, w_out, x, gp):
    b, t, _ = p.shape
    d = x.shape[-1]
    nc = t // RET_CHUNK
    nch = _step_chunks(nc, RET_FWD_STEP_CHUNKS)
    rows = nch * RET_CHUNK
    fwd = lambda blk: (lambda i, j: (i, j, blk))
    resid = pl.BlockSpec((None, rows, d), fwd(0))
    return pl.pallas_call(
        functools.partial(_ret_fwd_kernel, nch=nch),
        grid=(b, nc // nch),
        in_specs=[_smem_spec(),
                  pl.BlockSpec((None, rows, RET_QK), fwd(0)),
                  pl.BlockSpec((None, rows, RET_QK), fwd(1)),
                  pl.BlockSpec((None, rows, RET_V), fwd(1)),
                  pl.BlockSpec((None, rows, RET_V), fwd(2)),
                  pl.BlockSpec((None, nch, RET_HEADS, RET_DK, RET_DV),
                               lambda i, j: (i, j, 0, 0, 0)),
                  _state_spec(),
                  pl.BlockSpec((RET_V, d), lambda i, j: (0, 0)),
                  resid,
                  pl.BlockSpec((None, 1, d), lambda i, j: (i, 0, 0))],
        out_specs=[resid, _state_spec()],
        out_shape=[jax.ShapeDtypeStruct((b, t, d), F32),
                   jax.ShapeDtypeStruct(s0.shape, F32)],
        scratch_shapes=[pltpu.VMEM((RET_HEADS, RET_DK, RET_DV), F32),
                        pltpu.VMEM((RET_HEADS, RET_CHUNK, RET_CHUNK), F32),
                        pltpu.VMEM((RET_HEADS, 3, RET_CHUNK, LANES), BF16),
                        pltpu.VMEM((rows, RET_V), BF16)],
        compiler_params=_params(2, 56),
        name="ret_fwd",
    )(lg, p, p, p, p, sb, s0, w_out, x, gp)


def _lru_coeffs(xc_ref, wg_ref, bg_ref, kh_ref, a_scr, u_scr):
    for nb in range(LRU_BLOCKS):
        xc = xc_ref[nb]
        res = jnp.dot(xc.astype(BF16), wg_ref[nb], preferred_element_type=F32)
        t_r = jnp.tanh(res[:, :LRU_BW] + bg_ref[0, nb:nb + 1, :])
        t_g = jnp.tanh(res[:, LRU_BW:] + bg_ref[1, nb:nb + 1, :])
        kh = kh_ref[nb:nb + 1, :]
        p = jnp.tanh(t_r * kh + kh)
        w = 1.0 / (1.0 + p)
        a_scr[nb] = (1.0 - p) * w
        sqrt_p = jnp.where(p > 0.0, p * lax.rsqrt(p), 0.0)
        u_scr[nb] = sqrt_p * w * (t_g + 1.0) * xc


def _lru_scan(a_scr, u_scr, dst_ref, carry_scr, tt, reverse):
    n = tt // SUBLANES
    shape = (LRU_BLOCKS, SUBLANES, LRU_BW)
    sub = lax.broadcasted_iota(jnp.int32, shape, 1)
    steps = [slice(g * SUBLANES, (g + 1) * SUBLANES) for g in range(n)]
    if reverse:
        steps = steps[::-1]

    a_seg = a_scr[:, steps[0], :]
    h_seg = u_scr[:, steps[0], :]
    for rows in steps[1:]:
        a = a_scr[:, rows, :]
        h_seg = a * h_seg + u_scr[:, rows, :]
        a_seg = a_seg * a
    for d in (1, 2, 4):
        ok = (sub < SUBLANES - d) if reverse else (sub >= d)
        shift = SUBLANES - d if reverse else d
        a_s = jnp.where(ok, pltpu.roll(a_seg, shift, 1), 1.0)
        h_s = jnp.where(ok, pltpu.roll(h_seg, shift, 1), 0.0)
        h_seg = a_seg * h_s + h_seg
        a_seg = a_seg * a_s
    carry = carry_scr[...]
    h_end = a_seg * carry + h_seg
    if reverse:
        h_in = jnp.where(sub == SUBLANES - 1, carry, pltpu.roll(h_end, SUBLANES - 1, 1))
        last = h_end[:, 0:1, :]
    else:
        h_in = jnp.where(sub == 0, carry, pltpu.roll(h_end, 1, 1))
        last = h_end[:, SUBLANES - 1:SUBLANES, :]
    carry_scr[...] = jnp.broadcast_to(last, shape)

    h = h_in
    for rows in steps:
        h = a_scr[:, rows, :] * h + u_scr[:, rows, :]
        dst_ref[:, rows, :] = h


def _lru_bwd_kernel(xc_ref, wg_ref, bg_ref, kh_ref, h0_ref, hb_ref, hfin_ref,
                    a_scr, u_scr, carry_scr, *, tt):
    i = pl.program_id(1)

    @pl.when(i == 0)
    def _():
        carry_scr[...] = h0_ref[...]

    _lru_coeffs(xc_ref, wg_ref, bg_ref, kh_ref, a_scr, u_scr)
    _lru_scan(a_scr, u_scr, hb_ref, carry_scr, tt, True)

    @pl.when(i == pl.num_programs(1) - 1)
    def _():
        hfin_ref[...] = carry_scr[...]


def _lru_fwd_kernel(xc_ref, wg_ref, bg_ref, kh_ref, h0_ref, hb_ref, g_ref, w_ref, x_ref, gp_ref,
                    xo_ref, hfin_ref, a_scr, u_scr, carry_scr, y_scr, z_scr, *, tt):
    i = pl.program_id(1)

    @pl.when(i == 0)
    def _():
        carry_scr[...] = h0_ref[...]

    _lru_coeffs(xc_ref, wg_ref, bg_ref, kh_ref, a_scr, u_scr)
    _lru_scan(a_scr, u_scr, u_scr, carry_scr, tt, False)
    for nb in range(LRU_BLOCKS):
        gh = g_ref[nb]
        y = (u_scr[nb] + hb_ref[nb]).astype(BF16) * (gh * (jnp.tanh(gh) + 1.0))
        y_scr[:, nb * LRU_BW:(nb + 1) * LRU_BW] = y
    _residual_out_interleaved(x_ref, _gated_norm_proj(y_scr[...], w_ref, gp_ref), xo_ref, z_scr)

    @pl.when(i == pl.num_programs(1) - 1)
    def _():
        hfin_ref[...] = carry_scr[...]


def _lru_specs(b, t, tt, reverse):
    nblk = t // tt
    tidx = (lambda j: nblk - 1 - j) if reverse else (lambda j: j)
    slab = pl.BlockSpec((None, LRU_BLOCKS, tt, LRU_BW), lambda i, j: (i, 0, tidx(j), 0))
    state = pl.BlockSpec((None, LRU_BLOCKS, SUBLANES, LRU_BW), lambda i, j: (i, 0, 0, 0))
    gate = [pl.BlockSpec((LRU_BLOCKS, LRU_BW, 2 * LRU_BW), lambda i, j: (0, 0, 0)),
            pl.BlockSpec((2, LRU_BLOCKS, LRU_BW), lambda i, j: (0, 0, 0)),
            pl.BlockSpec((LRU_BLOCKS, LRU_BW), lambda i, j: (0, 0))]
    return nblk, slab, state, gate


def _lru_bwd(xc, wg, bg, kh, h0, *, tt):
    b, _, t, _ = xc.shape
    nblk, slab, state, gate = _lru_specs(b, t, tt, True)
    tile = (LRU_BLOCKS, tt, LRU_BW)
    return pl.pallas_call(
        functools.partial(_lru_bwd_kernel, tt=tt),
        grid=(b, nblk),
        in_specs=[slab] + gate + [state],
        out_specs=[slab, state],
        out_shape=[jax.ShapeDtypeStruct(xc.shape, F32), jax.ShapeDtypeStruct(h0.shape, F32)],
        scratch_shapes=[pltpu.VMEM(tile, F32), pltpu.VMEM(tile, F32),
                        pltpu.VMEM((LRU_BLOCKS, SUBLANES, LRU_BW), F32)],
        compiler_params=_params(2, 48),
        name="lru_bwd",
    )(xc, wg, bg, kh, h0)


def _lru_fwd(xc, wg, bg, kh, h0, hb, g, w_out, x, gp, *, tt):
    b, _, t, _ = xc.shape
    d = x.shape[-1]
    nblk, slab, state, gate = _lru_specs(b, t, tt, False)
    tile = (LRU_BLOCKS, tt, LRU_BW)
    resid = pl.BlockSpec((None, tt, d), lambda i, j: (i, j, 0))
    return pl.pallas_call(
        functools.partial(_lru_fwd_kernel, tt=tt),
        grid=(b, nblk),
        in_specs=[slab] + gate + [state, slab, slab,
                                  pl.BlockSpec((LRU_WIDTH, d), lambda i, j: (0, 0)),
                                  resid,
                                  pl.BlockSpec((None, 1, d), lambda i, j: (i, 0, 0))],
        out_specs=[resid, state],
        out_shape=[jax.ShapeDtypeStruct((b, t, d), F32),
                   jax.ShapeDtypeStruct(h0.shape, F32)],
        scratch_shapes=[pltpu.VMEM(tile, F32), pltpu.VMEM(tile, F32),
                        pltpu.VMEM((LRU_BLOCKS, SUBLANES, LRU_BW), F32),
                        pltpu.VMEM((tt, LRU_WIDTH), BF16),
                        pltpu.VMEM((d // LANES, tt, LANES), F32)],
        compiler_params=_params(2, 56),
        name="lru_fwd",
    )(xc, wg, bg, kh, h0, hb, g, w_out, x, gp)


def _rope_tables(n_tok):
    n_rows = n_tok // GRID_W
    row = jnp.repeat(jnp.arange(n_rows, dtype=F32), GRID_W)
    col = jnp.tile(jnp.arange(GRID_W, dtype=F32), n_rows)
    n_freq = RET_DK // 4
    inv = ROPE_BASE ** (-jnp.arange(n_freq, dtype=F32) / n_freq)
    ang = jnp.concatenate([row[:, None] * inv, col[:, None] * inv], axis=-1)
    return jnp.cos(ang), jnp.sin(ang)


def _ret_w_prep_kernel(w_ref, m_ref, o_ref):
    o_ref[...] = jnp.dot(w_ref[...].astype(BF16), m_ref[...],
                         preferred_element_type=F32).astype(BF16)


def _ret_w_prep(w_in_all, layer):
    d = w_in_all.shape[1]
    src = np.arange(RET_DK)
    perm = np.zeros((RET_DK, RET_DK), np.float32)
    perm[src, (src % 2) * HALF_DK + src // 2] = 1.0
    heads = RET_DV // RET_DK
    cols = heads * RET_DK
    eye = np.eye(cols, dtype=np.float32)
    mats = jnp.asarray(np.stack([np.kron(np.eye(heads, dtype=np.float32), perm), eye, 0.5 * eye]),
                       BF16)
    qk_blocks = 2 * RET_QK // cols
    v_blocks = RET_V // cols

    def which(j):
        return (j >= qk_blocks).astype(jnp.int32) + (j >= qk_blocks + v_blocks).astype(jnp.int32)

    return pl.pallas_call(
        _ret_w_prep_kernel,
        grid=(RET_IN // cols,),
        in_specs=[pl.BlockSpec((None, d, cols), lambda j: (layer, 0, j)),
                  pl.BlockSpec((None, cols, cols), lambda j: (which(j), 0, 0))],
        out_specs=pl.BlockSpec((d, cols), lambda j: (0, j)),
        out_shape=jax.ShapeDtypeStruct((d, RET_IN), BF16),
        compiler_params=_params(1, 32),
        name="ret_w_prep",
    )(w_in_all, mats)


def _retention_layer(x, s_ctx, mods, cos, sin, w_in_all, layer, log_decay, gn_g, w_out):
    (mul_l, add_l, gp_l), (mul_c, add_c, gp_c) = mods
    b = x.shape[0]
    lc = s_ctx.shape[1]
    w = _ret_w_prep(w_in_all, layer)
    lg = -jnp.abs(log_decay.astype(F32))
    w_o = (gn_g.astype(F32)[:, None] * w_out).astype(BF16)

    zero = jnp.zeros((b, RET_HEADS, RET_DK, RET_DV), F32)
    p_c, sb_c, sb_fin = _ret_inproj(s_ctx, mul_c, add_c, w, cos[:lc], sin[:lc], lg, zero,
                                    rope=False, tm=lc)
    p_l, sb_l, _ = _ret_inproj(x, mul_l, add_l, w, cos, sin, lg, sb_fin, rope=True, tm=ROW_TILE)
    s_ctx, sf_c = _ret_fwd(p_c, sb_c, lg, zero, w_o, s_ctx, gp_c)
    x, _ = _ret_fwd(p_l, sb_l, lg, sf_c, w_o, x, gp_l)
    return x, s_ctx


def _lru_layer(x, s_ctx, mods, w_in, conv_w, conv_b, w_a, b_a, w_x, b_x, lam, w_out):
    (mul_l, add_l, gp_l), (mul_c, add_c, gp_c) = mods
    b = x.shape[0]
    lc = s_ctx.shape[1]
    w = jnp.concatenate([w_in[:, :LRU_WIDTH], 0.5 * w_in[:, LRU_WIDTH:]], axis=1).astype(BF16)
    w_o = w_out.astype(BF16)
    cw = conv_w.astype(F32).reshape(CONV_W, LRU_BLOCKS, LRU_BW)
    cb = conv_b.astype(F32).reshape(LRU_BLOCKS, LRU_BW)
    kh = (0.25 * LRU_C) * jax.nn.softplus(-lam.astype(F32))

    def gate_params(d):
        wg = (0.5 * jnp.concatenate([w_a[d], w_x[d]], axis=-1)).astype(BF16)
        bg = (0.5 * jnp.stack([b_a[d], b_x[d]])).astype(F32).reshape(2, LRU_BLOCKS, LRU_BW)
        return wg, bg, kh[d].reshape(LRU_BLOCKS, LRU_BW)

    xc_l, g_l = _lru_inproj(x, mul_l, add_l, w, cw, cb, tm=ROW_TILE)
    xc_c, g_c = _lru_inproj(s_ctx, mul_c, add_c, w, cw, cb, tm=lc)
    zero = jnp.zeros((b, LRU_BLOCKS, SUBLANES, LRU_BW), F32)
    hb_c, hb_fin = _lru_bwd(xc_c, *gate_params(1), zero, tt=lc)
    hb_l, _ = _lru_bwd(xc_l, *gate_params(1), hb_fin, tt=ROW_TILE)
    s_ctx, hf_fin = _lru_fwd(xc_c, *gate_params(0), zero, hb_c, g_c, w_o, s_ctx, gp_c, tt=lc)
    x, _ = _lru_fwd(xc_l, *gate_params(0), hf_fin, hb_l, g_l, w_o, x, gp_l, tt=ROW_TILE)
    return x, s_ctx


def kernel(x, c, ctx, c_ctx, mod_w, mod_b, norm_pre, norm_post, ret_w_in, ret_log_decay, ret_gn,
           ret_w_out, lru_w_in, lru_conv_w, lru_conv_b, lru_w_a, lru_b_a, lru_w_x, lru_b_x,
           lru_lambda, lru_w_out):
    b, n_tok, d = x.shape
    depth = mod_w.shape[0]
    assert n_tok % ROW_TILE == 0 and ROW_TILE % RET_CHUNK == 0
    assert ctx.shape[1] % RET_CHUNK == 0 and ctx.shape[1] <= ROW_TILE
    cos, sin = _rope_tables(n_tok)

    rows = -(-(b + 1) // SUBLANES) * SUBLANES
    cc = jnp.concatenate([c, c_ctx[None, :], jnp.zeros((rows - b - 1, d), F32)], axis=0)
    mod = _modulation(cc, mod_w, mod_b)

    s_ctx = ctx
    for i in range(depth):
        shift, scale, gate = jnp.split(mod[i], 3, axis=-1)
        mods = []
        for sel in (slice(0, b), slice(b, b + 1)):
            mul = jnp.broadcast_to(norm_pre[i] * (1.0 + scale[sel]), (b, d))[:, None, :]
            add = jnp.broadcast_to(shift[sel], (b, d))[:, None, :]
            gp = jnp.broadcast_to(gate[sel] * norm_post[i], (b, d))[:, None, :]
            mods.append((mul, add, gp))
        j = i // 2
        if i % 2 == 0:
            x, s_ctx = _retention_layer(x, s_ctx, mods, cos, sin, ret_w_in, j, ret_log_decay[j],
                                        ret_gn[j], ret_w_out[j])
        else:
            x, s_ctx = _lru_layer(x, s_ctx, mods, lru_w_in[j], lru_conv_w[j], lru_conv_b[j],
                                  lru_w_a[j], lru_b_a[j], lru_w_x[j], lru_b_x[j],
                                  lru_lambda[j], lru_w_out[j])
    return x
```

```python
import functools

import jax
import jax.numpy as jnp
import numpy as np
from jax import lax
from jax.experimental import pallas as pl
from jax.experimental.pallas import tpu as pltpu

F32 = jnp.float32
BF16 = jnp.bfloat16

EPS = 1e-6
GRID_W = 64
ROPE_BASE = 10000.0

RET_HEADS = 4
RET_DK = 256
RET_DV = 512
RET_QK = RET_HEADS * RET_DK
RET_V = RET_HEADS * RET_DV
RET_IN = 2 * RET_QK + 2 * RET_V
HALF_DK = RET_DK // 2

LRU_BLOCKS = 10
LRU_BW = 128
LRU_WIDTH = LRU_BLOCKS * LRU_BW
CONV_W = 4
LRU_C = 8.0

LANES = 128
SUBLANES = 8
MIB = 1024 * 1024

RET_CHUNK = 256
RET_FWD_STEP_CHUNKS = 2
ROW_TILE = 512


def _params(n_axes, vmem_mib):
    return pltpu.CompilerParams(
        dimension_semantics=("arbitrary",) * n_axes,
        vmem_limit_bytes=vmem_mib * MIB)


def _smem_spec():
    return pl.BlockSpec(memory_space=pltpu.SMEM)


def _mod_kernel(cc_ref, w_ref, b_ref, o_ref):
    cc = cc_ref[...]
    act = cc * jax.nn.sigmoid(cc)
    o_ref[...] = jnp.dot(act.astype(BF16), w_ref[...].astype(BF16),
                         preferred_element_type=F32) + b_ref[...]


def _modulation(cc, mod_w, mod_b):
    depth, d, d3 = mod_w.shape
    rows = cc.shape[0]
    return pl.pallas_call(
        _mod_kernel,
        grid=(depth, d3 // d),
        in_specs=[pl.BlockSpec((rows, d), lambda i, j: (0, 0)),
                  pl.BlockSpec((None, d, d), lambda i, j: (i, 0, j)),
                  pl.BlockSpec((None, 1, d), lambda i, j: (i, 0, j))],
        out_specs=pl.BlockSpec((None, rows, d), lambda i, j: (i, 0, j)),
        out_shape=jax.ShapeDtypeStruct((depth, rows, d3), F32),
        compiler_params=_params(2, 32),
        name="modulation",
    )(cc, mod_w, mod_b.reshape(depth, 1, d3))


def _norm_modulate(x_ref, mul_ref, add_ref, h_scr):
    x = x_ref[...]
    ms = jnp.mean(x * x, axis=-1, keepdims=True)
    h = x * lax.rsqrt(ms + EPS) * mul_ref[...] + add_ref[...]
    h_scr[...] = h.astype(BF16)


def _ret_inproj_kernel(lg_ref, x_ref, mul_ref, add_ref, w_ref, cos_ref, sin_ref, s0_ref,
                       o_ref, sb_ref, sfin_ref, h_scr, s_scr, dec_scr, *, rope, nch):
    t = pl.program_id(1)

    @pl.when(t == 0)
    def _():
        s_scr[...] = s0_ref[...]
        pos = _chunk_pos()
        for h in range(RET_HEADS):
            dec_scr[h] = jnp.exp(lg_ref[1, h] * pos).astype(BF16)

    _norm_modulate(x_ref, mul_ref, add_ref, h_scr)
    hb = h_scr[...]
    scale = RET_DK ** -0.5
    for j in range(2 * RET_HEADS):
        cols = slice(j * RET_DK, (j + 1) * RET_DK)
        acc = jnp.dot(hb, w_ref[:, cols], preferred_element_type=F32)
        if j >= RET_HEADS:
            acc = acc * scale
        if rope:
            e, o = acc[:, :HALF_DK], acc[:, HALF_DK:]
            c, s = cos_ref[...], sin_ref[...]
            acc = jnp.concatenate([e * c - o * s, e * s + o * c], axis=1)
        o_ref[:, cols] = acc.astype(BF16)
    for j in range(2 * RET_QK // RET_DV, RET_IN // RET_DV):
        cols = slice(j * RET_DV, (j + 1) * RET_DV)
        o_ref[:, cols] = jnp.dot(hb, w_ref[:, cols], preferred_element_type=F32).astype(BF16)

    for ci in reversed(range(nch)):
        rows = slice(ci * RET_CHUNK, (ci + 1) * RET_CHUNK)
        for h in range(RET_HEADS):
            k = o_ref[rows, RET_QK + h * RET_DK:RET_QK + (h + 1) * RET_DK]
            v = o_ref[rows, 2 * RET_QK + h * RET_DV:2 * RET_QK + (h + 1) * RET_DV]
            st = s_scr[h]
            sb_ref[ci, h] = st.astype(BF16)
            kv = lax.dot_general(_row_scale(k, dec_scr[h]), v, (((0,), (0,)), ((), ())),
                                 preferred_element_type=F32)
            s_scr[h] = st * jnp.exp(lg_ref[1, h] * RET_CHUNK) + kv

    @pl.when(t == pl.num_programs(1) - 1)
    def _():
        sfin_ref[...] = s_scr[...]


def _ret_inproj(x, mul, add, w, cos, sin, lg, s0, *, rope, tm):
    b, t, d = x.shape
    nt = t // tm
    nch = tm // RET_CHUNK
    kern = functools.partial(_ret_inproj_kernel, rope=rope, nch=nch)
    rev = lambda i, j: (i, nt - 1 - j, 0)
    return pl.pallas_call(
        kern,
        grid=(b, nt),
        in_specs=[_smem_spec(),
                  pl.BlockSpec((None, tm, d), rev),
                  pl.BlockSpec((None, 1, d), lambda i, j: (i, 0, 0)),
                  pl.BlockSpec((None, 1, d), lambda i, j: (i, 0, 0)),
                  pl.BlockSpec((d, RET_IN), lambda i, j: (0, 0), pipeline_mode=pl.Buffered(1)),
                  pl.BlockSpec((tm, HALF_DK), lambda i, j: (nt - 1 - j, 0)),
                  pl.BlockSpec((tm, HALF_DK), lambda i, j: (nt - 1 - j, 0)),
                  _state_spec()],
        out_specs=[pl.BlockSpec((None, tm, RET_IN), rev),
                   pl.BlockSpec((None, nch, RET_HEADS, RET_DK, RET_DV),
                                lambda i, j: (i, nt - 1 - j, 0, 0, 0)),
                   _state_spec()],
        out_shape=[jax.ShapeDtypeStruct((b, t, RET_IN), BF16),
                   jax.ShapeDtypeStruct((b, t // RET_CHUNK, RET_HEADS, RET_DK, RET_DV), BF16),
                   jax.ShapeDtypeStruct(s0.shape, F32)],
        scratch_shapes=[pltpu.VMEM((tm, d), BF16),
                        pltpu.VMEM((RET_HEADS, RET_DK, RET_DV), F32),
                        pltpu.VMEM((RET_HEADS, RET_CHUNK, LANES), BF16)],
        compiler_params=_params(2, 56),
        name="ret_inproj_rope" if rope else "ret_inproj",
    )(lg, x, mul, add, w, cos, sin, s0)


def _store_interleaved(dst_ref, blk, val):
    seg = val.shape[0] // SUBLANES
    for s in range(SUBLANES):
        dst_ref[blk, pl.ds(s, seg, stride=SUBLANES), :] = val[s * seg:(s + 1) * seg, :]


def _lru_inproj_kernel(x_ref, xh_ref, mul_ref, add_ref, w_ref, cw_ref, cb_ref, wg_ref, bg_ref,
                       kh_ref, h0_ref, xc_ref, g_ref, hb_ref, hfin_ref,
                       h_scr, hs_scr, ext_scr, a_scr, u_scr, carry_scr):
    i = pl.program_id(1)
    has_prev = i < pl.num_programs(1) - 1
    has_next = i > 0
    tm, d = x_ref.shape
    halo_rows = xh_ref.shape[0]

    @pl.when(i == 0)
    def _():
        carry_scr[...] = h0_ref[...]

    def norm_modulate(x):
        ms = jnp.mean(x * x, axis=-1, keepdims=True)
        return x * lax.rsqrt(ms + EPS) * mul_ref[...] + add_ref[...]

    h = norm_modulate(x_ref[...])
    for s in range(d // LANES):
        _store_interleaved(hs_scr, s, h[:, s * LANES:(s + 1) * LANES])
    for s in range(d // LANES):
        h_scr[0:tm, s * LANES:(s + 1) * LANES] = hs_scr[s].astype(BF16)
    h_scr[tm:tm + halo_rows, :] = norm_modulate(xh_ref[...]).astype(BF16)

    sub = lax.broadcasted_iota(jnp.int32, (SUBLANES, LRU_BW), 0)
    left = CONV_W // 2
    n_chunk = LRU_WIDTH // (2 * LANES)

    def project(j):
        cols = slice(j * 2 * LANES, (j + 1) * 2 * LANES)
        gcols = slice(LRU_WIDTH + j * 2 * LANES, LRU_WIDTH + (j + 1) * 2 * LANES)
        return (jnp.dot(h_scr[...], w_ref[:, cols], preferred_element_type=F32),
                jnp.dot(h_scr[0:tm, :], w_ref[:, gcols], preferred_element_type=F32).astype(BF16))

    ahead = project(0)
    for j in range(n_chunk):
        acc, gacc = ahead
        if j + 1 < n_chunk:
            ahead = project(j + 1)
        for half in range(2):
            blk = 2 * j + half
            lanes = slice(half * LANES, (half + 1) * LANES)
            g_ref[blk] = gacc[:, lanes]
            ext = ext_scr.at[half]
            ext[left * SUBLANES:left * SUBLANES + tm, :] = acc[0:tm, lanes]
            for k in range(left):
                src = acc[tm - (left - k) * SUBLANES:tm - (left - k - 1) * SUBLANES, lanes]
                before = jnp.where(has_prev, acc[tm + k:tm + k + 1, lanes], 0.0)
                ext[k * SUBLANES:(k + 1) * SUBLANES, :] = jnp.where(
                    sub == 0, before, pltpu.roll(src, 1, 0))
            for k in range(CONV_W - 1 - left):
                src = acc[k * SUBLANES:(k + 1) * SUBLANES, lanes]
                after = jnp.where(has_next, acc[tm + left + k:tm + left + k + 1, lanes], 0.0)
                ext[(left + k) * SUBLANES + tm:(left + k + 1) * SUBLANES + tm, :] = jnp.where(
                    sub == SUBLANES - 1, after, pltpu.roll(src, SUBLANES - 1, 0))
            xc = cb_ref[blk:blk + 1, :]
            for tap in range(CONV_W):
                xc = xc + ext[tap * SUBLANES:tap * SUBLANES + tm, :] * cw_ref[tap, blk:blk + 1, :]
            xc_ref[blk] = xc
            _lru_coeff_block(blk, xc, wg_ref, bg_ref, kh_ref, a_scr, u_scr)

    _lru_scan(a_scr, u_scr, hb_ref, carry_scr, tm, True)

    @pl.when(i == pl.num_programs(1) - 1)
    def _():
        hfin_ref[...] = carry_scr[...]


def _conv_halo_rows(x, tm):
    b, t, d = x.shape
    nblk = t // tm
    r = x.reshape(b, nblk, tm, d)
    left = CONV_W // 2
    prev = jnp.concatenate([jnp.zeros((b, 1, left, d), F32), r[:, :-1, tm - left:, :]], axis=1)
    nxt = jnp.concatenate([r[:, 1:, 0:CONV_W - 1 - left, :],
                           jnp.zeros((b, 1, CONV_W - 1 - left, d), F32)], axis=1)
    pad = jnp.zeros((b, nblk, 2 * SUBLANES - (CONV_W - 1), d), F32)
    return jnp.concatenate([prev, nxt, pad], axis=2)


def _lru_inproj(x, mul, add, w, cw, cb, wg, bg, kh, h0, *, tm):
    b, t, d = x.shape
    halo_rows = 2 * SUBLANES
    nblk, slab, state, gate = _lru_specs(b, t, tm, True)
    tile = (LRU_BLOCKS, tm, LRU_BW)
    slab_f32 = jax.ShapeDtypeStruct((b, LRU_BLOCKS, t, LRU_BW), F32)
    return pl.pallas_call(
        _lru_inproj_kernel,
        grid=(b, nblk),
        in_specs=[pl.BlockSpec((None, tm, d), lambda i, j: (i, nblk - 1 - j, 0)),
                  pl.BlockSpec((None, None, halo_rows, d), lambda i, j: (i, nblk - 1 - j, 0, 0)),
                  pl.BlockSpec((None, 1, d), lambda i, j: (i, 0, 0)),
                  pl.BlockSpec((None, 1, d), lambda i, j: (i, 0, 0)),
                  pl.BlockSpec((d, 2 * LRU_WIDTH), lambda i, j: (0, 0)),
                  pl.BlockSpec((CONV_W, LRU_BLOCKS, LRU_BW), lambda i, j: (0, 0, 0)),
                  pl.BlockSpec((LRU_BLOCKS, LRU_BW), lambda i, j: (0, 0))] + gate + [state],
        out_specs=[slab, slab, slab, state],
        out_shape=[slab_f32, jax.ShapeDtypeStruct((b, LRU_BLOCKS, t, LRU_BW), BF16), slab_f32,
                   jax.ShapeDtypeStruct(h0.shape, F32)],
        scratch_shapes=[pltpu.VMEM((tm + halo_rows, d), BF16),
                        pltpu.VMEM((d // LANES, tm, LANES), F32),
                        pltpu.VMEM((2, tm + (CONV_W - 1) * SUBLANES, LRU_BW), F32),
                        pltpu.VMEM(tile, F32), pltpu.VMEM(tile, F32),
                        pltpu.VMEM((LRU_BLOCKS, SUBLANES, LRU_BW), F32)],
        compiler_params=_params(2, 56),
        name="lru_inproj",
    )(x, _conv_halo_rows(x, tm), mul, add, w, cw, cb, wg, bg, kh, h0)


def _gated_norm_proj(y, w_ref, gp_ref):
    z = jnp.dot(y, w_ref[...], preferred_element_type=F32)
    ms = jnp.mean(z * z, axis=-1, keepdims=True)
    return z * lax.rsqrt(ms + EPS) * gp_ref[...]


def _residual_out(x_ref, zn, o_ref, rows):
    o_ref[rows, :] = x_ref[rows, :] + zn


def _residual_out_interleaved(x_ref, zn, o_ref, z_scr):
    n_slab = zn.shape[1] // LANES
    seg = zn.shape[0] // SUBLANES
    for d in range(n_slab):
        z_scr[d] = zn[:, d * LANES:(d + 1) * LANES]
    for d in range(n_slab):
        cols = slice(d * LANES, (d + 1) * LANES)
        for s in range(SUBLANES):
            rows = slice(s * seg, (s + 1) * seg)
            o_ref[rows, cols] = x_ref[rows, cols] + z_scr[d, pl.ds(s, seg, stride=SUBLANES), :]


def _row_scale(x, dec):
    n = x.shape[1] // LANES
    return jnp.concatenate([x[:, i * LANES:(i + 1) * LANES] * dec for i in range(n)], axis=1)


def _chunk_pos():
    return lax.broadcasted_iota(jnp.int32, (RET_CHUNK, LANES), 0).astype(F32)


def _ret_fwd_kernel(lg_ref, q_ref, k_ref, v_ref, g_ref, sb_ref, s0_ref, w_ref, x_ref, gp_ref,
                    xo_ref, sfin_ref, s_scr, m_scr, dec_scr, y_scr, *, nch):
    c = pl.program_id(1)

    @pl.when(c == 0)
    def _():
        s_scr[...] = s0_ref[...]
        ri = lax.broadcasted_iota(jnp.int32, (RET_CHUNK, RET_CHUNK), 0)
        ci = lax.broadcasted_iota(jnp.int32, (RET_CHUNK, RET_CHUNK), 1)
        rel = (ri - ci).astype(F32)
        for h in range(RET_HEADS):
            mf = jnp.where(rel >= 0, jnp.exp(lg_ref[0, h] * jnp.maximum(rel, 0.0)), 0.0)
            mb = jnp.where(rel <= 0, jnp.exp(lg_ref[1, h] * jnp.maximum(-rel, 0.0)), 0.0)
            m_scr[h] = mf + mb
        pos = _chunk_pos()
        for h in range(RET_HEADS):
            dec_scr[h, 0] = jnp.exp(lg_ref[0, h] * (pos + 1.0)).astype(BF16)
            dec_scr[h, 1] = jnp.exp(lg_ref[1, h] * (RET_CHUNK - pos)).astype(BF16)
            dec_scr[h, 2] = jnp.exp(lg_ref[0, h] * (RET_CHUNK - 1.0 - pos)).astype(BF16)

    for ci, h in [(ci, h) for ci in range(nch) for h in range(RET_HEADS)]:
        rows = slice(ci * RET_CHUNK, (ci + 1) * RET_CHUNK)
        vcols = slice(h * RET_DV, (h + 1) * RET_DV)
        q = q_ref[rows, h * RET_DK:(h + 1) * RET_DK]
        k = k_ref[rows, h * RET_DK:(h + 1) * RET_DK]
        v = v_ref[rows, vcols]
        st = s_scr[h]
        s = lax.dot_general(q, k, (((1,), (1,)), ((), ())), preferred_element_type=F32)
        o = jnp.dot((s * m_scr[h]).astype(BF16), v, preferred_element_type=F32)
        o = jnp.dot(_row_scale(q, dec_scr[h, 0]), st.astype(BF16), preferred_element_type=F32) + o
        o = jnp.dot(_row_scale(q, dec_scr[h, 1]), sb_ref[ci, h], preferred_element_type=F32) + o
        kv = lax.dot_general(_row_scale(k, dec_scr[h, 2]), v, (((0,), (0,)), ((), ())),
                             preferred_element_type=F32)
        s_scr[h] = st * jnp.exp(lg_ref[0, h] * RET_CHUNK) + kv
        dlt = o - jnp.mean(o, axis=-1, keepdims=True)
        var = jnp.mean(dlt * dlt, axis=-1, keepdims=True)
        gh = g_ref[rows, vcols]
        y_scr[rows, vcols] = ((dlt * lax.rsqrt(var + EPS)).astype(BF16)
                              * (gh * (jnp.tanh(gh) + 1.0)))
        if h == RET_HEADS - 1:
            _residual_out(x_ref, _gated_norm_proj(y_scr[rows, :], w_ref, gp_ref), xo_ref, rows)

    @pl.when(c == pl.num_programs(1) - 1)
    def _():
        sfin_ref[...] = s_scr[...]


def _state_spec():
    return pl.BlockSpec((None, RET_HEADS, RET_DK, RET_DV), lambda i, j: (i, 0, 0, 0))


def _step_chunks(n_chunks, wanted):
    return wanted if n_chunks % wanted == 0 else 1


def _ret_fwd(p, sb, lg, s0, w_out, x, gp):
    b, t, _ = p.shape
    d = x.shape[-1]
    nc = t // RET_CHUNK
    nch = _step_chunks(nc, RET_FWD_STEP_CHUNKS)
    rows = nch * RET_CHUNK
    fwd = lambda blk: (lambda i, j: (i, j, blk))
    resid = pl.BlockSpec((None, rows, d), fwd(0))
    return pl.pallas_call(
        functools.partial(_ret_fwd_kernel, nch=nch),
        grid=(b, nc // nch),
        in_specs=[_smem_spec(),
                  pl.BlockSpec((None, rows, RET_QK), fwd(0)),
                  pl.BlockSpec((None, rows, RET_QK), fwd(1)),
                  pl.BlockSpec((None, rows, RET_V), fwd(1)),
                  pl.BlockSpec((None, rows, RET_V), fwd(2)),
                  pl.BlockSpec((None, nch, RET_HEADS, RET_DK, RET_DV),
                               lambda i, j: (i, j, 0, 0, 0)),
                  _state_spec(),
                  pl.BlockSpec((RET_V, d), lambda i, j: (0, 0)),
                  resid,
                  pl.BlockSpec((None, 1, d), lambda i, j: (i, 0, 0))],
        out_specs=[resid, _state_spec()],
        out_shape=[jax.ShapeDtypeStruct((b, t, d), F32),
                   jax.ShapeDtypeStruct(s0.shape, F32)],
        scratch_shapes=[pltpu.VMEM((RET_HEADS, RET_DK, RET_DV), F32),
                        pltpu.VMEM((RET_HEADS, RET_CHUNK, RET_CHUNK), F32),
                        pltpu.VMEM((RET_HEADS, 3, RET_CHUNK, LANES), BF16),
                        pltpu.VMEM((rows, RET_V), BF16)],
        compiler_params=_params(2, 56),
        name="ret_fwd",
    )(lg, p, p, p, p, sb, s0, w_out, x, gp)


def _lru_coeff_block(nb, xc, wg_ref, bg_ref, kh_ref, a_scr, u_scr):
    res = jnp.dot(xc.astype(BF16), wg_ref[nb], preferred_element_type=F32)
    t_r = jnp.tanh(res[:, :LRU_BW] + bg_ref[0, nb:nb + 1, :])
    t_g = jnp.tanh(res[:, LRU_BW:] + bg_ref[1, nb:nb + 1, :])
    kh = kh_ref[nb:nb + 1, :]
    p = jnp.tanh(t_r * kh + kh)
    w = 1.0 / (1.0 + p)
    a_scr[nb] = (1.0 - p) * w
    sqrt_p = jnp.where(p > 0.0, p * lax.rsqrt(p), 0.0)
    u_scr[nb] = sqrt_p * w * (t_g + 1.0) * xc


def _lru_scan(a_scr, u_scr, dst_ref, carry_scr, tt, reverse):
    n = tt // SUBLANES
    shape = (LRU_BLOCKS, SUBLANES, LRU_BW)
    sub = lax.broadcasted_iota(jnp.int32, shape, 1)
    steps = [slice(g * SUBLANES, (g + 1) * SUBLANES) for g in range(n)]
    if reverse:
        steps = steps[::-1]

    a_seg = a_scr[:, steps[0], :]
    h_seg = u_scr[:, steps[0], :]
    for rows in steps[1:]:
        a = a_scr[:, rows, :]
        h_seg = a * h_seg + u_scr[:, rows, :]
        a_seg = a_seg * a
    for d in (1, 2, 4):
        ok = (sub < SUBLANES - d) if reverse else (sub >= d)
        shift = SUBLANES - d if reverse else d
        a_s = jnp.where(ok, pltpu.roll(a_seg, shift, 1), 1.0)
        h_s = jnp.where(ok, pltpu.roll(h_seg, shift, 1), 0.0)
        h_seg = a_seg * h_s + h_seg
        a_seg = a_seg * a_s
    carry = carry_scr[...]
    h_end = a_seg * carry + h_seg
    if reverse:
        h_in = jnp.where(sub == SUBLANES - 1, carry, pltpu.roll(h_end, SUBLANES - 1, 1))
        last = h_end[:, 0:1, :]
    else:
        h_in = jnp.where(sub == 0, carry, pltpu.roll(h_end, 1, 1))
        last = h_end[:, SUBLANES - 1:SUBLANES, :]
    carry_scr[...] = jnp.broadcast_to(last, shape)

    h = h_in
    for rows in steps:
        h = a_scr[:, rows, :] * h + u_scr[:, rows, :]
        dst_ref[:, rows, :] = h


def _lru_fwd_kernel(xc_ref, wg_ref, bg_ref, kh_ref, h0_ref, hb_ref, g_ref, w_ref, x_ref, gp_ref,
                    xo_ref, hfin_ref, a_scr, u_scr, carry_scr, y_scr, z_scr, *, tt):
    i = pl.program_id(1)

    @pl.when(i == 0)
    def _():
        carry_scr[...] = h0_ref[...]

    for nb in range(LRU_BLOCKS):
        _lru_coeff_block(nb, xc_ref[nb], wg_ref, bg_ref, kh_ref, a_scr, u_scr)
    _lru_scan(a_scr, u_scr, u_scr, carry_scr, tt, False)
    for nb in range(LRU_BLOCKS):
        gh = g_ref[nb]
        y = (u_scr[nb] + hb_ref[nb]).astype(BF16) * (gh * (jnp.tanh(gh) + 1.0))
        y_scr[:, nb * LRU_BW:(nb + 1) * LRU_BW] = y
    _residual_out_interleaved(x_ref, _gated_norm_proj(y_scr[...], w_ref, gp_ref), xo_ref, z_scr)

    @pl.when(i == pl.num_programs(1) - 1)
    def _():
        hfin_ref[...] = carry_scr[...]


def _lru_specs(b, t, tt, reverse):
    nblk = t // tt
    tidx = (lambda j: nblk - 1 - j) if reverse else (lambda j: j)
    slab = pl.BlockSpec((None, LRU_BLOCKS, tt, LRU_BW), lambda i, j: (i, 0, tidx(j), 0))
    state = pl.BlockSpec((None, LRU_BLOCKS, SUBLANES, LRU_BW), lambda i, j: (i, 0, 0, 0))
    gate = [pl.BlockSpec((LRU_BLOCKS, LRU_BW, 2 * LRU_BW), lambda i, j: (0, 0, 0)),
            pl.BlockSpec((2, LRU_BLOCKS, LRU_BW), lambda i, j: (0, 0, 0)),
            pl.BlockSpec((LRU_BLOCKS, LRU_BW), lambda i, j: (0, 0))]
    return nblk, slab, state, gate


def _lru_fwd(xc, wg, bg, kh, h0, hb, g, w_out, x, gp, *, tt):
    b, _, t, _ = xc.shape
    d = x.shape[-1]
    nblk, slab, state, gate = _lru_specs(b, t, tt, False)
    tile = (LRU_BLOCKS, tt, LRU_BW)
    resid = pl.BlockSpec((None, tt, d), lambda i, j: (i, j, 0))
    return pl.pallas_call(
        functools.partial(_lru_fwd_kernel, tt=tt),
        grid=(b, nblk),
        in_specs=[slab] + gate + [state, slab, slab,
                                  pl.BlockSpec((LRU_WIDTH, d), lambda i, j: (0, 0)),
                                  resid,
                                  pl.BlockSpec((None, 1, d), lambda i, j: (i, 0, 0))],
        out_specs=[resid, state],
        out_shape=[jax.ShapeDtypeStruct((b, t, d), F32),
                   jax.ShapeDtypeStruct(h0.shape, F32)],
        scratch_shapes=[pltpu.VMEM(tile, F32), pltpu.VMEM(tile, F32),
                        pltpu.VMEM((LRU_BLOCKS, SUBLANES, LRU_BW), F32),
                        pltpu.VMEM((tt, LRU_WIDTH), BF16),
                        pltpu.VMEM((d // LANES, tt, LANES), F32)],
        compiler_params=_params(2, 56),
        name="lru_fwd",
    )(xc, wg, bg, kh, h0, hb, g, w_out, x, gp)


def _rope_tables(n_tok):
    n_rows = n_tok // GRID_W
    row = jnp.repeat(jnp.arange(n_rows, dtype=F32), GRID_W)
    col = jnp.tile(jnp.arange(GRID_W, dtype=F32), n_rows)
    n_freq = RET_DK // 4
    inv = ROPE_BASE ** (-jnp.arange(n_freq, dtype=F32) / n_freq)
    ang = jnp.concatenate([row[:, None] * inv, col[:, None] * inv], axis=-1)
    return jnp.cos(ang), jnp.sin(ang)


def _ret_w_prep_kernel(w_ref, m_ref, o_ref):
    o_ref[...] = jnp.dot(w_ref[...].astype(BF16), m_ref[...],
                         preferred_element_type=F32).astype(BF16)


def _ret_w_prep(w_in_all, layer):
    d = w_in_all.shape[1]
    src = np.arange(RET_DK)
    perm = np.zeros((RET_DK, RET_DK), np.float32)
    perm[src, (src % 2) * HALF_DK + src // 2] = 1.0
    heads = RET_DV // RET_DK
    cols = heads * RET_DK
    eye = np.eye(cols, dtype=np.float32)
    mats = jnp.asarray(np.stack([np.kron(np.eye(heads, dtype=np.float32), perm), eye, 0.5 * eye]),
                       BF16)
    qk_blocks = 2 * RET_QK // cols
    v_blocks = RET_V // cols

    def which(j):
        return (j >= qk_blocks).astype(jnp.int32) + (j >= qk_blocks + v_blocks).astype(jnp.int32)

    return pl.pallas_call(
        _ret_w_prep_kernel,
        grid=(RET_IN // cols,),
        in_specs=[pl.BlockSpec((None, d, cols), lambda j: (layer, 0, j)),
                  pl.BlockSpec((None, cols, cols), lambda j: (which(j), 0, 0))],
        out_specs=pl.BlockSpec((d, cols), lambda j: (0, j)),
        out_shape=jax.ShapeDtypeStruct((d, RET_IN), BF16),
        compiler_params=_params(1, 32),
        name="ret_w_prep",
    )(w_in_all, mats)


def _retention_layer(x, s_ctx, mods, cos, sin, w_in_all, layer, log_decay, gn_g, w_out):
    (mul_l, add_l, gp_l), (mul_c, add_c, gp_c) = mods
    b = x.shape[0]
    lc = s_ctx.shape[1]
    w = _ret_w_prep(w_in_all, layer)
    lg = -jnp.abs(log_decay.astype(F32))
    w_o = (gn_g.astype(F32)[:, None] * w_out).astype(BF16)

    zero = jnp.zeros((b, RET_HEADS, RET_DK, RET_DV), F32)
    p_c, sb_c, sb_fin = _ret_inproj(s_ctx, mul_c, add_c, w, cos[:lc], sin[:lc], lg, zero,
                                    rope=False, tm=lc)
    p_l, sb_l, _ = _ret_inproj(x, mul_l, add_l, w, cos, sin, lg, sb_fin, rope=True, tm=ROW_TILE)
    s_ctx, sf_c = _ret_fwd(p_c, sb_c, lg, zero, w_o, s_ctx, gp_c)
    x, _ = _ret_fwd(p_l, sb_l, lg, sf_c, w_o, x, gp_l)
    return x, s_ctx


def _lru_layer(x, s_ctx, mods, w_in, conv_w, conv_b, w_a, b_a, w_x, b_x, lam, w_out):
    (mul_l, add_l, gp_l), (mul_c, add_c, gp_c) = mods
    b = x.shape[0]
    lc = s_ctx.shape[1]
    w = jnp.concatenate([w_in[:, :LRU_WIDTH], 0.5 * w_in[:, LRU_WIDTH:]], axis=1).astype(BF16)
    w_o = w_out.astype(BF16)
    cw = conv_w.astype(F32).reshape(CONV_W, LRU_BLOCKS, LRU_BW)
    cb = conv_b.astype(F32).reshape(LRU_BLOCKS, LRU_BW)
    kh = (0.25 * LRU_C) * jax.nn.softplus(-lam.astype(F32))

    def gate_params(d):
        wg = (0.5 * jnp.concatenate([w_a[d], w_x[d]], axis=-1)).astype(BF16)
        bg = (0.5 * jnp.stack([b_a[d], b_x[d]])).astype(F32).reshape(2, LRU_BLOCKS, LRU_BW)
        return wg, bg, kh[d].reshape(LRU_BLOCKS, LRU_BW)

    zero = jnp.zeros((b, LRU_BLOCKS, SUBLANES, LRU_BW), F32)
    xc_c, g_c, hb_c, hb_fin = _lru_inproj(s_ctx, mul_c, add_c, w, cw, cb, *gate_params(1), zero,
                                          tm=lc)
    xc_l, g_l, hb_l, _ = _lru_inproj(x, mul_l, add_l, w, cw, cb, *gate_params(1), hb_fin,
                                     tm=ROW_TILE)
    s_ctx, hf_fin = _lru_fwd(xc_c, *gate_params(0), zero, hb_c, g_c, w_o, s_ctx, gp_c, tt=lc)
    x, _ = _lru_fwd(xc_l, *gate_params(0), hf_fin, hb_l, g_l, w_o, x, gp_l, tt=ROW_TILE)
    return x, s_ctx


def kernel(x, c, ctx, c_ctx, mod_w, mod_b, norm_pre, norm_post, ret_w_in, ret_log_decay, ret_gn,
           ret_w_out, lru_w_in, lru_conv_w, lru_conv_b, lru_w_a, lru_b_a, lru_w_x, lru_b_x,
           lru_lambda, lru_w_out):
    b, n_tok, d = x.shape
    depth = mod_w.shape[0]
    assert n_tok % ROW_TILE == 0 and ROW_TILE % RET_CHUNK == 0
    assert ctx.shape[1] % RET_CHUNK == 0 and ctx.shape[1] <= ROW_TILE
    cos, sin = _rope_tables(n_tok)

    rows = -(-(b + 1) // SUBLANES) * SUBLANES
    cc = jnp.concatenate([c, c_ctx[None, :], jnp.zeros((rows - b - 1, d), F32)], axis=0)
    mod = _modulation(cc, mod_w, mod_b)

    s_ctx = ctx
    for i in range(depth):
        shift, scale, gate = jnp.split(mod[i], 3, axis=-1)
        mods = []
        for sel in (slice(0, b), slice(b, b + 1)):
            mul = jnp.broadcast_to(norm_pre[i] * (1.0 + scale[sel]), (b, d))[:, None, :]
            add = jnp.broadcast_to(shift[sel], (b, d))[:, None, :]
            gp = jnp.broadcast_to(gate[sel] * norm_post[i], (b, d))[:, None, :]
            mods.append((mul, add, gp))
        j = i // 2
        if i % 2 == 0:
            x, s_ctx = _retention_layer(x, s_ctx, mods, cos, sin, ret_w_in, j, ret_log_decay[j],
                                        ret_gn[j], ret_w_out[j])
        else:
            x, s_ctx = _lru_layer(x, s_ctx, mods, lru_w_in[j], lru_conv_w[j], lru_conv_b[j],
                                  lru_w_a[j], lru_b_a[j], lru_w_x[j], lru_b_x[j],
                                  lru_lambda[j], lru_w_out[j])
    return x
```

```python
import functools
from typing import NamedTuple, Optional

import jax
import jax.numpy as jnp
import numpy as np
from jax import lax
from jax.experimental import pallas as pl
from jax.experimental.pallas import tpu as pltpu

F32 = jnp.float32
BF16 = jnp.bfloat16

EPS = 1e-6
GRID_W = 64
ROPE_BASE = 10000.0

RET_HEADS = 4
RET_DK = 256
RET_DV = 512
RET_QK = RET_HEADS * RET_DK
RET_V = RET_HEADS * RET_DV
RET_IN = 2 * RET_QK + 2 * RET_V
HALF_DK = RET_DK // 2

LRU_BLOCKS = 10
LRU_BW = 128
LRU_WIDTH = LRU_BLOCKS * LRU_BW
CONV_W = 4
LRU_C = 8.0

LANES = 128
SUBLANES = 8
MIB = 1024 * 1024

RET_CHUNK = 256
RET_FWD_STEP_CHUNKS = 2
ROW_TILE = 512


def _params(n_axes, vmem_mib):
    return pltpu.CompilerParams(
        dimension_semantics=("arbitrary",) * n_axes,
        vmem_limit_bytes=vmem_mib * MIB)


def _smem_spec():
    return pl.BlockSpec(memory_space=pltpu.SMEM)


MOD_SHIFT, MOD_SCALE, MOD_GATE = 0, 1, 2


def _mod_kernel(cc_ref, w_ref, b_ref, pre_ref, post_ref, o_ref):
    j = pl.program_id(1)
    cc = cc_ref[...]
    act = cc * jax.nn.sigmoid(cc)
    val = jnp.dot(act.astype(BF16), w_ref[...].astype(BF16),
                  preferred_element_type=F32) + b_ref[...]
    o_ref[...] = jnp.where(j == MOD_SCALE, pre_ref[...] * (1.0 + val),
                           jnp.where(j == MOD_GATE, post_ref[...] * val, val))


def _modulation(cc, mod_w, mod_b, norm_pre, norm_post):
    depth, d, d3 = mod_w.shape
    rows = cc.shape[0]
    gain = pl.BlockSpec((None, 1, d), lambda i, j: (i, 0, 0))
    out = pl.pallas_call(
        _mod_kernel,
        grid=(depth, d3 // d),
        in_specs=[pl.BlockSpec((rows, d), lambda i, j: (0, 0)),
                  pl.BlockSpec((None, d, d), lambda i, j: (i, 0, j)),
                  pl.BlockSpec((None, 1, d), lambda i, j: (i, 0, j)),
                  gain, gain],
        out_specs=pl.BlockSpec((None, None, rows, d), lambda i, j: (i, j, 0, 0)),
        out_shape=jax.ShapeDtypeStruct((depth, d3 // d, rows, d), F32),
        compiler_params=_params(2, 32),
        name="modulation",
    )(cc, mod_w, mod_b.reshape(depth, 1, d3), norm_pre.astype(F32).reshape(depth, 1, d),
      norm_post.astype(F32).reshape(depth, 1, d))
    return out.reshape(depth, d3 // d, rows, 1, d)


class _ModVec(NamedTuple):
    arr: jax.Array
    layer: int
    slab: int
    row: Optional[int]


def _mod_spec(v):
    d = v.arr.shape[-1]
    if v.row is None:
        return pl.BlockSpec((None, None, None, 1, d), lambda i, j: (v.layer, v.slab, i, 0, 0))
    return pl.BlockSpec((None, None, None, 1, d), lambda i, j: (v.layer, v.slab, v.row, 0, 0))


def _norm_modulate(x_ref, mul_ref, add_ref, h_scr):
    x = x_ref[...]
    ms = jnp.mean(x * x, axis=-1, keepdims=True)
    h = x * lax.rsqrt(ms + EPS) * mul_ref[...] + add_ref[...]
    h_scr[...] = h.astype(BF16)


def _ret_inproj_kernel(lg_ref, x_ref, mul_ref, add_ref, w_ref, cos_ref, sin_ref, s0_ref,
                       o_ref, sb_ref, sfin_ref, h_scr, s_scr, dec_scr, *, rope, nch):
    t = pl.program_id(1)

    @pl.when(t == 0)
    def _():
        s_scr[...] = s0_ref[...]
        pos = _chunk_pos()
        for h in range(RET_HEADS):
            dec_scr[h] = jnp.exp(lg_ref[1, h] * pos).astype(BF16)

    _norm_modulate(x_ref, mul_ref, add_ref, h_scr)
    hb = h_scr[...]
    scale = RET_DK ** -0.5
    for j in range(2 * RET_HEADS):
        cols = slice(j * RET_DK, (j + 1) * RET_DK)
        acc = jnp.dot(hb, w_ref[:, cols], preferred_element_type=F32)
        if j >= RET_HEADS:
            acc = acc * scale
        if rope:
            e, o = acc[:, :HALF_DK], acc[:, HALF_DK:]
            c, s = cos_ref[...], sin_ref[...]
            acc = jnp.concatenate([e * c - o * s, e * s + o * c], axis=1)
        o_ref[:, cols] = acc.astype(BF16)
    for j in range(2 * RET_QK // RET_DV, RET_IN // RET_DV):
        cols = slice(j * RET_DV, (j + 1) * RET_DV)
        o_ref[:, cols] = jnp.dot(hb, w_ref[:, cols], preferred_element_type=F32).astype(BF16)

    for ci in reversed(range(nch)):
        rows = slice(ci * RET_CHUNK, (ci + 1) * RET_CHUNK)
        for h in range(RET_HEADS):
            k = o_ref[rows, RET_QK + h * RET_DK:RET_QK + (h + 1) * RET_DK]
            v = o_ref[rows, 2 * RET_QK + h * RET_DV:2 * RET_QK + (h + 1) * RET_DV]
            st = s_scr[h]
            sb_ref[ci, h] = st.astype(BF16)
            kv = lax.dot_general(_row_scale(k, dec_scr[h]), v, (((0,), (0,)), ((), ())),
                                 preferred_element_type=F32)
            s_scr[h] = st * jnp.exp(lg_ref[1, h] * RET_CHUNK) + kv

    @pl.when(t == pl.num_programs(1) - 1)
    def _():
        sfin_ref[...] = s_scr[...]


def _ret_inproj(x, mul, add, w, cos, sin, lg, s0, *, rope, tm):
    b, t, d = x.shape
    nt = t // tm
    nch = tm // RET_CHUNK
    kern = functools.partial(_ret_inproj_kernel, rope=rope, nch=nch)
    rev = lambda i, j: (i, nt - 1 - j, 0)
    return pl.pallas_call(
        kern,
        grid=(b, nt),
        in_specs=[_smem_spec(),
                  pl.BlockSpec((None, tm, d), rev),
                  _mod_spec(mul),
                  _mod_spec(add),
                  pl.BlockSpec((d, RET_IN), lambda i, j: (0, 0), pipeline_mode=pl.Buffered(1)),
                  pl.BlockSpec((tm, HALF_DK), lambda i, j: (nt - 1 - j, 0)),
                  pl.BlockSpec((tm, HALF_DK), lambda i, j: (nt - 1 - j, 0)),
                  _state_spec()],
        out_specs=[pl.BlockSpec((None, tm, RET_IN), rev),
                   pl.BlockSpec((None, nch, RET_HEADS, RET_DK, RET_DV),
                                lambda i, j: (i, nt - 1 - j, 0, 0, 0)),
                   _state_spec()],
        out_shape=[jax.ShapeDtypeStruct((b, t, RET_IN), BF16),
                   jax.ShapeDtypeStruct((b, t // RET_CHUNK, RET_HEADS, RET_DK, RET_DV), BF16),
                   jax.ShapeDtypeStruct(s0.shape, F32)],
        scratch_shapes=[pltpu.VMEM((tm, d), BF16),
                        pltpu.VMEM((RET_HEADS, RET_DK, RET_DV), F32),
                        pltpu.VMEM((RET_HEADS, RET_CHUNK, LANES), BF16)],
        compiler_params=_params(2, 56),
        name="ret_inproj_rope" if rope else "ret_inproj",
    )(lg, x, mul.arr, add.arr, w, cos, sin, s0)


def _store_interleaved(dst_ref, blk, val):
    seg = val.shape[0] // SUBLANES
    for s in range(SUBLANES):
        dst_ref[blk, pl.ds(s, seg, stride=SUBLANES), :] = val[s * seg:(s + 1) * seg, :]


def _lru_inproj_kernel(x_ref, xh_ref, mul_ref, add_ref, w_ref, cw_ref, cb_ref, wg_ref, bg_ref,
                       kh_ref, h0_ref, xc_ref, g_ref, hb_ref, hfin_ref,
                       h_scr, hs_scr, ext_scr, a_scr, u_scr, carry_scr):
    i = pl.program_id(1)
    has_prev = i < pl.num_programs(1) - 1
    has_next = i > 0
    tm, d = x_ref.shape
    halo_rows = xh_ref.shape[0]

    @pl.when(i == 0)
    def _():
        carry_scr[...] = h0_ref[...]

    def norm_modulate(x):
        ms = jnp.mean(x * x, axis=-1, keepdims=True)
        return x * lax.rsqrt(ms + EPS) * mul_ref[...] + add_ref[...]

    h = norm_modulate(x_ref[...])
    for s in range(d // LANES):
        _store_interleaved(hs_scr, s, h[:, s * LANES:(s + 1) * LANES])
    for s in range(d // LANES):
        h_scr[0:tm, s * LANES:(s + 1) * LANES] = hs_scr[s].astype(BF16)
    h_scr[tm:tm + halo_rows, :] = norm_modulate(xh_ref[...]).astype(BF16)

    sub = lax.broadcasted_iota(jnp.int32, (SUBLANES, LRU_BW), 0)
    left = CONV_W // 2
    n_chunk = LRU_WIDTH // (2 * LANES)

    def project(j):
        cols = slice(j * 2 * LANES, (j + 1) * 2 * LANES)
        gcols = slice(LRU_WIDTH + j * 2 * LANES, LRU_WIDTH + (j + 1) * 2 * LANES)
        return (jnp.dot(h_scr[...], w_ref[:, cols], preferred_element_type=F32),
                jnp.dot(h_scr[0:tm, :], w_ref[:, gcols], preferred_element_type=F32).astype(BF16))

    ahead = project(0)
    for j in range(n_chunk):
        acc, gacc = ahead
        if j + 1 < n_chunk:
            ahead = project(j + 1)
        for half in range(2):
            blk = 2 * j + half
            lanes = slice(half * LANES, (half + 1) * LANES)
            g_ref[blk] = gacc[:, lanes]
            ext = ext_scr.at[half]
            ext[left * SUBLANES:left * SUBLANES + tm, :] = acc[0:tm, lanes]
            for k in range(left):
                src = acc[tm - (left - k) * SUBLANES:tm - (left - k - 1) * SUBLANES, lanes]
                before = jnp.where(has_prev, acc[tm + k:tm + k + 1, lanes], 0.0)
                ext[k * SUBLANES:(k + 1) * SUBLANES, :] = jnp.where(
                    sub == 0, before, pltpu.roll(src, 1, 0))
            for k in range(CONV_W - 1 - left):
                src = acc[k * SUBLANES:(k + 1) * SUBLANES, lanes]
                after = jnp.where(has_next, acc[tm + left + k:tm + left + k + 1, lanes], 0.0)
                ext[(left + k) * SUBLANES + tm:(left + k + 1) * SUBLANES + tm, :] = jnp.where(
                    sub == SUBLANES - 1, after, pltpu.roll(src, SUBLANES - 1, 0))
            xc = cb_ref[blk:blk + 1, :]
            for tap in range(CONV_W):
                xc = xc + ext[tap * SUBLANES:tap * SUBLANES + tm, :] * cw_ref[tap, blk:blk + 1, :]
            xc_ref[blk] = xc
            _lru_coeff_block(blk, xc, wg_ref, bg_ref, kh_ref, a_scr, u_scr)

    _lru_scan(a_scr, u_scr, hb_ref, carry_scr, tm, True)

    @pl.when(i == pl.num_programs(1) - 1)
    def _():
        hfin_ref[...] = carry_scr[...]


def _conv_halo_rows(x, tm):
    b, t, d = x.shape
    nblk = t // tm
    r = x.reshape(b, nblk, tm, d)
    left = CONV_W // 2
    prev = jnp.concatenate([jnp.zeros((b, 1, left, d), F32), r[:, :-1, tm - left:, :]], axis=1)
    nxt = jnp.concatenate([r[:, 1:, 0:CONV_W - 1 - left, :],
                           jnp.zeros((b, 1, CONV_W - 1 - left, d), F32)], axis=1)
    pad = jnp.zeros((b, nblk, 2 * SUBLANES - (CONV_W - 1), d), F32)
    return jnp.concatenate([prev, nxt, pad], axis=2)


def _lru_inproj(x, mul, add, w, cw, cb, wg, bg, kh, h0, *, tm):
    b, t, d = x.shape
    halo_rows = 2 * SUBLANES
    nblk, slab, state, gate = _lru_specs(b, t, tm, True)
    tile = (LRU_BLOCKS, tm, LRU_BW)
    slab_f32 = jax.ShapeDtypeStruct((b, LRU_BLOCKS, t, LRU_BW), F32)
    return pl.pallas_call(
        _lru_inproj_kernel,
        grid=(b, nblk),
        in_specs=[pl.BlockSpec((None, tm, d), lambda i, j: (i, nblk - 1 - j, 0)),
                  pl.BlockSpec((None, None, halo_rows, d), lambda i, j: (i, nblk - 1 - j, 0, 0)),
                  _mod_spec(mul),
                  _mod_spec(add),
                  pl.BlockSpec((d, 2 * LRU_WIDTH), lambda i, j: (0, 0)),
                  pl.BlockSpec((CONV_W, LRU_BLOCKS, LRU_BW), lambda i, j: (0, 0, 0)),
                  pl.BlockSpec((LRU_BLOCKS, LRU_BW), lambda i, j: (0, 0))] + gate + [state],
        out_specs=[slab, slab, slab, state],
        out_shape=[slab_f32, jax.ShapeDtypeStruct((b, LRU_BLOCKS, t, LRU_BW), BF16), slab_f32,
                   jax.ShapeDtypeStruct(h0.shape, F32)],
        scratch_shapes=[pltpu.VMEM((tm + halo_rows, d), BF16),
                        pltpu.VMEM((d // LANES, tm, LANES), F32),
                        pltpu.VMEM((2, tm + (CONV_W - 1) * SUBLANES, LRU_BW), F32),
                        pltpu.VMEM(tile, F32), pltpu.VMEM(tile, F32),
                        pltpu.VMEM((LRU_BLOCKS, SUBLANES, LRU_BW), F32)],
        compiler_params=_params(2, 56),
        name="lru_inproj",
    )(x, _conv_halo_rows(x, tm), mul.arr, add.arr, w, cw, cb, wg, bg, kh, h0)


def _gated_norm_proj(y, w_ref, gp_ref):
    z = jnp.dot(y, w_ref[...], preferred_element_type=F32)
    ms = jnp.mean(z * z, axis=-1, keepdims=True)
    return z * lax.rsqrt(ms + EPS) * gp_ref[...]


def _residual_out(x_ref, zn, o_ref, rows):
    o_ref[rows, :] = x_ref[rows, :] + zn


def _residual_out_interleaved(x_ref, zn, o_ref, z_scr):
    n_slab = zn.shape[1] // LANES
    seg = zn.shape[0] // SUBLANES
    for d in range(n_slab):
        z_scr[d] = zn[:, d * LANES:(d + 1) * LANES]
    for d in range(n_slab):
        cols = slice(d * LANES, (d + 1) * LANES)
        for s in range(SUBLANES):
            rows = slice(s * seg, (s + 1) * seg)
            o_ref[rows, cols] = x_ref[rows, cols] + z_scr[d, pl.ds(s, seg, stride=SUBLANES), :]


def _row_scale(x, dec):
    n = x.shape[1] // LANES
    return jnp.concatenate([x[:, i * LANES:(i + 1) * LANES] * dec for i in range(n)], axis=1)


def _chunk_pos():
    return lax.broadcasted_iota(jnp.int32, (RET_CHUNK, LANES), 0).astype(F32)


def _ret_fwd_kernel(lg_ref, q_ref, k_ref, v_ref, g_ref, sb_ref, s0_ref, w_ref, x_ref, gp_ref,
                    xo_ref, sfin_ref, s_scr, m_scr, dec_scr, y_scr, *, nch):
    c = pl.program_id(1)

    @pl.when(c == 0)
    def _():
        s_scr[...] = s0_ref[...]
        ri = lax.broadcasted_iota(jnp.int32, (RET_CHUNK, RET_CHUNK), 0)
        ci = lax.broadcasted_iota(jnp.int32, (RET_CHUNK, RET_CHUNK), 1)
        rel = (ri - ci).astype(F32)
        for h in range(RET_HEADS):
            mf = jnp.where(rel >= 0, jnp.exp(lg_ref[0, h] * jnp.maximum(rel, 0.0)), 0.0)
            mb = jnp.where(rel <= 0, jnp.exp(lg_ref[1, h] * jnp.maximum(-rel, 0.0)), 0.0)
            m_scr[h] = mf + mb
        pos = _chunk_pos()
        for h in range(RET_HEADS):
            dec_scr[h, 0] = jnp.exp(lg_ref[0, h] * (pos + 1.0)).astype(BF16)
            dec_scr[h, 1] = jnp.exp(lg_ref[1, h] * (RET_CHUNK - pos)).astype(BF16)
            dec_scr[h, 2] = jnp.exp(lg_ref[0, h] * (RET_CHUNK - 1.0 - pos)).astype(BF16)

    for ci, h in [(ci, h) for ci in range(nch) for h in range(RET_HEADS)]:
        rows = slice(ci * RET_CHUNK, (ci + 1) * RET_CHUNK)
        vcols = slice(h * RET_DV, (h + 1) * RET_DV)
        q = q_ref[rows, h * RET_DK:(h + 1) * RET_DK]
        k = k_ref[rows, h * RET_DK:(h + 1) * RET_DK]
        v = v_ref[rows, vcols]
        st = s_scr[h]
        s = lax.dot_general(q, k, (((1,), (1,)), ((), ())), preferred_element_type=F32)
        o = jnp.dot((s * m_scr[h]).astype(BF16), v, preferred_element_type=F32)
        o = jnp.dot(_row_scale(q, dec_scr[h, 0]), st.astype(BF16), preferred_element_type=F32) + o
        o = jnp.dot(_row_scale(q, dec_scr[h, 1]), sb_ref[ci, h], preferred_element_type=F32) + o
        kv = lax.dot_general(_row_scale(k, dec_scr[h, 2]), v, (((0,), (0,)), ((), ())),
                             preferred_element_type=F32)
        s_scr[h] = st * jnp.exp(lg_ref[0, h] * RET_CHUNK) + kv
        dlt = o - jnp.mean(o, axis=-1, keepdims=True)
        var = jnp.mean(dlt * dlt, axis=-1, keepdims=True)
        gh = g_ref[rows, vcols]
        y_scr[rows, vcols] = ((dlt * lax.rsqrt(var + EPS)).astype(BF16)
                              * (gh * (jnp.tanh(gh) + 1.0)))
        if h == RET_HEADS - 1:
            _residual_out(x_ref, _gated_norm_proj(y_scr[rows, :], w_ref, gp_ref), xo_ref, rows)

    @pl.when(c == pl.num_programs(1) - 1)
    def _():
        sfin_ref[...] = s_scr[...]


def _state_spec():
    return pl.BlockSpec((None, RET_HEADS, RET_DK, RET_DV), lambda i, j: (i, 0, 0, 0))


def _step_chunks(n_chunks, wanted):
    return wanted if n_chunks % wanted == 0 else 1


def _ret_fwd(p, sb, lg, s0, w_out, x, gp):
    b, t, _ = p.shape
    d = x.shape[-1]
    nc = t // RET_CHUNK
    nch = _step_chunks(nc, RET_FWD_STEP_CHUNKS)
    rows = nch * RET_CHUNK
    fwd = lambda blk: (lambda i, j: (i, j, blk))
    resid = pl.BlockSpec((None, rows, d), fwd(0))
    return pl.pallas_call(
        functools.partial(_ret_fwd_kernel, nch=nch),
        grid=(b, nc // nch),
        in_specs=[_smem_spec(),
                  pl.BlockSpec((None, rows, RET_QK), fwd(0)),
                  pl.BlockSpec((None, rows, RET_QK), fwd(1)),
                  pl.BlockSpec((None, rows, RET_V), fwd(1)),
                  pl.BlockSpec((None, rows, RET_V), fwd(2)),
                  pl.BlockSpec((None, nch, RET_HEADS, RET_DK, RET_DV),
                               lambda i, j: (i, j, 0, 0, 0)),
                  _state_spec(),
                  pl.BlockSpec((RET_V, d), lambda i, j: (0, 0)),
                  resid,
                  _mod_spec(gp)],
        out_specs=[resid, _state_spec()],
        out_shape=[jax.ShapeDtypeStruct((b, t, d), F32),
                   jax.ShapeDtypeStruct(s0.shape, F32)],
        scratch_shapes=[pltpu.VMEM((RET_HEADS, RET_DK, RET_DV), F32),
                        pltpu.VMEM((RET_HEADS, RET_CHUNK, RET_CHUNK), F32),
                        pltpu.VMEM((RET_HEADS, 3, RET_CHUNK, LANES), BF16),
                        pltpu.VMEM((rows, RET_V), BF16)],
        compiler_params=_params(2, 56),
        name="ret_fwd",
    )(lg, p, p, p, p, sb, s0, w_out, x, gp.arr)


def _lru_coeff_block(nb, xc, wg_ref, bg_ref, kh_ref, a_scr, u_scr):
    res = jnp.dot(xc.astype(BF16), wg_ref[nb], preferred_element_type=F32)
    t_r = jnp.tanh(res[:, :LRU_BW] + bg_ref[0, nb:nb + 1, :])
    t_g = jnp.tanh(res[:, LRU_BW:] + bg_ref[1, nb:nb + 1, :])
    kh = kh_ref[nb:nb + 1, :]
    p = jnp.tanh(t_r * kh + kh)
    w = 1.0 / (1.0 + p)
    a_scr[nb] = (1.0 - p) * w
    sqrt_p = jnp.where(p > 0.0, p * lax.rsqrt(p), 0.0)
    u_scr[nb] = sqrt_p * w * (t_g + 1.0) * xc


def _lru_scan(a_scr, u_scr, dst_ref, carry_scr, tt, reverse):
    n = tt // SUBLANES
    shape = (LRU_BLOCKS, SUBLANES, LRU_BW)
    sub = lax.broadcasted_iota(jnp.int32, shape, 1)
    steps = [slice(g * SUBLANES, (g + 1) * SUBLANES) for g in range(n)]
    if reverse:
        steps = steps[::-1]

    a_seg = a_scr[:, steps[0], :]
    h_seg = u_scr[:, steps[0], :]
    for rows in steps[1:]:
        a = a_scr[:, rows, :]
        h_seg = a * h_seg + u_scr[:, rows, :]
        a_seg = a_seg * a
    for d in (1, 2, 4):
        ok = (sub < SUBLANES - d) if reverse else (sub >= d)
        shift = SUBLANES - d if reverse else d
        a_s = jnp.where(ok, pltpu.roll(a_seg, shift, 1), 1.0)
        h_s = jnp.where(ok, pltpu.roll(h_seg, shift, 1), 0.0)
        h_seg = a_seg * h_s + h_seg
        a_seg = a_seg * a_s
    carry = carry_scr[...]
    h_end = a_seg * carry + h_seg
    if reverse:
        h_in = jnp.where(sub == SUBLANES - 1, carry, pltpu.roll(h_end, SUBLANES - 1, 1))
        last = h_end[:, 0:1, :]
    else:
        h_in = jnp.where(sub == 0, carry, pltpu.roll(h_end, 1, 1))
        last = h_end[:, SUBLANES - 1:SUBLANES, :]
    carry_scr[...] = jnp.broadcast_to(last, shape)

    h = h_in
    for rows in steps:
        h = a_scr[:, rows, :] * h + u_scr[:, rows, :]
        dst_ref[:, rows, :] = h


def _lru_fwd_kernel(xc_ref, wg_ref, bg_ref, kh_ref, h0_ref, hb_ref, g_ref, w_ref, x_ref, gp_ref,
                    xo_ref, hfin_ref, a_scr, u_scr, carry_scr, y_scr, z_scr, *, tt):
    i = pl.program_id(1)

    @pl.when(i == 0)
    def _():
        carry_scr[...] = h0_ref[...]

    for nb in range(LRU_BLOCKS):
        _lru_coeff_block(nb, xc_ref[nb], wg_ref, bg_ref, kh_ref, a_scr, u_scr)
    _lru_scan(a_scr, u_scr, u_scr, carry_scr, tt, False)
    for nb in range(LRU_BLOCKS):
        gh = g_ref[nb]
        y = (u_scr[nb] + hb_ref[nb]).astype(BF16) * (gh * (jnp.tanh(gh) + 1.0))
        y_scr[:, nb * LRU_BW:(nb + 1) * LRU_BW] = y
    _residual_out_interleaved(x_ref, _gated_norm_proj(y_scr[...], w_ref, gp_ref), xo_ref, z_scr)

    @pl.when(i == pl.num_programs(1) - 1)
    def _():
        hfin_ref[...] = carry_scr[...]


def _lru_specs(b, t, tt, reverse):
    nblk = t // tt
    tidx = (lambda j: nblk - 1 - j) if reverse else (lambda j: j)
    slab = pl.BlockSpec((None, LRU_BLOCKS, tt, LRU_BW), lambda i, j: (i, 0, tidx(j), 0))
    state = pl.BlockSpec((None, LRU_BLOCKS, SUBLANES, LRU_BW), lambda i, j: (i, 0, 0, 0))
    gate = [pl.BlockSpec((LRU_BLOCKS, LRU_BW, 2 * LRU_BW), lambda i, j: (0, 0, 0)),
            pl.BlockSpec((2, LRU_BLOCKS, LRU_BW), lambda i, j: (0, 0, 0)),
            pl.BlockSpec((LRU_BLOCKS, LRU_BW), lambda i, j: (0, 0))]
    return nblk, slab, state, gate


def _lru_fwd(xc, wg, bg, kh, h0, hb, g, w_out, x, gp, *, tt):
    b, _, t, _ = xc.shape
    d = x.shape[-1]
    nblk, slab, state, gate = _lru_specs(b, t, tt, False)
    tile = (LRU_BLOCKS, tt, LRU_BW)
    resid = pl.BlockSpec((None, tt, d), lambda i, j: (i, j, 0))
    return pl.pallas_call(
        functools.partial(_lru_fwd_kernel, tt=tt),
        grid=(b, nblk),
        in_specs=[slab] + gate + [state, slab, slab,
                                  pl.BlockSpec((LRU_WIDTH, d), lambda i, j: (0, 0)),
                                  resid,
                                  _mod_spec(gp)],
        out_specs=[resid, state],
        out_shape=[jax.ShapeDtypeStruct((b, t, d), F32),
                   jax.ShapeDtypeStruct(h0.shape, F32)],
        scratch_shapes=[pltpu.VMEM(tile, F32), pltpu.VMEM(tile, F32),
                        pltpu.VMEM((LRU_BLOCKS, SUBLANES, LRU_BW), F32),
                        pltpu.VMEM((tt, LRU_WIDTH), BF16),
                        pltpu.VMEM((d // LANES, tt, LANES), F32)],
        compiler_params=_params(2, 56),
        name="lru_fwd",
    )(xc, wg, bg, kh, h0, hb, g, w_out, x, gp.arr)


def _rope_tables(n_tok):
    n_rows = n_tok // GRID_W
    row = jnp.repeat(jnp.arange(n_rows, dtype=F32), GRID_W)
    col = jnp.tile(jnp.arange(GRID_W, dtype=F32), n_rows)
    n_freq = RET_DK // 4
    inv = ROPE_BASE ** (-jnp.arange(n_freq, dtype=F32) / n_freq)
    ang = jnp.concatenate([row[:, None] * inv, col[:, None] * inv], axis=-1)
    return jnp.cos(ang), jnp.sin(ang)


def _ret_w_prep_kernel(w_ref, m_ref, o_ref):
    o_ref[...] = jnp.dot(w_ref[...].astype(BF16), m_ref[...],
                         preferred_element_type=F32).astype(BF16)


def _ret_w_prep(w_in_all, layer):
    d = w_in_all.shape[1]
    src = np.arange(RET_DK)
    perm = np.zeros((RET_DK, RET_DK), np.float32)
    perm[src, (src % 2) * HALF_DK + src // 2] = 1.0
    heads = RET_DV // RET_DK
    cols = heads * RET_DK
    eye = np.eye(cols, dtype=np.float32)
    mats = jnp.asarray(np.stack([np.kron(np.eye(heads, dtype=np.float32), perm), eye, 0.5 * eye]),
                       BF16)
    qk_blocks = 2 * RET_QK // cols
    v_blocks = RET_V // cols

    def which(j):
        return (j >= qk_blocks).astype(jnp.int32) + (j >= qk_blocks + v_blocks).astype(jnp.int32)

    return pl.pallas_call(
        _ret_w_prep_kernel,
        grid=(RET_IN // cols,),
        in_specs=[pl.BlockSpec((None, d, cols), lambda j: (layer, 0, j)),
                  pl.BlockSpec((None, cols, cols), lambda j: (which(j), 0, 0))],
        out_specs=pl.BlockSpec((d, cols), lambda j: (0, j)),
        out_shape=jax.ShapeDtypeStruct((d, RET_IN), BF16),
        compiler_params=_params(1, 32),
        name="ret_w_prep",
    )(w_in_all, mats)


def _retention_layer(x, s_ctx, mods, cos, sin, w_in_all, layer, log_decay, gn_g, w_out):
    (mul_l, add_l, gp_l), (mul_c, add_c, gp_c) = mods
    b = x.shape[0]
    lc = s_ctx.shape[1]
    w = _ret_w_prep(w_in_all, layer)
    lg = -jnp.abs(log_decay.astype(F32))
    w_o = (gn_g.astype(F32)[:, None] * w_out).astype(BF16)

    zero = jnp.zeros((b, RET_HEADS, RET_DK, RET_DV), F32)
    p_c, sb_c, sb_fin = _ret_inproj(s_ctx, mul_c, add_c, w, cos[:lc], sin[:lc], lg, zero,
                                    rope=False, tm=lc)
    p_l, sb_l, _ = _ret_inproj(x, mul_l, add_l, w, cos, sin, lg, sb_fin, rope=True, tm=ROW_TILE)
    s_ctx, sf_c = _ret_fwd(p_c, sb_c, lg, zero, w_o, s_ctx, gp_c)
    x, _ = _ret_fwd(p_l, sb_l, lg, sf_c, w_o, x, gp_l)
    return x, s_ctx


def _lru_layer(x, s_ctx, mods, w_in, conv_w, conv_b, w_a, b_a, w_x, b_x, lam, w_out):
    (mul_l, add_l, gp_l), (mul_c, add_c, gp_c) = mods
    b = x.shape[0]
    lc = s_ctx.shape[1]
    w = jnp.concatenate([w_in[:, :LRU_WIDTH], 0.5 * w_in[:, LRU_WIDTH:]], axis=1).astype(BF16)
    w_o = w_out.astype(BF16)
    cw = conv_w.astype(F32).reshape(CONV_W, LRU_BLOCKS, LRU_BW)
    cb = conv_b.astype(F32).reshape(LRU_BLOCKS, LRU_BW)
    kh = (0.25 * LRU_C) * jax.nn.softplus(-lam.astype(F32))

    def gate_params(d):
        wg = (0.5 * jnp.concatenate([w_a[d], w_x[d]], axis=-1)).astype(BF16)
        bg = (0.5 * jnp.stack([b_a[d], b_x[d]])).astype(F32).reshape(2, LRU_BLOCKS, LRU_BW)
        return wg, bg, kh[d].reshape(LRU_BLOCKS, LRU_BW)

    zero = jnp.zeros((b, LRU_BLOCKS, SUBLANES, LRU_BW), F32)
    xc_c, g_c, hb_c, hb_fin = _lru_inproj(s_ctx, mul_c, add_c, w, cw, cb, *gate_params(1), zero,
                                          tm=lc)
    xc_l, g_l, hb_l, _ = _lru_inproj(x, mul_l, add_l, w, cw, cb, *gate_params(1), hb_fin,
                                     tm=ROW_TILE)
    s_ctx, hf_fin = _lru_fwd(xc_c, *gate_params(0), zero, hb_c, g_c, w_o, s_ctx, gp_c, tt=lc)
    x, _ = _lru_fwd(xc_l, *gate_params(0), hf_fin, hb_l, g_l, w_o, x, gp_l, tt=ROW_TILE)
    return x, s_ctx


def kernel(x, c, ctx, c_ctx, mod_w, mod_b, norm_pre, norm_post, ret_w_in, ret_log_decay, ret_gn,
           ret_w_out, lru_w_in, lru_conv_w, lru_conv_b, lru_w_a, lru_b_a, lru_w_x, lru_b_x,
           lru_lambda, lru_w_out):
    b, n_tok, d = x.shape
    depth = mod_w.shape[0]
    assert n_tok % ROW_TILE == 0 and ROW_TILE % RET_CHUNK == 0
    assert ctx.shape[1] % RET_CHUNK == 0 and ctx.shape[1] <= ROW_TILE
    cos, sin = _rope_tables(n_tok)

    rows = -(-(b + 1) // SUBLANES) * SUBLANES
    cc = jnp.concatenate([c, c_ctx[None, :], jnp.zeros((rows - b - 1, d), F32)], axis=0)
    mod = _modulation(cc, mod_w, mod_b, norm_pre, norm_post)

    s_ctx = ctx
    for i in range(depth):
        mods = [tuple(_ModVec(mod, i, slab, row) for slab in (MOD_SCALE, MOD_SHIFT, MOD_GATE))
                for row in (None, b)]
        j = i // 2
        if i % 2 == 0:
            x, s_ctx = _retention_layer(x, s_ctx, mods, cos, sin, ret_w_in, j, ret_log_decay[j],
                                        ret_gn[j], ret_w_out[j])
        else:
            x, s_ctx = _lru_layer(x, s_ctx, mods, lru_w_in[j], lru_conv_w[j], lru_conv_b[j],
                                  lru_w_a[j], lru_b_a[j], lru_w_x[j], lru_b_x[j],
                                  lru_lambda[j], lru_w_out[j])
    return x
```

```python
import functools
from typing import NamedTuple, Optional

import jax
import jax.numpy as jnp
import numpy as np
from jax import lax
from jax.experimental import pallas as pl
from jax.experimental.pallas import tpu as pltpu

F32 = jnp.float32
BF16 = jnp.bfloat16

EPS = 1e-6
GRID_W = 64
ROPE_BASE = 10000.0

RET_HEADS = 4
RET_DK = 256
RET_DV = 512
RET_QK = RET_HEADS * RET_DK
RET_V = RET_HEADS * RET_DV
RET_IN = 2 * RET_QK + 2 * RET_V
HALF_DK = RET_DK // 2

LRU_BLOCKS = 10
LRU_BW = 128
LRU_WIDTH = LRU_BLOCKS * LRU_BW
CONV_W = 4
LRU_C = 8.0
LOG2_E = 1.4426950408889634

LANES = 128
SUBLANES = 8
SLAB_PAD_ROWS = 8
MIB = 1024 * 1024

RET_CHUNK = 256
RET_FWD_STEP_CHUNKS = 2
ROW_TILE = 512


def _params(n_axes, vmem_mib):
    return pltpu.CompilerParams(
        dimension_semantics=("arbitrary",) * n_axes,
        vmem_limit_bytes=vmem_mib * MIB)


def _smem_spec():
    return pl.BlockSpec(memory_space=pltpu.SMEM)


MOD_SHIFT, MOD_SCALE, MOD_GATE = 0, 1, 2


def _mod_kernel(cc_ref, w_ref, b_ref, pre_ref, post_ref, o_ref):
    j = pl.program_id(1)
    cc = cc_ref[...]
    act = cc * jax.nn.sigmoid(cc)
    val = jnp.dot(act.astype(BF16), w_ref[...].astype(BF16),
                  preferred_element_type=F32) + b_ref[...]
    o_ref[...] = jnp.where(j == MOD_SCALE, pre_ref[...] * (1.0 + val),
                           jnp.where(j == MOD_GATE, post_ref[...] * val, val))


def _modulation(cc, mod_w, mod_b, norm_pre, norm_post):
    depth, d, d3 = mod_w.shape
    rows = cc.shape[0]
    gain = pl.BlockSpec((None, 1, d), lambda i, j: (i, 0, 0))
    out = pl.pallas_call(
        _mod_kernel,
        grid=(depth, d3 // d),
        in_specs=[pl.BlockSpec((rows, d), lambda i, j: (0, 0)),
                  pl.BlockSpec((None, d, d), lambda i, j: (i, 0, j)),
                  pl.BlockSpec((None, 1, d), lambda i, j: (i, 0, j)),
                  gain, gain],
        out_specs=pl.BlockSpec((None, None, rows, d), lambda i, j: (i, j, 0, 0)),
        out_shape=jax.ShapeDtypeStruct((depth, d3 // d, rows, d), F32),
        compiler_params=_params(2, 32),
        name="modulation",
    )(cc, mod_w, mod_b.reshape(depth, 1, d3), norm_pre.astype(F32).reshape(depth, 1, d),
      norm_post.astype(F32).reshape(depth, 1, d))
    return out.reshape(depth, d3 // d, rows, 1, d)


class _ModVec(NamedTuple):
    arr: jax.Array
    layer: int
    slab: int
    row: Optional[int]


def _mod_spec(v):
    d = v.arr.shape[-1]
    if v.row is None:
        return pl.BlockSpec((None, None, None, 1, d), lambda i, j: (v.layer, v.slab, i, 0, 0))
    return pl.BlockSpec((None, None, None, 1, d), lambda i, j: (v.layer, v.slab, v.row, 0, 0))


def _norm_modulate(x_ref, mul_ref, add_ref, h_scr):
    x = x_ref[...]
    ms = jnp.mean(x * x, axis=-1, keepdims=True)
    h = x * lax.rsqrt(ms + EPS) * mul_ref[...] + add_ref[...]
    h_scr[...] = h.astype(BF16)


def _ret_inproj_kernel(lg_ref, x_ref, mul_ref, add_ref, w_ref, cos_ref, sin_ref, s0_ref,
                       o_ref, sb_ref, sfin_ref, h_scr, s_scr, dec_scr, *, rope, nch):
    t = pl.program_id(1)

    @pl.when(t == 0)
    def _():
        s_scr[...] = s0_ref[...]
        pos = _chunk_pos()
        for h in range(RET_HEADS):
            dec_scr[h] = jnp.exp(lg_ref[1, h] * pos).astype(BF16)

    _norm_modulate(x_ref, mul_ref, add_ref, h_scr)
    hb = h_scr[...]
    scale = RET_DK ** -0.5
    for j in range(2 * RET_HEADS):
        cols = slice(j * RET_DK, (j + 1) * RET_DK)
        acc = jnp.dot(hb, w_ref[:, cols], preferred_element_type=F32)
        if j >= RET_HEADS:
            acc = acc * scale
        if rope:
            e, o = acc[:, :HALF_DK], acc[:, HALF_DK:]
            c, s = cos_ref[...], sin_ref[...]
            acc = jnp.concatenate([e * c - o * s, e * s + o * c], axis=1)
        o_ref[:, cols] = acc.astype(BF16)
    for j in range(2 * RET_QK // RET_DV, RET_IN // RET_DV):
        cols = slice(j * RET_DV, (j + 1) * RET_DV)
        o_ref[:, cols] = jnp.dot(hb, w_ref[:, cols], preferred_element_type=F32).astype(BF16)

    for ci in reversed(range(nch)):
        rows = slice(ci * RET_CHUNK, (ci + 1) * RET_CHUNK)
        for h in range(RET_HEADS):
            k = o_ref[rows, RET_QK + h * RET_DK:RET_QK + (h + 1) * RET_DK]
            v = o_ref[rows, 2 * RET_QK + h * RET_DV:2 * RET_QK + (h + 1) * RET_DV]
            st = s_scr[h]
            sb_ref[ci, h] = st.astype(BF16)
            kv = lax.dot_general(_row_scale(k, dec_scr[h]), v, (((0,), (0,)), ((), ())),
                                 preferred_element_type=F32)
            s_scr[h] = st * jnp.exp(lg_ref[1, h] * RET_CHUNK) + kv

    @pl.when(t == pl.num_programs(1) - 1)
    def _():
        sfin_ref[...] = s_scr[...]


def _ret_inproj(x, mul, add, w, cos, sin, lg, s0, *, rope, tm):
    b, t, d = x.shape
    nt = t // tm
    nch = tm // RET_CHUNK
    kern = functools.partial(_ret_inproj_kernel, rope=rope, nch=nch)
    rev = lambda i, j: (i, nt - 1 - j, 0)
    return pl.pallas_call(
        kern,
        grid=(b, nt),
        in_specs=[_smem_spec(),
                  pl.BlockSpec((None, tm, d), rev),
                  _mod_spec(mul),
                  _mod_spec(add),
                  pl.BlockSpec((d, RET_IN), lambda i, j: (0, 0), pipeline_mode=pl.Buffered(1)),
                  pl.BlockSpec((tm, HALF_DK), lambda i, j: (nt - 1 - j, 0)),
                  pl.BlockSpec((tm, HALF_DK), lambda i, j: (nt - 1 - j, 0)),
                  _state_spec()],
        out_specs=[pl.BlockSpec((None, tm, RET_IN), rev),
                   pl.BlockSpec((None, nch, RET_HEADS, RET_DK, RET_DV),
                                lambda i, j: (i, nt - 1 - j, 0, 0, 0)),
                   _state_spec()],
        out_shape=[jax.ShapeDtypeStruct((b, t, RET_IN), BF16),
                   jax.ShapeDtypeStruct((b, t // RET_CHUNK, RET_HEADS, RET_DK, RET_DV), BF16),
                   jax.ShapeDtypeStruct(s0.shape, F32)],
        scratch_shapes=[pltpu.VMEM((tm, d), BF16),
                        pltpu.VMEM((RET_HEADS, RET_DK, RET_DV), F32),
                        pltpu.VMEM((RET_HEADS, RET_CHUNK, LANES), BF16)],
        compiler_params=_params(2, 56),
        name="ret_inproj_rope" if rope else "ret_inproj",
    )(lg, x, mul.arr, add.arr, w, cos, sin, s0)


def _store_interleaved(dst_ref, blk, val):
    seg = val.shape[0] // SUBLANES
    for s in range(SUBLANES):
        dst_ref[blk, pl.ds(s, seg, stride=SUBLANES), :] = val[s * seg:(s + 1) * seg, :]


def _lru_inproj_kernel(x_ref, xh_ref, mul_ref, add_ref, w_ref, cw_ref, cb_ref, wg_ref, bg_ref,
                       kh_ref, h0_ref, xc_ref, g_ref, hb_ref, hfin_ref,
                       h_scr, hs_scr, ext_scr, a_scr, u_scr, carry_scr):
    i = pl.program_id(1)
    has_prev = i < pl.num_programs(1) - 1
    has_next = i > 0
    tm, d = x_ref.shape
    halo_rows = xh_ref.shape[0]

    @pl.when(i == 0)
    def _():
        carry_scr[...] = h0_ref[...]

    def norm_modulate(x):
        ms = jnp.mean(x * x, axis=-1, keepdims=True)
        return x * lax.rsqrt(ms + EPS) * mul_ref[...] + add_ref[...]

    h = norm_modulate(x_ref[...])
    for s in range(d // LANES):
        _store_interleaved(hs_scr, s, h[:, s * LANES:(s + 1) * LANES])
    for s in range(d // LANES):
        h_scr[0:tm, s * LANES:(s + 1) * LANES] = hs_scr[s].astype(BF16)
    h_scr[tm:tm + halo_rows, :] = norm_modulate(xh_ref[...]).astype(BF16)

    sub = lax.broadcasted_iota(jnp.int32, (SUBLANES, LRU_BW), 0)
    left = CONV_W // 2
    n_chunk = LRU_WIDTH // (2 * LANES)

    def project(j):
        cols = slice(j * 2 * LANES, (j + 1) * 2 * LANES)
        gcols = slice(LRU_WIDTH + j * 2 * LANES, LRU_WIDTH + (j + 1) * 2 * LANES)
        return (jnp.dot(h_scr[...], w_ref[:, cols], preferred_element_type=F32),
                jnp.dot(h_scr[0:tm, :], w_ref[:, gcols], preferred_element_type=F32).astype(BF16))

    ahead = project(0)
    for j in range(n_chunk):
        acc, gacc = ahead
        if j + 1 < n_chunk:
            ahead = project(j + 1)
        for half in range(2):
            blk = 2 * j + half
            lanes = slice(half * LANES, (half + 1) * LANES)
            g_ref[blk] = gacc[:, lanes]
            ext = ext_scr.at[half]
            ext[left * SUBLANES:left * SUBLANES + tm, :] = acc[0:tm, lanes]
            for k in range(left):
                src = acc[tm - (left - k) * SUBLANES:tm - (left - k - 1) * SUBLANES, lanes]
                before = jnp.where(has_prev, acc[tm + k:tm + k + 1, lanes], 0.0)
                ext[k * SUBLANES:(k + 1) * SUBLANES, :] = jnp.where(
                    sub == 0, before, pltpu.roll(src, 1, 0))
            for k in range(CONV_W - 1 - left):
                src = acc[k * SUBLANES:(k + 1) * SUBLANES, lanes]
                after = jnp.where(has_next, acc[tm + left + k:tm + left + k + 1, lanes], 0.0)
                ext[(left + k) * SUBLANES + tm:(left + k + 1) * SUBLANES + tm, :] = jnp.where(
                    sub == SUBLANES - 1, after, pltpu.roll(src, SUBLANES - 1, 0))
            xc = cb_ref[blk:blk + 1, :]
            for tap in range(CONV_W):
                xc = xc + ext[tap * SUBLANES:tap * SUBLANES + tm, :] * cw_ref[tap, blk:blk + 1, :]
            xc_ref[blk] = xc
            _lru_coeff_block(blk, xc, wg_ref, bg_ref, kh_ref, a_scr, u_scr)

    _lru_scan(a_scr, u_scr, hb_ref, carry_scr, tm, True)

    @pl.when(i == pl.num_programs(1) - 1)
    def _():
        hfin_ref[...] = carry_scr[...]


def _conv_halo_rows(x, tm):
    b, t, d = x.shape
    nblk = t // tm
    r = x.reshape(b, nblk, tm, d)
    left = CONV_W // 2
    prev = jnp.concatenate([jnp.zeros((b, 1, left, d), F32), r[:, :-1, tm - left:, :]], axis=1)
    nxt = jnp.concatenate([r[:, 1:, 0:CONV_W - 1 - left, :],
                           jnp.zeros((b, 1, CONV_W - 1 - left, d), F32)], axis=1)
    pad = jnp.zeros((b, nblk, 2 * SUBLANES - (CONV_W - 1), d), F32)
    return jnp.concatenate([prev, nxt, pad], axis=2)


def _lru_inproj(x, mul, add, w, cw, cb, wg, bg, kh, h0, *, tm):
    b, t, d = x.shape
    halo_rows = 2 * SUBLANES
    nblk, slab, state, gate = _lru_specs(b, t, tm, True)
    tile = (LRU_BLOCKS, tm + SLAB_PAD_ROWS, LRU_BW)
    slab_f32 = jax.ShapeDtypeStruct((b, LRU_BLOCKS, t, LRU_BW), F32)
    return pl.pallas_call(
        _lru_inproj_kernel,
        grid=(b, nblk),
        in_specs=[pl.BlockSpec((None, tm, d), lambda i, j: (i, nblk - 1 - j, 0)),
                  pl.BlockSpec((None, None, halo_rows, d), lambda i, j: (i, nblk - 1 - j, 0, 0)),
                  _mod_spec(mul),
                  _mod_spec(add),
                  pl.BlockSpec((d, 2 * LRU_WIDTH), lambda i, j: (0, 0)),
                  pl.BlockSpec((CONV_W, LRU_BLOCKS, LRU_BW), lambda i, j: (0, 0, 0)),
                  pl.BlockSpec((LRU_BLOCKS, LRU_BW), lambda i, j: (0, 0))] + gate + [state],
        out_specs=[slab, slab, slab, state],
        out_shape=[slab_f32, jax.ShapeDtypeStruct((b, LRU_BLOCKS, t, LRU_BW), BF16), slab_f32,
                   jax.ShapeDtypeStruct(h0.shape, F32)],
        scratch_shapes=[pltpu.VMEM((tm + halo_rows, d), BF16),
                        pltpu.VMEM((d // LANES, tm, LANES), F32),
                        pltpu.VMEM((2, tm + (CONV_W - 1) * SUBLANES, LRU_BW), F32),
                        pltpu.VMEM(tile, F32), pltpu.VMEM(tile, F32),
                        pltpu.VMEM((LRU_BLOCKS, SUBLANES, LRU_BW), F32)],
        compiler_params=_params(2, 56),
        name="lru_inproj",
    )(x, _conv_halo_rows(x, tm), mul.arr, add.arr, w, cw, cb, wg, bg, kh, h0)


def _gated_norm_proj(y, w_ref, gp_ref):
    z = jnp.dot(y, w_ref[...], preferred_element_type=F32)
    ms = jnp.mean(z * z, axis=-1, keepdims=True)
    return z * lax.rsqrt(ms + EPS) * gp_ref[...]


def _residual_out(x_ref, zn, o_ref, rows):
    o_ref[rows, :] = x_ref[rows, :] + zn


def _residual_out_interleaved(x_ref, zn, o_ref, z_scr):
    n_slab = zn.shape[1] // LANES
    seg = zn.shape[0] // SUBLANES
    for d in range(n_slab):
        z_scr[d, 0:zn.shape[0], :] = zn[:, d * LANES:(d + 1) * LANES]
    for d in range(n_slab):
        cols = slice(d * LANES, (d + 1) * LANES)
        for s in range(SUBLANES):
            rows = slice(s * seg, (s + 1) * seg)
            o_ref[rows, cols] = x_ref[rows, cols] + z_scr[d, pl.ds(s, seg, stride=SUBLANES), :]


def _row_scale(x, dec):
    n = x.shape[1] // LANES
    return jnp.concatenate([x[:, i * LANES:(i + 1) * LANES] * dec for i in range(n)], axis=1)


def _chunk_pos():
    return lax.broadcasted_iota(jnp.int32, (RET_CHUNK, LANES), 0).astype(F32)


def _ret_fwd_kernel(lg_ref, q_ref, k_ref, v_ref, g_ref, sb_ref, s0_ref, w_ref, x_ref, gp_ref,
                    xo_ref, sfin_ref, s_scr, m_scr, dec_scr, y_scr, *, nch):
    c = pl.program_id(1)

    @pl.when(c == 0)
    def _():
        s_scr[...] = s0_ref[...]
        ri = lax.broadcasted_iota(jnp.int32, (RET_CHUNK, RET_CHUNK), 0)
        ci = lax.broadcasted_iota(jnp.int32, (RET_CHUNK, RET_CHUNK), 1)
        rel = (ri - ci).astype(F32)
        for h in range(RET_HEADS):
            mf = jnp.where(rel >= 0, jnp.exp(lg_ref[0, h] * jnp.maximum(rel, 0.0)), 0.0)
            mb = jnp.where(rel <= 0, jnp.exp(lg_ref[1, h] * jnp.maximum(-rel, 0.0)), 0.0)
            m_scr[h] = mf + mb
        pos = _chunk_pos()
        for h in range(RET_HEADS):
            dec_scr[h, 0] = jnp.exp(lg_ref[0, h] * (pos + 1.0)).astype(BF16)
            dec_scr[h, 1] = jnp.exp(lg_ref[1, h] * (RET_CHUNK - pos)).astype(BF16)
            dec_scr[h, 2] = jnp.exp(lg_ref[0, h] * (RET_CHUNK - 1.0 - pos)).astype(BF16)

    for ci, h in [(ci, h) for ci in range(nch) for h in range(RET_HEADS)]:
        rows = slice(ci * RET_CHUNK, (ci + 1) * RET_CHUNK)
        vcols = slice(h * RET_DV, (h + 1) * RET_DV)
        q = q_ref[rows, h * RET_DK:(h + 1) * RET_DK]
        k = k_ref[rows, h * RET_DK:(h + 1) * RET_DK]
        v = v_ref[rows, vcols]
        st = s_scr[h]
        s = lax.dot_general(q, k, (((1,), (1,)), ((), ())), preferred_element_type=F32)
        o = jnp.dot((s * m_scr[h]).astype(BF16), v, preferred_element_type=F32)
        o = jnp.dot(_row_scale(q, dec_scr[h, 0]), st.astype(BF16), preferred_element_type=F32) + o
        o = jnp.dot(_row_scale(q, dec_scr[h, 1]), sb_ref[ci, h], preferred_element_type=F32) + o
        kv = lax.dot_general(_row_scale(k, dec_scr[h, 2]), v, (((0,), (0,)), ((), ())),
                             preferred_element_type=F32)
        s_scr[h] = st * jnp.exp(lg_ref[0, h] * RET_CHUNK) + kv
        dlt = o - jnp.mean(o, axis=-1, keepdims=True)
        var = jnp.mean(dlt * dlt, axis=-1, keepdims=True)
        gh = g_ref[rows, vcols]
        y_scr[rows, vcols] = ((dlt * lax.rsqrt(var + EPS)).astype(BF16)
                              * (gh * (jnp.tanh(gh) + 1.0)))
        if h == RET_HEADS - 1:
            _residual_out(x_ref, _gated_norm_proj(y_scr[rows, :], w_ref, gp_ref), xo_ref, rows)

    @pl.when(c == pl.num_programs(1) - 1)
    def _():
        sfin_ref[...] = s_scr[...]


def _state_spec():
    return pl.BlockSpec((None, RET_HEADS, RET_DK, RET_DV), lambda i, j: (i, 0, 0, 0))


def _step_chunks(n_chunks, wanted):
    return wanted if n_chunks % wanted == 0 else 1


def _ret_fwd(p, sb, lg, s0, w_out, x, gp):
    b, t, _ = p.shape
    d = x.shape[-1]
    nc = t // RET_CHUNK
    nch = _step_chunks(nc, RET_FWD_STEP_CHUNKS)
    rows = nch * RET_CHUNK
    fwd = lambda blk: (lambda i, j: (i, j, blk))
    resid = pl.BlockSpec((None, rows, d), fwd(0))
    return pl.pallas_call(
        functools.partial(_ret_fwd_kernel, nch=nch),
        grid=(b, nc // nch),
        in_specs=[_smem_spec(),
                  pl.BlockSpec((None, rows, RET_QK), fwd(0)),
                  pl.BlockSpec((None, rows, RET_QK), fwd(1)),
                  pl.BlockSpec((None, rows, RET_V), fwd(1)),
                  pl.BlockSpec((None, rows, RET_V), fwd(2)),
                  pl.BlockSpec((None, nch, RET_HEADS, RET_DK, RET_DV),
                               lambda i, j: (i, j, 0, 0, 0)),
                  _state_spec(),
                  pl.BlockSpec((RET_V, d), lambda i, j: (0, 0)),
                  resid,
                  _mod_spec(gp)],
        out_specs=[resid, _state_spec()],
        out_shape=[jax.ShapeDtypeStruct((b, t, d), F32),
                   jax.ShapeDtypeStruct(s0.shape, F32)],
        scratch_shapes=[pltpu.VMEM((RET_HEADS, RET_DK, RET_DV), F32),
                        pltpu.VMEM((RET_HEADS, RET_CHUNK, RET_CHUNK), F32),
                        pltpu.VMEM((RET_HEADS, 3, RET_CHUNK, LANES), BF16),
                        pltpu.VMEM((rows, RET_V), BF16)],
        compiler_params=_params(2, 56),
        name="ret_fwd",
    )(lg, p, p, p, p, sb, s0, w_out, x, gp.arr)


def _lru_coeff_block(nb, xc, wg_ref, bg_ref, kh_ref, a_scr, u_scr):
    res = jnp.dot(xc.astype(BF16), wg_ref[nb], preferred_element_type=F32)
    t_r = jnp.tanh(res[:, :LRU_BW] + bg_ref[0, nb:nb + 1, :])
    t_g = jnp.tanh(res[:, LRU_BW:] + bg_ref[1, nb:nb + 1, :])
    kh = kh_ref[nb:nb + 1, :]
    a = jnp.exp2(t_r * kh + kh)
    tt = xc.shape[0]
    a_scr[nb, 0:tt, :] = a
    om = 0.25 - 0.25 * (a * a)
    half_root = jnp.where(om > 0.0, om * lax.rsqrt(om), 0.0)
    u_scr[nb, 0:tt, :] = half_root * (t_g + 1.0) * xc


def _lru_scan(a_scr, u_scr, dst_ref, carry_scr, tt, reverse):
    n = tt // SUBLANES
    shape = (LRU_BLOCKS, SUBLANES, LRU_BW)
    sub = lax.broadcasted_iota(jnp.int32, shape, 1)
    steps = [slice(g * SUBLANES, (g + 1) * SUBLANES) for g in range(n)]
    if reverse:
        steps = steps[::-1]

    a_seg = a_scr[:, steps[0], :]
    h_seg = u_scr[:, steps[0], :]
    for rows in steps[1:]:
        a = a_scr[:, rows, :]
        h_seg = a * h_seg + u_scr[:, rows, :]
        a_seg = a_seg * a
    for d in (1, 2, 4):
        ok = (sub < SUBLANES - d) if reverse else (sub >= d)
        shift = SUBLANES - d if reverse else d
        a_s = jnp.where(ok, pltpu.roll(a_seg, shift, 1), 1.0)
        h_s = jnp.where(ok, pltpu.roll(h_seg, shift, 1), 0.0)
        h_seg = a_seg * h_s + h_seg
        a_seg = a_seg * a_s
    carry = carry_scr[...]
    h_end = a_seg * carry + h_seg
    if reverse:
        h_in = jnp.where(sub == SUBLANES - 1, carry, pltpu.roll(h_end, SUBLANES - 1, 1))
        last = h_end[:, 0:1, :]
    else:
        h_in = jnp.where(sub == 0, carry, pltpu.roll(h_end, 1, 1))
        last = h_end[:, SUBLANES - 1:SUBLANES, :]
    carry_scr[...] = jnp.broadcast_to(last, shape)

    h = h_in
    for rows in steps:
        h = a_scr[:, rows, :] * h + u_scr[:, rows, :]
        dst_ref[:, rows, :] = h


def _lru_fwd_kernel(xc_ref, wg_ref, bg_ref, kh_ref, h0_ref, hb_ref, g_ref, w_ref, x_ref, gp_ref,
                    xo_ref, hfin_ref, a_scr, u_scr, carry_scr, y_scr, z_scr, *, tt):
    i = pl.program_id(1)

    @pl.when(i == 0)
    def _():
        carry_scr[...] = h0_ref[...]

    for nb in range(LRU_BLOCKS):
        _lru_coeff_block(nb, xc_ref[nb], wg_ref, bg_ref, kh_ref, a_scr, u_scr)
    _lru_scan(a_scr, u_scr, u_scr, carry_scr, tt, False)
    for nb in range(LRU_BLOCKS):
        gh = g_ref[nb]
        y = (u_scr[nb, 0:tt, :] + hb_ref[nb]).astype(BF16) * (gh * (jnp.tanh(gh) + 1.0))
        y_scr[:, nb * LRU_BW:(nb + 1) * LRU_BW] = y
    _residual_out_interleaved(x_ref, _gated_norm_proj(y_scr[...], w_ref, gp_ref), xo_ref, z_scr)

    @pl.when(i == pl.num_programs(1) - 1)
    def _():
        hfin_ref[...] = carry_scr[...]


def _lru_specs(b, t, tt, reverse):
    nblk = t // tt
    tidx = (lambda j: nblk - 1 - j) if reverse else (lambda j: j)
    slab = pl.BlockSpec((None, LRU_BLOCKS, tt, LRU_BW), lambda i, j: (i, 0, tidx(j), 0))
    state = pl.BlockSpec((None, LRU_BLOCKS, SUBLANES, LRU_BW), lambda i, j: (i, 0, 0, 0))
    gate = [pl.BlockSpec((LRU_BLOCKS, LRU_BW, 2 * LRU_BW), lambda i, j: (0, 0, 0)),
            pl.BlockSpec((2, LRU_BLOCKS, LRU_BW), lambda i, j: (0, 0, 0)),
            pl.BlockSpec((LRU_BLOCKS, LRU_BW), lambda i, j: (0, 0))]
    return nblk, slab, state, gate


def _lru_fwd(xc, wg, bg, kh, h0, hb, g, w_out, x, gp, *, tt):
    b, _, t, _ = xc.shape
    d = x.shape[-1]
    nblk, slab, state, gate = _lru_specs(b, t, tt, False)
    tile = (LRU_BLOCKS, tt + SLAB_PAD_ROWS, LRU_BW)
    resid = pl.BlockSpec((None, tt, d), lambda i, j: (i, j, 0))
    return pl.pallas_call(
        functools.partial(_lru_fwd_kernel, tt=tt),
        grid=(b, nblk),
        in_specs=[slab] + gate + [state, slab, slab,
                                  pl.BlockSpec((LRU_WIDTH, d), lambda i, j: (0, 0)),
                                  resid,
                                  _mod_spec(gp)],
        out_specs=[resid, state],
        out_shape=[jax.ShapeDtypeStruct((b, t, d), F32),
                   jax.ShapeDtypeStruct(h0.shape, F32)],
        scratch_shapes=[pltpu.VMEM(tile, F32), pltpu.VMEM(tile, F32),
                        pltpu.VMEM((LRU_BLOCKS, SUBLANES, LRU_BW), F32),
                        pltpu.VMEM((tt, LRU_WIDTH), BF16),
                        pltpu.VMEM((d // LANES, tt + SLAB_PAD_ROWS, LANES), F32)],
        compiler_params=_params(2, 56),
        name="lru_fwd",
    )(xc, wg, bg, kh, h0, hb, g, w_out, x, gp.arr)


def _rope_tables(n_tok):
    n_rows = n_tok // GRID_W
    row = jnp.repeat(jnp.arange(n_rows, dtype=F32), GRID_W)
    col = jnp.tile(jnp.arange(GRID_W, dtype=F32), n_rows)
    n_freq = RET_DK // 4
    inv = ROPE_BASE ** (-jnp.arange(n_freq, dtype=F32) / n_freq)
    ang = jnp.concatenate([row[:, None] * inv, col[:, None] * inv], axis=-1)
    return jnp.cos(ang), jnp.sin(ang)


def _ret_w_prep_kernel(w_ref, m_ref, o_ref):
    o_ref[...] = jnp.dot(w_ref[...].astype(BF16), m_ref[...],
                         preferred_element_type=F32).astype(BF16)


def _ret_w_prep(w_in_all, layer):
    d = w_in_all.shape[1]
    src = np.arange(RET_DK)
    perm = np.zeros((RET_DK, RET_DK), np.float32)
    perm[src, (src % 2) * HALF_DK + src // 2] = 1.0
    heads = RET_DV // RET_DK
    cols = heads * RET_DK
    eye = np.eye(cols, dtype=np.float32)
    mats = jnp.asarray(np.stack([np.kron(np.eye(heads, dtype=np.float32), perm), eye, 0.5 * eye]),
                       BF16)
    qk_blocks = 2 * RET_QK // cols
    v_blocks = RET_V // cols

    def which(j):
        return (j >= qk_blocks).astype(jnp.int32) + (j >= qk_blocks + v_blocks).astype(jnp.int32)

    return pl.pallas_call(
        _ret_w_prep_kernel,
        grid=(RET_IN // cols,),
        in_specs=[pl.BlockSpec((None, d, cols), lambda j: (layer, 0, j)),
                  pl.BlockSpec((None, cols, cols), lambda j: (which(j), 0, 0))],
        out_specs=pl.BlockSpec((d, cols), lambda j: (0, j)),
        out_shape=jax.ShapeDtypeStruct((d, RET_IN), BF16),
        compiler_params=_params(1, 32),
        name="ret_w_prep",
    )(w_in_all, mats)


def _retention_layer(x, s_ctx, mods, cos, sin, w_in_all, layer, log_decay, gn_g, w_out):
    (mul_l, add_l, gp_l), (mul_c, add_c, gp_c) = mods
    b = x.shape[0]
    lc = s_ctx.shape[1]
    w = _ret_w_prep(w_in_all, layer)
    lg = -jnp.abs(log_decay.astype(F32))
    w_o = (gn_g.astype(F32)[:, None] * w_out).astype(BF16)

    zero = jnp.zeros((b, RET_HEADS, RET_DK, RET_DV), F32)
    p_c, sb_c, sb_fin = _ret_inproj(s_ctx, mul_c, add_c, w, cos[:lc], sin[:lc], lg, zero,
                                    rope=False, tm=lc)
    p_l, sb_l, _ = _ret_inproj(x, mul_l, add_l, w, cos, sin, lg, sb_fin, rope=True, tm=ROW_TILE)
    s_ctx, sf_c = _ret_fwd(p_c, sb_c, lg, zero, w_o, s_ctx, gp_c)
    x, _ = _ret_fwd(p_l, sb_l, lg, sf_c, w_o, x, gp_l)
    return x, s_ctx


def _lru_layer(x, s_ctx, mods, w_in, conv_w, conv_b, w_a, b_a, w_x, b_x, lam, w_out):
    (mul_l, add_l, gp_l), (mul_c, add_c, gp_c) = mods
    b = x.shape[0]
    lc = s_ctx.shape[1]
    w = jnp.concatenate([w_in[:, :LRU_WIDTH], 0.5 * w_in[:, LRU_WIDTH:]], axis=1).astype(BF16)
    w_o = w_out.astype(BF16)
    cw = conv_w.astype(F32).reshape(CONV_W, LRU_BLOCKS, LRU_BW)
    cb = conv_b.astype(F32).reshape(LRU_BLOCKS, LRU_BW)
    kh = (-0.5 * LRU_C * LOG2_E) * jax.nn.softplus(-lam.astype(F32))

    def gate_params(d):
        wg = (0.5 * jnp.concatenate([w_a[d], w_x[d]], axis=-1)).astype(BF16)
        bg = (0.5 * jnp.stack([b_a[d], b_x[d]])).astype(F32).reshape(2, LRU_BLOCKS, LRU_BW)
        return wg, bg, kh[d].reshape(LRU_BLOCKS, LRU_BW)

    zero = jnp.zeros((b, LRU_BLOCKS, SUBLANES, LRU_BW), F32)
    xc_c, g_c, hb_c, hb_fin = _lru_inproj(s_ctx, mul_c, add_c, w, cw, cb, *gate_params(1), zero,
                                          tm=lc)
    xc_l, g_l, hb_l, _ = _lru_inproj(x, mul_l, add_l, w, cw, cb, *gate_params(1), hb_fin,
                                     tm=ROW_TILE)
    s_ctx, hf_fin = _lru_fwd(xc_c, *gate_params(0), zero, hb_c, g_c, w_o, s_ctx, gp_c, tt=lc)
    x, _ = _lru_fwd(xc_l, *gate_params(0), hf_fin, hb_l, g_l, w_o, x, gp_l, tt=ROW_TILE)
    return x, s_ctx


def kernel(x, c, ctx, c_ctx, mod_w, mod_b, norm_pre, norm_post, ret_w_in, ret_log_decay, ret_gn,
           ret_w_out, lru_w_in, lru_conv_w, lru_conv_b, lru_w_a, lru_b_a, lru_w_x, lru_b_x,
           lru_lambda, lru_w_out):
    b, n_tok, d = x.shape
    depth = mod_w.shape[0]
    assert n_tok % ROW_TILE == 0 and ROW_TILE % RET_CHUNK == 0
    assert ctx.shape[1] % RET_CHUNK == 0 and ctx.shape[1] <= ROW_TILE
    cos, sin = _rope_tables(n_tok)

    rows = -(-(b + 1) // SUBLANES) * SUBLANES
    cc = jnp.concatenate([c, c_ctx[None, :], jnp.zeros((rows - b - 1, d), F32)], axis=0)
    mod = _modulation(cc, mod_w, mod_b, norm_pre, norm_post)

    s_ctx = ctx
    for i in range(depth):
        mods = [tuple(_ModVec(mod, i, slab, row) for slab in (MOD_SCALE, MOD_SHIFT, MOD_GATE))
                for row in (None, b)]
        j = i // 2
        if i % 2 == 0:
            x, s_ctx = _retention_layer(x, s_ctx, mods, cos, sin, ret_w_in, j, ret_log_decay[j],
                                        ret_gn[j], ret_w_out[j])
        else:
            x, s_ctx = _lru_layer(x, s_ctx, mods, lru_w_in[j], lru_conv_w[j], lru_conv_b[j],
                                  lru_w_a[j], lru_b_a[j], lru_w_x[j], lru_b_x[j],
                                  lru_lambda[j], lru_w_out[j])
    return x
```

```python
import functools
from typing import NamedTuple, Optional

import jax
import jax.numpy as jnp
import numpy as np
from jax import lax
from jax.experimental import pallas as pl
from jax.experimental.pallas import tpu as pltpu

F32 = jnp.float32
BF16 = jnp.bfloat16

EPS = 1e-6
GRID_W = 64
ROPE_BASE = 10000.0

RET_HEADS = 4
RET_DK = 256
RET_DV = 512
RET_QK = RET_HEADS * RET_DK
RET_V = RET_HEADS * RET_DV
RET_IN = 2 * RET_QK + 2 * RET_V
HALF_DK = RET_DK // 2

LRU_BLOCKS = 10
LRU_BW = 128
LRU_WIDTH = LRU_BLOCKS * LRU_BW
CONV_W = 4
LRU_C = 8.0
LOG2_E = 1.4426950408889634

LANES = 128
SUBLANES = 8
MIB = 1024 * 1024

RET_CHUNK = 256
RET_FWD_STEP_CHUNKS = 2
ROW_TILE = 512


def _params(n_axes, vmem_mib):
    return pltpu.CompilerParams(
        dimension_semantics=("arbitrary",) * n_axes,
        vmem_limit_bytes=vmem_mib * MIB)


def _smem_spec():
    return pl.BlockSpec(memory_space=pltpu.SMEM)


MOD_SHIFT, MOD_SCALE, MOD_GATE = 0, 1, 2


def _mod_kernel(cc_ref, w_ref, b_ref, pre_ref, post_ref, o_ref):
    j = pl.program_id(1)
    cc = cc_ref[...]
    act = cc * jax.nn.sigmoid(cc)
    val = jnp.dot(act.astype(BF16), w_ref[...].astype(BF16),
                  preferred_element_type=F32) + b_ref[...]
    o_ref[...] = jnp.where(j == MOD_SCALE, pre_ref[...] * (1.0 + val),
                           jnp.where(j == MOD_GATE, post_ref[...] * val, val))


def _modulation(cc, mod_w, mod_b, norm_pre, norm_post):
    depth, d, d3 = mod_w.shape
    rows = cc.shape[0]
    gain = pl.BlockSpec((None, 1, d), lambda i, j: (i, 0, 0))
    out = pl.pallas_call(
        _mod_kernel,
        grid=(depth, d3 // d),
        in_specs=[pl.BlockSpec((rows, d), lambda i, j: (0, 0)),
                  pl.BlockSpec((None, d, d), lambda i, j: (i, 0, j)),
                  pl.BlockSpec((None, 1, d), lambda i, j: (i, 0, j)),
                  gain, gain],
        out_specs=pl.BlockSpec((None, None, rows, d), lambda i, j: (i, j, 0, 0)),
        out_shape=jax.ShapeDtypeStruct((depth, d3 // d, rows, d), F32),
        compiler_params=_params(2, 32),
        name="modulation",
    )(cc, mod_w, mod_b.reshape(depth, 1, d3), norm_pre.astype(F32).reshape(depth, 1, d),
      norm_post.astype(F32).reshape(depth, 1, d))
    return out.reshape(depth, d3 // d, rows, 1, d)


class _ModVec(NamedTuple):
    arr: jax.Array
    layer: int
    slab: int
    row: Optional[int]


def _mod_spec(v):
    d = v.arr.shape[-1]
    if v.row is None:
        return pl.BlockSpec((None, None, None, 1, d), lambda i, j: (v.layer, v.slab, i, 0, 0))
    return pl.BlockSpec((None, None, None, 1, d), lambda i, j: (v.layer, v.slab, v.row, 0, 0))


def _norm_modulate(x_ref, mul_ref, add_ref, h_scr):
    x = x_ref[...]
    ms = jnp.mean(x * x, axis=-1, keepdims=True)
    h = x * lax.rsqrt(ms + EPS) * mul_ref[...] + add_ref[...]
    h_scr[...] = h.astype(BF16)


def _ret_inproj_kernel(lg_ref, x_ref, mul_ref, add_ref, w_ref, cos_ref, sin_ref, s0_ref,
                       o_ref, sb_ref, sfin_ref, h_scr, s_scr, dec_scr, *, rope, nch):
    t = pl.program_id(1)

    @pl.when(t == 0)
    def _():
        s_scr[...] = s0_ref[...]
        pos = _chunk_pos()
        for h in range(RET_HEADS):
            dec_scr[h] = jnp.exp(lg_ref[1, h] * pos).astype(BF16)

    _norm_modulate(x_ref, mul_ref, add_ref, h_scr)
    hb = h_scr[...]
    scale = RET_DK ** -0.5
    for j in range(2 * RET_HEADS):
        cols = slice(j * RET_DK, (j + 1) * RET_DK)
        acc = jnp.dot(hb, w_ref[:, cols], preferred_element_type=F32)
        if j >= RET_HEADS:
            acc = acc * scale
        if rope:
            e, o = acc[:, :HALF_DK], acc[:, HALF_DK:]
            c, s = cos_ref[...], sin_ref[...]
            acc = jnp.concatenate([e * c - o * s, e * s + o * c], axis=1)
        o_ref[:, cols] = acc.astype(BF16)
    for j in range(2 * RET_QK // RET_DV, RET_IN // RET_DV):
        cols = slice(j * RET_DV, (j + 1) * RET_DV)
        o_ref[:, cols] = jnp.dot(hb, w_ref[:, cols], preferred_element_type=F32).astype(BF16)

    for ci in reversed(range(nch)):
        rows = slice(ci * RET_CHUNK, (ci + 1) * RET_CHUNK)
        for h in range(RET_HEADS):
            k = o_ref[rows, RET_QK + h * RET_DK:RET_QK + (h + 1) * RET_DK]
            v = o_ref[rows, 2 * RET_QK + h * RET_DV:2 * RET_QK + (h + 1) * RET_DV]
            st = s_scr[h]
            sb_ref[ci, h] = st.astype(BF16)
            kv = lax.dot_general(_row_scale(k, dec_scr[h]), v, (((0,), (0,)), ((), ())),
                                 preferred_element_type=F32)
            s_scr[h] = st * jnp.exp(lg_ref[1, h] * RET_CHUNK) + kv

    @pl.when(t == pl.num_programs(1) - 1)
    def _():
        sfin_ref[...] = s_scr[...]


def _ret_inproj(x, mul, add, w, cos, sin, lg, s0, *, rope, tm):
    b, t, d = x.shape
    nt = t // tm
    nch = tm // RET_CHUNK
    kern = functools.partial(_ret_inproj_kernel, rope=rope, nch=nch)
    rev = lambda i, j: (i, nt - 1 - j, 0)
    return pl.pallas_call(
        kern,
        grid=(b, nt),
        in_specs=[_smem_spec(),
                  pl.BlockSpec((None, tm, d), rev),
                  _mod_spec(mul),
                  _mod_spec(add),
                  pl.BlockSpec((d, RET_IN), lambda i, j: (0, 0), pipeline_mode=pl.Buffered(1)),
                  pl.BlockSpec((tm, HALF_DK), lambda i, j: (nt - 1 - j, 0)),
                  pl.BlockSpec((tm, HALF_DK), lambda i, j: (nt - 1 - j, 0)),
                  _state_spec()],
        out_specs=[pl.BlockSpec((None, tm, RET_IN), rev),
                   pl.BlockSpec((None, nch, RET_HEADS, RET_DK, RET_DV),
                                lambda i, j: (i, nt - 1 - j, 0, 0, 0)),
                   _state_spec()],
        out_shape=[jax.ShapeDtypeStruct((b, t, RET_IN), BF16),
                   jax.ShapeDtypeStruct((b, t // RET_CHUNK, RET_HEADS, RET_DK, RET_DV), BF16),
                   jax.ShapeDtypeStruct(s0.shape, F32)],
        scratch_shapes=[pltpu.VMEM((tm, d), BF16),
                        pltpu.VMEM((RET_HEADS, RET_DK, RET_DV), F32),
                        pltpu.VMEM((RET_HEADS, RET_CHUNK, LANES), BF16)],
        compiler_params=_params(2, 56),
        name="ret_inproj_rope" if rope else "ret_inproj",
    )(lg, x, mul.arr, add.arr, w, cos, sin, s0)


def _store_interleaved(dst_ref, blk, val):
    seg = val.shape[0] // SUBLANES
    for s in range(SUBLANES):
        dst_ref[blk, pl.ds(s, seg, stride=SUBLANES), :] = val[s * seg:(s + 1) * seg, :]


def _lru_inproj_kernel(x_ref, xh_ref, mul_ref, add_ref, w_ref, cw_ref, cb_ref, wg_ref, bg_ref,
                       kh_ref, h0_ref, xc_ref, g_ref, hb_ref, hfin_ref,
                       h_scr, hs_scr, ext_scr, a_scr, u_scr, carry_scr):
    i = pl.program_id(1)
    has_prev = i < pl.num_programs(1) - 1
    has_next = i > 0
    tm, d = x_ref.shape
    halo_rows = xh_ref.shape[0]

    @pl.when(i == 0)
    def _():
        carry_scr[...] = h0_ref[...]

    def norm_modulate(x):
        ms = jnp.mean(x * x, axis=-1, keepdims=True)
        return x * lax.rsqrt(ms + EPS) * mul_ref[...] + add_ref[...]

    h = norm_modulate(x_ref[...])
    for s in range(d // LANES):
        _store_interleaved(hs_scr, s, h[:, s * LANES:(s + 1) * LANES])
    for s in range(d // LANES):
        h_scr[0:tm, s * LANES:(s + 1) * LANES] = hs_scr[s].astype(BF16)
    h_scr[tm:tm + halo_rows, :] = norm_modulate(xh_ref[...]).astype(BF16)

    sub = lax.broadcasted_iota(jnp.int32, (SUBLANES, LRU_BW), 0)
    left = CONV_W // 2
    n_chunk = LRU_WIDTH // (2 * LANES)

    def project(j):
        cols = slice(j * 2 * LANES, (j + 1) * 2 * LANES)
        gcols = slice(LRU_WIDTH + j * 2 * LANES, LRU_WIDTH + (j + 1) * 2 * LANES)
        return (jnp.dot(h_scr[...], w_ref[:, cols], preferred_element_type=F32),
                jnp.dot(h_scr[0:tm, :], w_ref[:, gcols], preferred_element_type=F32).astype(BF16))

    ahead = project(0)
    for j in range(n_chunk):
        acc, gacc = ahead
        if j + 1 < n_chunk:
            ahead = project(j + 1)
        for half in range(2):
            blk = 2 * j + half
            lanes = slice(half * LANES, (half + 1) * LANES)
            g_ref[blk] = gacc[:, lanes]
            ext = ext_scr.at[half]
            ext[left * SUBLANES:left * SUBLANES + tm, :] = acc[0:tm, lanes]
            for k in range(left):
                src = acc[tm - (left - k) * SUBLANES:tm - (left - k - 1) * SUBLANES, lanes]
                before = jnp.where(has_prev, acc[tm + k:tm + k + 1, lanes], 0.0)
                ext[k * SUBLANES:(k + 1) * SUBLANES, :] = jnp.where(
                    sub == 0, before, pltpu.roll(src, 1, 0))
            for k in range(CONV_W - 1 - left):
                src = acc[k * SUBLANES:(k + 1) * SUBLANES, lanes]
                after = jnp.where(has_next, acc[tm + left + k:tm + left + k + 1, lanes], 0.0)
                ext[(left + k) * SUBLANES + tm:(left + k + 1) * SUBLANES + tm, :] = jnp.where(
                    sub == SUBLANES - 1, after, pltpu.roll(src, SUBLANES - 1, 0))
            xc = cb_ref[blk:blk + 1, :]
            for tap in range(CONV_W):
                xc = xc + ext[tap * SUBLANES:tap * SUBLANES + tm, :] * cw_ref[tap, blk:blk + 1, :]
            xc_ref[blk] = xc
            _lru_coeff_block(blk, xc, wg_ref, bg_ref, kh_ref, a_scr, u_scr)

    _lru_scan(a_scr, u_scr, hb_ref, carry_scr, tm, True)

    @pl.when(i == pl.num_programs(1) - 1)
    def _():
        hfin_ref[...] = carry_scr[...]


def _conv_halo_rows(x, tm):
    b, t, d = x.shape
    nblk = t // tm
    r = x.reshape(b, nblk, tm, d)
    left = CONV_W // 2
    prev = jnp.concatenate([jnp.zeros((b, 1, left, d), F32), r[:, :-1, tm - left:, :]], axis=1)
    nxt = jnp.concatenate([r[:, 1:, 0:CONV_W - 1 - left, :],
                           jnp.zeros((b, 1, CONV_W - 1 - left, d), F32)], axis=1)
    pad = jnp.zeros((b, nblk, 2 * SUBLANES - (CONV_W - 1), d), F32)
    return jnp.concatenate([prev, nxt, pad], axis=2)


def _lru_inproj(x, mul, add, w, cw, cb, wg, bg, kh, h0, *, tm):
    b, t, d = x.shape
    halo_rows = 2 * SUBLANES
    nblk, slab, state, gate = _lru_specs(b, t, tm, True)
    tile = (LRU_BLOCKS, tm, LRU_BW)
    slab_f32 = jax.ShapeDtypeStruct((b, nblk, LRU_BLOCKS, tm, LRU_BW), F32)
    return pl.pallas_call(
        _lru_inproj_kernel,
        grid=(b, nblk),
        in_specs=[pl.BlockSpec((None, tm, d), lambda i, j: (i, nblk - 1 - j, 0)),
                  pl.BlockSpec((None, None, halo_rows, d), lambda i, j: (i, nblk - 1 - j, 0, 0)),
                  _mod_spec(mul),
                  _mod_spec(add),
                  pl.BlockSpec((d, 2 * LRU_WIDTH), lambda i, j: (0, 0)),
                  pl.BlockSpec((CONV_W, LRU_BLOCKS, LRU_BW), lambda i, j: (0, 0, 0)),
                  pl.BlockSpec((LRU_BLOCKS, LRU_BW), lambda i, j: (0, 0))] + gate + [state],
        out_specs=[slab, slab, slab, state],
        out_shape=[slab_f32, jax.ShapeDtypeStruct(slab_f32.shape, BF16), slab_f32,
                   jax.ShapeDtypeStruct(h0.shape, F32)],
        scratch_shapes=[pltpu.VMEM((tm + halo_rows, d), BF16),
                        pltpu.VMEM((d // LANES, tm, LANES), F32),
                        pltpu.VMEM((2, tm + (CONV_W - 1) * SUBLANES, LRU_BW), F32),
                        pltpu.VMEM(tile, F32), pltpu.VMEM(tile, F32),
                        pltpu.VMEM((LRU_BLOCKS, SUBLANES, LRU_BW), F32)],
        compiler_params=_params(2, 56),
        name="lru_inproj",
    )(x, _conv_halo_rows(x, tm), mul.arr, add.arr, w, cw, cb, wg, bg, kh, h0)


def _gated_norm_proj(y, w_ref, gp_ref):
    z = jnp.dot(y, w_ref[...], preferred_element_type=F32)
    ms = jnp.mean(z * z, axis=-1, keepdims=True)
    return z * lax.rsqrt(ms + EPS) * gp_ref[...]


def _residual_out(x_ref, zn, o_ref, rows):
    o_ref[rows, :] = x_ref[rows, :] + zn


def _residual_out_interleaved(x_ref, zn, o_ref, z_scr):
    n_slab = zn.shape[1] // LANES
    seg = zn.shape[0] // SUBLANES
    for d in range(n_slab):
        z_scr[d] = zn[:, d * LANES:(d + 1) * LANES]
    for d in range(n_slab):
        cols = slice(d * LANES, (d + 1) * LANES)
        for s in range(SUBLANES):
            rows = slice(s * seg, (s + 1) * seg)
            o_ref[rows, cols] = x_ref[rows, cols] + z_scr[d, pl.ds(s, seg, stride=SUBLANES), :]


def _row_scale(x, dec):
    n = x.shape[1] // LANES
    return jnp.concatenate([x[:, i * LANES:(i + 1) * LANES] * dec for i in range(n)], axis=1)


def _chunk_pos():
    return lax.broadcasted_iota(jnp.int32, (RET_CHUNK, LANES), 0).astype(F32)


def _ret_fwd_kernel(lg_ref, q_ref, k_ref, v_ref, g_ref, sb_ref, s0_ref, w_ref, x_ref, gp_ref,
                    xo_ref, sfin_ref, s_scr, m_scr, dec_scr, y_scr, *, nch):
    c = pl.program_id(1)

    @pl.when(c == 0)
    def _():
        s_scr[...] = s0_ref[...]
        ri = lax.broadcasted_iota(jnp.int32, (RET_CHUNK, RET_CHUNK), 0)
        ci = lax.broadcasted_iota(jnp.int32, (RET_CHUNK, RET_CHUNK), 1)
        rel = (ri - ci).astype(F32)
        for h in range(RET_HEADS):
            mf = jnp.where(rel >= 0, jnp.exp(lg_ref[0, h] * jnp.maximum(rel, 0.0)), 0.0)
            mb = jnp.where(rel <= 0, jnp.exp(lg_ref[1, h] * jnp.maximum(-rel, 0.0)), 0.0)
            m_scr[h] = mf + mb
        pos = _chunk_pos()
        for h in range(RET_HEADS):
            dec_scr[h, 0] = jnp.exp(lg_ref[0, h] * (pos + 1.0)).astype(BF16)
            dec_scr[h, 1] = jnp.exp(lg_ref[1, h] * (RET_CHUNK - pos)).astype(BF16)
            dec_scr[h, 2] = jnp.exp(lg_ref[0, h] * (RET_CHUNK - 1.0 - pos)).astype(BF16)

    for ci, h in [(ci, h) for ci in range(nch) for h in range(RET_HEADS)]:
        rows = slice(ci * RET_CHUNK, (ci + 1) * RET_CHUNK)
        vcols = slice(h * RET_DV, (h + 1) * RET_DV)
        q = q_ref[rows, h * RET_DK:(h + 1) * RET_DK]
        k = k_ref[rows, h * RET_DK:(h + 1) * RET_DK]
        v = v_ref[rows, vcols]
        st = s_scr[h]
        s = lax.dot_general(q, k, (((1,), (1,)), ((), ())), preferred_element_type=F32)
        o = jnp.dot((s * m_scr[h]).astype(BF16), v, preferred_element_type=F32)
        o = jnp.dot(_row_scale(q, dec_scr[h, 0]), st.astype(BF16), preferred_element_type=F32) + o
        o = jnp.dot(_row_scale(q, dec_scr[h, 1]), sb_ref[ci, h], preferred_element_type=F32) + o
        kv = lax.dot_general(_row_scale(k, dec_scr[h, 2]), v, (((0,), (0,)), ((), ())),
                             preferred_element_type=F32)
        s_scr[h] = st * jnp.exp(lg_ref[0, h] * RET_CHUNK) + kv
        dlt = o - jnp.mean(o, axis=-1, keepdims=True)
        var = jnp.mean(dlt * dlt, axis=-1, keepdims=True)
        gh = g_ref[rows, vcols]
        y_scr[rows, vcols] = ((dlt * lax.rsqrt(var + EPS)).astype(BF16)
                              * (gh * (jnp.tanh(gh) + 1.0)))
        if h == RET_HEADS - 1:
            _residual_out(x_ref, _gated_norm_proj(y_scr[rows, :], w_ref, gp_ref), xo_ref, rows)

    @pl.when(c == pl.num_programs(1) - 1)
    def _():
        sfin_ref[...] = s_scr[...]


def _state_spec():
    return pl.BlockSpec((None, RET_HEADS, RET_DK, RET_DV), lambda i, j: (i, 0, 0, 0))


def _step_chunks(n_chunks, wanted):
    return wanted if n_chunks % wanted == 0 else 1


def _ret_fwd(p, sb, lg, s0, w_out, x, gp):
    b, t, _ = p.shape
    d = x.shape[-1]
    nc = t // RET_CHUNK
    nch = _step_chunks(nc, RET_FWD_STEP_CHUNKS)
    rows = nch * RET_CHUNK
    fwd = lambda blk: (lambda i, j: (i, j, blk))
    resid = pl.BlockSpec((None, rows, d), fwd(0))
    return pl.pallas_call(
        functools.partial(_ret_fwd_kernel, nch=nch),
        grid=(b, nc // nch),
        in_specs=[_smem_spec(),
                  pl.BlockSpec((None, rows, RET_QK), fwd(0)),
                  pl.BlockSpec((None, rows, RET_QK), fwd(1)),
                  pl.BlockSpec((None, rows, RET_V), fwd(1)),
                  pl.BlockSpec((None, rows, RET_V), fwd(2)),
                  pl.BlockSpec((None, nch, RET_HEADS, RET_DK, RET_DV),
                               lambda i, j: (i, j, 0, 0, 0)),
                  _state_spec(),
                  pl.BlockSpec((RET_V, d), lambda i, j: (0, 0)),
                  resid,
                  _mod_spec(gp)],
        out_specs=[resid, _state_spec()],
        out_shape=[jax.ShapeDtypeStruct((b, t, d), F32),
                   jax.ShapeDtypeStruct(s0.shape, F32)],
        scratch_shapes=[pltpu.VMEM((RET_HEADS, RET_DK, RET_DV), F32),
                        pltpu.VMEM((RET_HEADS, RET_CHUNK, RET_CHUNK), F32),
                        pltpu.VMEM((RET_HEADS, 3, RET_CHUNK, LANES), BF16),
                        pltpu.VMEM((rows, RET_V), BF16)],
        compiler_params=_params(2, 56),
        name="ret_fwd",
    )(lg, p, p, p, p, sb, s0, w_out, x, gp.arr)


def _lru_coeff_block(nb, xc, wg_ref, bg_ref, kh_ref, a_scr, u_scr):
    res = jnp.dot(xc.astype(BF16), wg_ref[nb], preferred_element_type=F32)
    t_r = jnp.tanh(res[:, :LRU_BW] + bg_ref[0, nb:nb + 1, :])
    t_g = jnp.tanh(res[:, LRU_BW:] + bg_ref[1, nb:nb + 1, :])
    kh = kh_ref[nb:nb + 1, :]
    a = jnp.exp2(t_r * kh + kh)
    a_scr[nb] = a
    om = 0.25 - 0.25 * (a * a)
    half_root = jnp.where(om > 0.0, om * lax.rsqrt(om), 0.0)
    u_scr[nb] = half_root * (t_g + 1.0) * xc


def _lru_scan(a_scr, u_scr, dst_ref, carry_scr, tt, reverse):
    n = tt // SUBLANES
    shape = (LRU_BLOCKS, SUBLANES, LRU_BW)
    sub = lax.broadcasted_iota(jnp.int32, shape, 1)
    steps = [slice(g * SUBLANES, (g + 1) * SUBLANES) for g in range(n)]
    if reverse:
        steps = steps[::-1]

    a_seg = a_scr[:, steps[0], :]
    h_seg = u_scr[:, steps[0], :]
    for rows in steps[1:]:
        a = a_scr[:, rows, :]
        h_seg = a * h_seg + u_scr[:, rows, :]
        a_seg = a_seg * a
    for d in (1, 2, 4):
        ok = (sub < SUBLANES - d) if reverse else (sub >= d)
        shift = SUBLANES - d if reverse else d
        a_s = jnp.where(ok, pltpu.roll(a_seg, shift, 1), 1.0)
        h_s = jnp.where(ok, pltpu.roll(h_seg, shift, 1), 0.0)
        h_seg = a_seg * h_s + h_seg
        a_seg = a_seg * a_s
    carry = carry_scr[...]
    h_end = a_seg * carry + h_seg
    if reverse:
        h_in = jnp.where(sub == SUBLANES - 1, carry, pltpu.roll(h_end, SUBLANES - 1, 1))
        last = h_end[:, 0:1, :]
    else:
        h_in = jnp.where(sub == 0, carry, pltpu.roll(h_end, 1, 1))
        last = h_end[:, SUBLANES - 1:SUBLANES, :]
    carry_scr[...] = jnp.broadcast_to(last, shape)

    h = h_in
    for rows in steps:
        h = a_scr[:, rows, :] * h + u_scr[:, rows, :]
        dst_ref[:, rows, :] = h


def _lru_fwd_kernel(xc_ref, wg_ref, bg_ref, kh_ref, h0_ref, hb_ref, g_ref, w_ref, x_ref, gp_ref,
                    xo_ref, hfin_ref, a_scr, u_scr, carry_scr, y_scr, z_scr, *, tt):
    i = pl.program_id(1)

    @pl.when(i == 0)
    def _():
        carry_scr[...] = h0_ref[...]

    for nb in range(LRU_BLOCKS):
        _lru_coeff_block(nb, xc_ref[nb], wg_ref, bg_ref, kh_ref, a_scr, u_scr)
    _lru_scan(a_scr, u_scr, u_scr, carry_scr, tt, False)
    for nb in range(LRU_BLOCKS):
        gh = g_ref[nb]
        y = (u_scr[nb] + hb_ref[nb]).astype(BF16) * (gh * (jnp.tanh(gh) + 1.0))
        y_scr[:, nb * LRU_BW:(nb + 1) * LRU_BW] = y
    _residual_out_interleaved(x_ref, _gated_norm_proj(y_scr[...], w_ref, gp_ref), xo_ref, z_scr)

    @pl.when(i == pl.num_programs(1) - 1)
    def _():
        hfin_ref[...] = carry_scr[...]


def _lru_specs(b, t, tt, reverse):
    nblk = t // tt
    tidx = (lambda j: nblk - 1 - j) if reverse else (lambda j: j)
    slab = pl.BlockSpec((None, None, LRU_BLOCKS, tt, LRU_BW), lambda i, j: (i, tidx(j), 0, 0, 0))
    state = pl.BlockSpec((None, LRU_BLOCKS, SUBLANES, LRU_BW), lambda i, j: (i, 0, 0, 0))
    gate = [pl.BlockSpec((LRU_BLOCKS, LRU_BW, 2 * LRU_BW), lambda i, j: (0, 0, 0)),
            pl.BlockSpec((2, LRU_BLOCKS, LRU_BW), lambda i, j: (0, 0, 0)),
            pl.BlockSpec((LRU_BLOCKS, LRU_BW), lambda i, j: (0, 0))]
    return nblk, slab, state, gate


def _lru_fwd(xc, wg, bg, kh, h0, hb, g, w_out, x, gp, *, tt):
    b, t, d = x.shape
    nblk, slab, state, gate = _lru_specs(b, t, tt, False)
    tile = (LRU_BLOCKS, tt, LRU_BW)
    resid = pl.BlockSpec((None, tt, d), lambda i, j: (i, j, 0))
    return pl.pallas_call(
        functools.partial(_lru_fwd_kernel, tt=tt),
        grid=(b, nblk),
        in_specs=[slab] + gate + [state, slab, slab,
                                  pl.BlockSpec((LRU_WIDTH, d), lambda i, j: (0, 0)),
                                  resid,
                                  _mod_spec(gp)],
        out_specs=[resid, state],
        out_shape=[jax.ShapeDtypeStruct((b, t, d), F32),
                   jax.ShapeDtypeStruct(h0.shape, F32)],
        scratch_shapes=[pltpu.VMEM(tile, F32), pltpu.VMEM(tile, F32),
                        pltpu.VMEM((LRU_BLOCKS, SUBLANES, LRU_BW), F32),
                        pltpu.VMEM((tt, LRU_WIDTH), BF16),
                        pltpu.VMEM((d // LANES, tt, LANES), F32)],
        compiler_params=_params(2, 56),
        name="lru_fwd",
    )(xc, wg, bg, kh, h0, hb, g, w_out, x, gp.arr)


def _rope_tables(n_tok):
    n_rows = n_tok // GRID_W
    row = jnp.repeat(jnp.arange(n_rows, dtype=F32), GRID_W)
    col = jnp.tile(jnp.arange(GRID_W, dtype=F32), n_rows)
    n_freq = RET_DK // 4
    inv = ROPE_BASE ** (-jnp.arange(n_freq, dtype=F32) / n_freq)
    ang = jnp.concatenate([row[:, None] * inv, col[:, None] * inv], axis=-1)
    return jnp.cos(ang), jnp.sin(ang)


def _ret_w_prep_kernel(w_ref, m_ref, o_ref):
    o_ref[...] = jnp.dot(w_ref[...].astype(BF16), m_ref[...],
                         preferred_element_type=F32).astype(BF16)


def _ret_w_prep(w_in_all, layer):
    d = w_in_all.shape[1]
    src = np.arange(RET_DK)
    perm = np.zeros((RET_DK, RET_DK), np.float32)
    perm[src, (src % 2) * HALF_DK + src // 2] = 1.0
    heads = RET_DV // RET_DK
    cols = heads * RET_DK
    eye = np.eye(cols, dtype=np.float32)
    mats = jnp.asarray(np.stack([np.kron(np.eye(heads, dtype=np.float32), perm), eye, 0.5 * eye]),
                       BF16)
    qk_blocks = 2 * RET_QK // cols
    v_blocks = RET_V // cols

    def which(j):
        return (j >= qk_blocks).astype(jnp.int32) + (j >= qk_blocks + v_blocks).astype(jnp.int32)

    return pl.pallas_call(
        _ret_w_prep_kernel,
        grid=(RET_IN // cols,),
        in_specs=[pl.BlockSpec((None, d, cols), lambda j: (layer, 0, j)),
                  pl.BlockSpec((None, cols, cols), lambda j: (which(j), 0, 0))],
        out_specs=pl.BlockSpec((d, cols), lambda j: (0, j)),
        out_shape=jax.ShapeDtypeStruct((d, RET_IN), BF16),
        compiler_params=_params(1, 32),
        name="ret_w_prep",
    )(w_in_all, mats)


def _retention_layer(x, s_ctx, mods, cos, sin, w_in_all, layer, log_decay, gn_g, w_out):
    (mul_l, add_l, gp_l), (mul_c, add_c, gp_c) = mods
    b = x.shape[0]
    lc = s_ctx.shape[1]
    w = _ret_w_prep(w_in_all, layer)
    lg = -jnp.abs(log_decay.astype(F32))
    w_o = (gn_g.astype(F32)[:, None] * w_out).astype(BF16)

    zero = jnp.zeros((b, RET_HEADS, RET_DK, RET_DV), F32)
    p_c, sb_c, sb_fin = _ret_inproj(s_ctx, mul_c, add_c, w, cos[:lc], sin[:lc], lg, zero,
                                    rope=False, tm=lc)
    p_l, sb_l, _ = _ret_inproj(x, mul_l, add_l, w, cos, sin, lg, sb_fin, rope=True, tm=ROW_TILE)
    s_ctx, sf_c = _ret_fwd(p_c, sb_c, lg, zero, w_o, s_ctx, gp_c)
    x, _ = _ret_fwd(p_l, sb_l, lg, sf_c, w_o, x, gp_l)
    return x, s_ctx


def _lru_layer(x, s_ctx, mods, w_in, conv_w, conv_b, w_a, b_a, w_x, b_x, lam, w_out):
    (mul_l, add_l, gp_l), (mul_c, add_c, gp_c) = mods
    b = x.shape[0]
    lc = s_ctx.shape[1]
    w = jnp.concatenate([w_in[:, :LRU_WIDTH], 0.5 * w_in[:, LRU_WIDTH:]], axis=1).astype(BF16)
    w_o = w_out.astype(BF16)
    cw = conv_w.astype(F32).reshape(CONV_W, LRU_BLOCKS, LRU_BW)
    cb = conv_b.astype(F32).reshape(LRU_BLOCKS, LRU_BW)
    kh = (-0.5 * LRU_C * LOG2_E) * jax.nn.softplus(-lam.astype(F32))

    def gate_params(d):
        wg = (0.5 * jnp.concatenate([w_a[d], w_x[d]], axis=-1)).astype(BF16)
        bg = (0.5 * jnp.stack([b_a[d], b_x[d]])).astype(F32).reshape(2, LRU_BLOCKS, LRU_BW)
        return wg, bg, kh[d].reshape(LRU_BLOCKS, LRU_BW)

    zero = jnp.zeros((b, LRU_BLOCKS, SUBLANES, LRU_BW), F32)
    xc_c, g_c, hb_c, hb_fin = _lru_inproj(s_ctx, mul_c, add_c, w, cw, cb, *gate_params(1), zero,
                                          tm=lc)
    xc_l, g_l, hb_l, _ = _lru_inproj(x, mul_l, add_l, w, cw, cb, *gate_params(1), hb_fin,
                                     tm=ROW_TILE)
    s_ctx, hf_fin = _lru_fwd(xc_c, *gate_params(0), zero, hb_c, g_c, w_o, s_ctx, gp_c, tt=lc)
    x, _ = _lru_fwd(xc_l, *gate_params(0), hf_fin, hb_l, g_l, w_o, x, gp_l, tt=ROW_TILE)
    return x, s_ctx


def kernel(x, c, ctx, c_ctx, mod_w, mod_b, norm_pre, norm_post, ret_w_in, ret_log_decay, ret_gn,
           ret_w_out, lru_w_in, lru_conv_w, lru_conv_b, lru_w_a, lru_b_a, lru_w_x, lru_b_x,
           lru_lambda, lru_w_out):
    b, n_tok, d = x.shape
    depth = mod_w.shape[0]
    assert n_tok % ROW_TILE == 0 and ROW_TILE % RET_CHUNK == 0
    assert ctx.shape[1] % RET_CHUNK == 0 and ctx.shape[1] <= ROW_TILE
    cos, sin = _rope_tables(n_tok)

    rows = -(-(b + 1) // SUBLANES) * SUBLANES
    cc = jnp.concatenate([c, c_ctx[None, :], jnp.zeros((rows - b - 1, d), F32)], axis=0)
    mod = _modulation(cc, mod_w, mod_b, norm_pre, norm_post)

    s_ctx = ctx
    for i in range(depth):
        mods = [tuple(_ModVec(mod, i, slab, row) for slab in (MOD_SCALE, MOD_SHIFT, MOD_GATE))
                for row in (None, b)]
        j = i // 2
        if i % 2 == 0:
            x, s_ctx = _retention_layer(x, s_ctx, mods, cos, sin, ret_w_in, j, ret_log_decay[j],
                                        ret_gn[j], ret_w_out[j])
        else:
            x, s_ctx = _lru_layer(x, s_ctx, mods, lru_w_in[j], lru_conv_w[j], lru_conv_b[j],
                                  lru_w_a[j], lru_b_a[j], lru_w_x[j], lru_b_x[j],
                                  lru_lambda[j], lru_w_out[j])
    return x
```
